```python
import math
import jax
import jax.numpy as jnp
from jax import lax
import numpy as np

D_MODEL = 1024
BATCH = 8
SEQ = 2048
DEPTH = 2

GRID_W = 64
CTX_LEN = 256
HEAD_DIM = 64
N_MIXERS = 4
GROUP_WIDTH = D_MODEL // N_MIXERS
MIX_WIDTH = N_MIXERS * GROUP_WIDTH
NA_HEADS = GROUP_WIDTH // HEAD_DIM
NA_ROWS = 8
NA_COLS = 16
DIFF_HEADS = GROUP_WIDTH // HEAD_DIM
DIFF_QK_DIM = HEAD_DIM // 2
POOL_WINDOWS = (2, 4, 8, 16)
POOL_GROUP = GROUP_WIDTH // len(POOL_WINDOWS)
FFT_GROUPS = 4
FFT_GROUP = GROUP_WIDTH // FFT_GROUPS
OFF_NA_Q = 0 * GROUP_WIDTH
OFF_NA_K = 1 * GROUP_WIDTH
OFF_NA_V = 2 * GROUP_WIDTH
OFF_DF_Q = 3 * GROUP_WIDTH
OFF_DF_K = 4 * GROUP_WIDTH
OFF_DF_V = 5 * GROUP_WIDTH
OFF_POOL = 6 * GROUP_WIDTH
OFF_FFT = 7 * GROUP_WIDTH
IN_WIDTH = 8 * GROUP_WIDTH
N_EXPERTS = 32
TOP_K = 4
D_FF = D_MODEL
SWIGLU_ALPHA = 1.702
SWIGLU_LIMIT = 7.0
EXPERT_BLOCK = 256
Q_BLOCK = 128
ROPE_THETA = 10000.0
NORM_EPS = 1e-6
MASK_VALUE = -1e30

kernel_name = 'hybrid_prefix_dit_block'


def rms_norm(x, g):
    xf = x.astype(jnp.float32)
    y = xf * lax.rsqrt(jnp.mean(xf * xf, axis=-1, keepdims=True) + NORM_EPS)
    return (y * g.astype(jnp.float32)).astype(x.dtype)


def modulate(x, g, shift, scale):
    return rms_norm(x, g) * (1 + scale) + shift


def split_heads(t, n_heads):
    b, l, _ = t.shape
    return t.reshape(b, l, n_heads, -1).transpose(0, 2, 1, 3)


def merge_heads(t):
    b, h, l, d = t.shape
    return t.transpose(0, 2, 1, 3).reshape(b, l, h * d)


def diff_split(t):
    b, l, _ = t.shape
    return t.reshape(b, l, DIFF_HEADS, 2, DIFF_QK_DIM).transpose(0, 2, 3, 1, 4)


def axial_rope(t):
    l = t.shape[-2]
    pos = jnp.arange(l)
    half = DIFF_QK_DIM // 2
    inv = ROPE_THETA ** (-jnp.arange(0, half, 2, dtype=jnp.float32) / half)
    tf = t.astype(jnp.float32)
    outs = []
    for axis_pos, seg in ((pos // GRID_W, tf[..., :half]), (pos % GRID_W, tf[..., half:])):
        ang = axis_pos.astype(jnp.float32)[:, None] * inv
        cos, sin = jnp.cos(ang), jnp.sin(ang)
        s1, s2 = seg[..., :half // 2], seg[..., half // 2:]
        outs += [s1 * cos - s2 * sin, s2 * cos + s1 * sin]
    return jnp.concatenate(outs, axis=-1).astype(t.dtype)


def neighbourhood_attention(q, k, v, kc, vc, rpb):
    b, h, l, d = q.shape
    rows = l // GRID_W
    kr = min(NA_ROWS, rows)
    qg = q.reshape(b, h, rows, GRID_W, d)
    kg = k.reshape(b, h, rows, GRID_W, d)
    vg = v.reshape(b, h, rows, GRID_W, d)
    r = jnp.arange(rows)
    row_start = jnp.clip(r - NA_ROWS // 2, 0, rows - kr)
    row_idx = row_start[:, None] + jnp.arange(kr)
    k_rows = kg[:, :, row_idx]
    v_rows = vg[:, :, row_idx]
    col = jnp.arange(GRID_W)
    col_start = jnp.clip(col - NA_COLS // 2, 0, GRID_W - NA_COLS)
    in_win = (col[None, :] >= col_start[:, None]) & (col[None, :] < col_start[:, None] + NA_COLS)
    rel_r = row_idx - r[:, None] + NA_ROWS - 1
    rel_c = jnp.clip(col[None, :] - col[:, None], 1 - NA_COLS, NA_COLS - 1) + NA_COLS - 1
    bias = rpb.astype(jnp.float32)[:, rel_r[:, None, :, None], rel_c[None, :, None, :]]
    scale = d ** -0.5
    s_loc = jnp.einsum('bhrqd,bhrkcd->bhrqkc', qg, k_rows).astype(jnp.float32) * scale + bias
    s_loc = jnp.where(in_win[:, None, :], s_loc, MASK_VALUE)
    s_ctx = jnp.einsum('bhrqd,bhjd->bhrqj', qg, kc).astype(jnp.float32) * scale
    n_loc = kr * GRID_W
    s_all = jnp.concatenate([s_loc.reshape(b, h, rows, GRID_W, n_loc), s_ctx], axis=-1)
    p = jax.nn.softmax(s_all, axis=-1).astype(v.dtype)
    p_loc = p[..., :n_loc].reshape(b, h, rows, GRID_W, kr, GRID_W)
    out = (jnp.einsum('bhrqkc,bhrkcd->bhrqd', p_loc, v_rows)
           + jnp.einsum('bhrqj,bhjd->bhrqd', p[..., n_loc:], vc))
    return out.reshape(b, h, l, d)


def context_attention(q, k, v):
    s = jnp.einsum('bhqd,bhkd->bhqk', q, k).astype(jnp.float32) * q.shape[-1] ** -0.5
    return jnp.einsum('bhqk,bhkd->bhqd', jax.nn.softmax(s, axis=-1).astype(v.dtype), v)


def differential_attention(q, k, v, lam):
    s = jnp.einsum('bhiqd,bhikd->bhiqk', q, k).astype(jnp.float32) * q.shape[-1] ** -0.5
    p = jax.nn.softmax(s, axis=-1)
    a = p[:, :, 0] - lam * p[:, :, 1]
    return jnp.einsum('bhqk,bhkd->bhqd', a.astype(v.dtype), v)


def multiscale_pool(t, w, scale):
    b, l, _ = t.shape
    tf = t.astype(jnp.float32).reshape(b, l, len(POOL_WINDOWS), POOL_GROUP)
    cs = jnp.concatenate([jnp.zeros((b, 1, len(POOL_WINDOWS), POOL_GROUP), jnp.float32),
                          jnp.cumsum(tf, axis=1)], axis=1)
    pos = jnp.arange(l)
    outs = []
    for g, win in enumerate(POOL_WINDOWS):
        lo = jnp.clip(pos - win // 2, 0, l)
        hi = jnp.clip(pos - win // 2 + win, 0, l)
        csg = cs[:, :, g]
        mean = (csg[:, hi] - csg[:, lo]) / (hi - lo).astype(jnp.float32)[:, None]
        outs.append(mean - tf[:, :, g])
    m = jnp.stack(outs, axis=2)
    y = jnp.einsum('blgc,gce->blge', m, w.astype(jnp.float32)).reshape(b, l, GROUP_WIDTH)
    return (y * scale.astype(jnp.float32)).astype(t.dtype)


def fourier_mix(t, w):
    b, l, _ = t.shape
    tg = t.astype(jnp.float32).reshape(b, l, FFT_GROUPS, FFT_GROUP)
    f = jnp.fft.fft2(tg, axes=(1, 3), norm='ortho').real
    y = jnp.einsum('blgc,gce->blge', f, w.astype(jnp.float32)).reshape(b, l, GROUP_WIDTH)
    return y.astype(t.dtype)


def clamped_swiglu(u):
    glu = jnp.minimum(u[..., ::2], SWIGLU_LIMIT)
    lin = jnp.clip(u[..., 1::2], -SWIGLU_LIMIT, SWIGLU_LIMIT)
    return glu * jax.nn.sigmoid(SWIGLU_ALPHA * glu) * (lin + 1)


def expert_ffn(h, router_w, router_b, w1, b1, w2, b2):
    n, dm = h.shape
    logits = (h @ router_w).astype(jnp.float32) + router_b.astype(jnp.float32)
    top_val, top_idx = lax.top_k(logits, TOP_K)
    gate = jax.nn.softmax(top_val, axis=-1)
    n_assign = n * TOP_K
    flat_e = top_idx.reshape(n_assign)
    order = jnp.argsort(flat_e)
    sorted_e = flat_e[order]
    counts = jnp.bincount(flat_e, length=N_EXPERTS)
    padded = (counts + EXPERT_BLOCK - 1) // EXPERT_BLOCK * EXPERT_BLOCK
    group_start = jnp.cumsum(counts) - counts
    padded_end = jnp.cumsum(padded)
    padded_start = padded_end - padded
    dest = padded_start[sorted_e] + jnp.arange(n_assign) - group_start[sorted_e]
    n_blocks = -(-n_assign // EXPERT_BLOCK) + N_EXPERTS
    slots = n_blocks * EXPERT_BLOCK
    slot_tok = jnp.zeros((slots,), jnp.int32).at[dest].set((order // TOP_K).astype(jnp.int32))
    slot_w = jnp.zeros((slots,), jnp.float32).at[dest].set(gate.reshape(n_assign)[order])
    block_e = jnp.minimum(jnp.searchsorted(padded_end, jnp.arange(n_blocks) * EXPERT_BLOCK, side='right'),
                          N_EXPERTS - 1)

    def run_block(args):
        e, tok = args
        u = h[tok] @ w1[e] + b1[e]
        return clamped_swiglu(u) @ w2[e] + b2[e]

    y = lax.map(run_block, (block_e, slot_tok.reshape(n_blocks, EXPERT_BLOCK)))
    y = y.reshape(slots, dm) * slot_w[:, None].astype(h.dtype)
    return jnp.zeros_like(h).at[slot_tok].add(y)


def hybrid_layer(x, cx, c_act, cctx_act, p, layer_idx, ctx_out):
    b, l, dm = x.shape
    lc = cx.shape[1]
    gw = GROUP_WIDTH
    mod_x = (c_act @ p['w_ada'] + p['b_ada'])[:, None, :]
    mod_c = cctx_act @ p['w_ada'] + p['b_ada']
    sh1, sc1, g1, sh2, sc2, g2 = jnp.split(mod_x, 6, axis=-1)
    ch1, cs1, cg1, ch2, cs2, cg2 = jnp.split(mod_c, 6, axis=-1)
    w_in = p['w_in']

    hx = modulate(x, p['g1'], sh1, sc1)
    hc = modulate(cx, p['g1'], ch1, cs1)
    px = hx @ w_in
    if ctx_out:
        pc = hc @ w_in
        ccol = lambda off: pc[..., off:off + gw]
    else:
        ccol = lambda off: hc @ w_in[:, off:off + gw]
    xcol = lambda off: px[..., off:off + gw]

    qa = rms_norm(split_heads(xcol(OFF_NA_Q), NA_HEADS), p['na_qg'])
    ka = rms_norm(split_heads(xcol(OFF_NA_K), NA_HEADS), p['na_kg'])
    va = split_heads(xcol(OFF_NA_V), NA_HEADS)
    kca = rms_norm(split_heads(ccol(OFF_NA_K), NA_HEADS), p['na_kg'])
    vca = split_heads(ccol(OFF_NA_V), NA_HEADS)
    ya = merge_heads(neighbourhood_attention(qa, ka, va, kca, vca, p['na_rpb']))

    lam_init = 0.8 - 0.6 * math.exp(-0.3 * layer_idx)
    lam = (jnp.exp(jnp.sum(p['lq1'].astype(jnp.float32) * p['lk1'].astype(jnp.float32)))
           - jnp.exp(jnp.sum(p['lq2'].astype(jnp.float32) * p['lk2'].astype(jnp.float32))) + lam_init)
    qd = axial_rope(rms_norm(diff_split(xcol(OFF_DF_Q)), p['df_qg']))
    kd = axial_rope(rms_norm(diff_split(xcol(OFF_DF_K)), p['df_kg']))
    vd = split_heads(xcol(OFF_DF_V), DIFF_HEADS)
    kcd = rms_norm(diff_split(ccol(OFF_DF_K)), p['df_kg'])
    vcd = split_heads(ccol(OFF_DF_V), DIFF_HEADS)
    keys = jnp.concatenate([kcd, kd], axis=3)
    vals = jnp.concatenate([vcd, vd], axis=2)
    nb = l // Q_BLOCK
    q_blocks = jnp.moveaxis(qd.reshape(b, DIFF_HEADS, 2, nb, Q_BLOCK, DIFF_QK_DIM), 3, 0)
    yd = lax.map(lambda qb: differential_attention(qb, keys, vals, lam), q_blocks)
    yd = jnp.moveaxis(yd, 0, 2).reshape(b, DIFF_HEADS, l, HEAD_DIM)
    yd = merge_heads(rms_norm(yd, p['df_subln']) * (1 - lam_init))

    yb = multiscale_pool(xcol(OFF_POOL), p['pool_w'], p['pool_scale'])
    yf = fourier_mix(xcol(OFF_FFT), p['fft_w'])
    x = x + g1 * (jnp.concatenate([ya, yd, yb, yf], axis=-1) @ p['w_out'])

    if ctx_out:
        qca = rms_norm(split_heads(ccol(OFF_NA_Q), NA_HEADS), p['na_qg'])
        yca = merge_heads(context_attention(qca, kca, vca))
        qcd = rms_norm(diff_split(ccol(OFF_DF_Q)), p['df_qg'])
        ycd = merge_heads(rms_norm(differential_attention(qcd, kcd, vcd, lam), p['df_subln']) * (1 - lam_init))
        ycb = multiscale_pool(ccol(OFF_POOL), p['pool_w'], p['pool_scale'])
        ycf = fourier_mix(ccol(OFF_FFT), p['fft_w'])
        cx = cx + cg1 * (jnp.concatenate([yca, ycd, ycb, ycf], axis=-1) @ p['w_out'])

    hx2 = modulate(x, p['g2'], sh2, sc2).reshape(b * l, dm)
    moe_args = (p['router_w'], p['router_b'], p['w1'], p['b1'], p['w2'], p['b2'])
    if ctx_out:
        hc2 = modulate(cx, p['g2'], ch2, cs2).reshape(b * lc, dm)
        y = expert_ffn(jnp.concatenate([hx2, hc2], axis=0), *moe_args)
        x = x + g2 * y[:b * l].reshape(b, l, dm)
        cx = cx + cg2 * y[b * l:].reshape(b, lc, dm)
    else:
        x = x + g2 * expert_ffn(hx2, *moe_args).reshape(b, l, dm)
        cx = None
    return x, cx


def setup_inputs(seed: int = 0) -> dict:
    key = jax.random.key(seed)
    ks = jax.random.split(key, 32)
    f32 = jnp.float32
    nrm = lambda k, shape, s: jax.random.normal(k, shape, f32) * s
    L, D, E, F = DEPTH, D_MODEL, N_EXPERTS, D_FF
    return {
        'x': nrm(ks[0], (BATCH, SEQ, D), 1.0),
        'c': nrm(ks[1], (BATCH, D), 1.0),
        'ctx': nrm(ks[2], (BATCH, CTX_LEN, D), 1.0),
        'c_ctx': nrm(ks[3], (D,), 1.0),
        'w_ada': nrm(ks[4], (L, D, 6 * D), 0.5 * D ** -0.5),
        'b_ada': nrm(ks[5], (L, 6 * D), 0.02),
        'g_norm1': 1.0 + nrm(ks[6], (L, D), 0.05),
        'g_norm2': 1.0 + nrm(ks[7], (L, D), 0.05),
        'w_in': nrm(ks[8], (L, D, IN_WIDTH), D ** -0.5),
        'w_out': nrm(ks[9], (L, MIX_WIDTH, D), MIX_WIDTH ** -0.5),
        'na_q_gain': 1.0 + nrm(ks[10], (L, HEAD_DIM), 0.05),
        'na_k_gain': 1.0 + nrm(ks[11], (L, HEAD_DIM), 0.05),
        'na_rpb': nrm(ks[12], (L, NA_HEADS, 2 * NA_ROWS - 1, 2 * NA_COLS - 1), 0.5),
        'diff_q_gain': 1.0 + nrm(ks[13], (L, DIFF_QK_DIM), 0.05),
        'diff_k_gain': 1.0 + nrm(ks[14], (L, DIFF_QK_DIM), 0.05),
        'diff_lambda_q1': nrm(ks[15], (L, DIFF_QK_DIM), 0.1),
        'diff_lambda_k1': nrm(ks[16], (L, DIFF_QK_DIM), 0.1),
        'diff_lambda_q2': nrm(ks[17], (L, DIFF_QK_DIM), 0.1),
        'diff_lambda_k2': nrm(ks[18], (L, DIFF_QK_DIM), 0.1),
        'diff_subln': 1.0 + nrm(ks[19], (L, HEAD_DIM), 0.05),
        'pool_w': nrm(ks[20], (L, len(POOL_WINDOWS), POOL_GROUP, POOL_GROUP), POOL_GROUP ** -0.5),
        'pool_scale': 1.0 + nrm(ks[21], (L, GROUP_WIDTH), 0.1),
        'fft_w': nrm(ks[22], (L, FFT_GROUPS, FFT_GROUP, FFT_GROUP), FFT_GROUP ** -0.5),
        'router_w': nrm(ks[23], (L, D, E), D ** -0.5),
        'router_b': nrm(ks[24], (L, E), 0.01),
        'moe_w1': nrm(ks[25], (L, E, D, 2 * F), D ** -0.5),
        'moe_b1': nrm(ks[26], (L, E, 2 * F), 0.01),
        'moe_w2': nrm(ks[27], (L, E, F, D), F ** -0.5),
        'moe_b2': nrm(ks[28], (L, E, D), 0.01),
    }


def reference(x, c, ctx, c_ctx, w_ada, b_ada, g_norm1, g_norm2, w_in, w_out, na_q_gain, na_k_gain,
              na_rpb, diff_q_gain, diff_k_gain, diff_lambda_q1, diff_lambda_k1, diff_lambda_q2,
              diff_lambda_k2, diff_subln, pool_w, pool_scale, fft_w, router_w, router_b,
              moe_w1, moe_b1, moe_w2, moe_b2):
    c_act = jax.nn.silu(c)
    cctx_act = jax.nn.silu(c_ctx)
    cx = ctx
    for i in range(DEPTH):
        p = {
            'w_ada': w_ada[i], 'b_ada': b_ada[i], 'g1': g_norm1[i], 'g2': g_norm2[i],
            'w_in': w_in[i], 'w_out': w_out[i],
            'na_qg': na_q_gain[i], 'na_kg': na_k_gain[i], 'na_rpb': na_rpb[i],
            'df_qg': diff_q_gain[i], 'df_kg': diff_k_gain[i],
            'lq1': diff_lambda_q1[i], 'lk1': diff_lambda_k1[i],
            'lq2': diff_lambda_q2[i], 'lk2': diff_lambda_k2[i], 'df_subln': diff_subln[i],
            'pool_w': pool_w[i], 'pool_scale': pool_scale[i], 'fft_w': fft_w[i],
            'router_w': router_w[i], 'router_b': router_b[i],
            'w1': moe_w1[i], 'b1': moe_b1[i], 'w2': moe_w2[i], 'b2': moe_b2[i],
        }
        x, cx = hybrid_layer(x, cx, c_act, cctx_act, p, i, i < DEPTH - 1)
    return x
```

```python
import functools
import math

import numpy as np
import jax
import jax.numpy as jnp
from jax import lax
from jax.experimental import pallas as pl
from jax.experimental.pallas import tpu as pltpu

F32 = jnp.float32
BF16 = jnp.bfloat16
HIGHEST = lax.Precision.HIGHEST

D_MODEL = 1024
DEPTH = 2
GRID_W = 64
HEAD_DIM = 64
GROUP_WIDTH = 256
N_HEADS = GROUP_WIDTH // HEAD_DIM
DIFF_QK_DIM = HEAD_DIM // 2
NA_ROWS = 8
NA_COLS = 16
POOL_WINDOWS = (2, 4, 8, 16)
POOL_HALO = max(POOL_WINDOWS) // 2
N_EXPERTS = 32
TOP_K = 4
D_FF = D_MODEL
SWIGLU_ALPHA = 1.702
SWIGLU_LIMIT = 7.0
ROPE_THETA = 10000.0
NORM_EPS = 1e-6
MASK_VALUE = -1e30

LANES = 128
SUBLANES = 8
ROW_TILE = 512
TOKEN_TILE = 256
EXPERT_BLOCK = 256
Q_TILE = 256
NA_Q_ROWS = Q_TILE // GRID_W
NA_KEY_ROWS = NA_ROWS + NA_Q_ROWS - 1
FFT_TILE = 512
SLABS = D_MODEL // LANES
VMEM_LIMIT = 48 * 1024 * 1024


def _cparams(n_axes):
    return pltpu.CompilerParams(dimension_semantics=("arbitrary",) * n_axes,
                                vmem_limit_bytes=VMEM_LIMIT)


def _dot(a, b):
    return jnp.dot(a, b, preferred_element_type=F32)


def _dot_nt(a, b):
    return lax.dot_general(a, b, (((1,), (1,)), ((), ())), preferred_element_type=F32)


def _full(shape):
    zeros = (0,) * len(shape)
    return pl.BlockSpec(shape, lambda *_: zeros)


def _ada_kernel(c_ref, w_ref, b_ref, o_ref):
    c = c_ref[...]
    act = c * jax.nn.sigmoid(c)
    o_ref[0] = jnp.dot(act, w_ref[0], precision=HIGHEST, preferred_element_type=F32) + b_ref[0]


def _ada_call(cvec, w_ada, b_ada):
    depth, d, n = w_ada.shape
    r = cvec.shape[0]
    tn = 1024
    return pl.pallas_call(
        _ada_kernel,
        grid=(depth, n // tn),
        in_specs=[_full((r, d)),
                  pl.BlockSpec((1, d, tn), lambda l, j: (l, 0, j)),
                  pl.BlockSpec((1, 1, tn), lambda l, j: (l, 0, j))],
        out_specs=pl.BlockSpec((1, r, tn), lambda l, j: (l, 0, j)),
        out_shape=jax.ShapeDtypeStruct((depth, r, n), F32),
        compiler_params=_cparams(2),
        name="ada_mod",
    )(cvec, w_ada, b_ada.reshape(depth, 1, n))


def _seg_mean_sq(p, gmat_ref, width):
    sq = p * p
    hi = sq.astype(BF16)
    lo = (sq - hi.astype(F32)).astype(BF16)
    g = gmat_ref[...]
    return (_dot(hi, g) + _dot(lo, g)) * (1.0 / width)


def _in_kernel(x_ref, sh_ref, sc_ref, g_ref, w_ref, gains_ref, cos_ref, sin_ref, g64_ref, g32_ref,
               qkv_ref, pool_ref, fft_ref):
    x = x_ref[...]
    ms = jnp.mean(x * x, axis=-1, keepdims=True)
    y = x * lax.rsqrt(ms + NORM_EPS) * g_ref[...]
    h = (y * (1.0 + sc_ref[0]) + sh_ref[0]).astype(BF16)
    gw = GROUP_WIDTH

    def proj(g):
        return _dot(h, w_ref[:, g * gw:(g + 1) * gw])

    def put(g, val):
        qkv_ref[:, g * gw:(g + 1) * gw] = val.astype(BF16)

    def normed(p, gmat_ref, width, row):
        return p * lax.rsqrt(_seg_mean_sq(p, gmat_ref, width) + NORM_EPS) * gains_ref[row:row + 1, :]

    lane = lax.broadcasted_iota(jnp.int32, (1, gw), 1)
    first_half = (lane % 16) < 8

    def rope(p):
        rot = jnp.where(first_half, pltpu.roll(p, gw - 8, 1), pltpu.roll(p, 8, 1))
        return p * cos_ref[...] + rot * sin_ref[...]

    put(0, normed(proj(0), g64_ref, HEAD_DIM, 0))
    put(1, normed(proj(1), g64_ref, HEAD_DIM, 1))
    put(2, proj(2))
    put(3, rope(normed(proj(3), g32_ref, DIFF_QK_DIM, 2)))
    put(4, rope(normed(proj(4), g32_ref, DIFF_QK_DIM, 3)))
    put(5, proj(5))
    pool_ref[...] = proj(6)
    fft_ref[...] = proj(7).astype(BF16)


def _in_call(xc, mod3, g1, w_in_bf, gains, cos_t, sin_t, g64, g32, n_lat_rows, seq):
    rows, d = xc.shape
    tm = ROW_TILE
    n_lat_tiles = n_lat_rows // tm
    tiles_per_seq = seq // tm
    n_batch = n_lat_rows // seq
    gw = GROUP_WIDTH

    def mod_row(i):
        return jnp.minimum(i // tiles_per_seq, n_batch)

    def tab_row(i):
        return jnp.where(i < n_lat_tiles, i % tiles_per_seq, tiles_per_seq)

    return pl.pallas_call(
        _in_kernel,
        grid=(rows // tm,),
        in_specs=[pl.BlockSpec((tm, d), lambda i: (i, 0)),
                  pl.BlockSpec((1, 1, d), lambda i: (mod_row(i), 0, 0)),
                  pl.BlockSpec((1, 1, d), lambda i: (mod_row(i), 0, 1)),
                  _full((1, d)),
                  _full(w_in_bf.shape),
                  _full(gains.shape),
                  pl.BlockSpec((tm, gw), lambda i: (tab_row(i), 0)),
                  pl.BlockSpec((tm, gw), lambda i: (tab_row(i), 0)),
                  _full((gw, gw)),
                  _full((gw, gw))],
        out_specs=[pl.BlockSpec((tm, 6 * gw), lambda i: (i, 0)),
                   pl.BlockSpec((tm, gw), lambda i: (i, 0)),
                   pl.BlockSpec((tm, gw), lambda i: (i, 0))],
        out_shape=[jax.ShapeDtypeStruct((rows, 6 * gw), BF16),
                   jax.ShapeDtypeStruct((rows, gw), F32),
                   jax.ShapeDtypeStruct((rows, gw), BF16)],
        compiler_params=_cparams(1),
        name="in_proj",
    )(xc, mod3, mod3, g1, w_in_bf, gains, cos_t, sin_t, g64, g32)


def _lane_mask(width, start, size):
    lane = lax.broadcasted_iota(jnp.int32, (1, width), 1)
    return (lane >= start) & (lane < start + size)


def _softmax_pv(qm, keys, biases, vals):
    scores = []
    for k, bias in zip(keys, biases):
        s = _dot_nt(qm, k)
        scores.append(s if bias is None else s + bias)
    m = scores[0].max(axis=-1, keepdims=True)
    for s in scores[1:]:
        m = jnp.maximum(m, s.max(axis=-1, keepdims=True))
    o = None
    l = None
    for s, v in zip(scores, vals):
        e = jnp.exp(s - m)
        part = _dot(e.astype(BF16), v)
        o = part if o is None else o + part
        ls = e.sum(axis=-1, keepdims=True)
        l = ls if l is None else l + ls
    return o, l


def _na_heads(q, keys, bias_fn, vals):
    gw = GROUP_WIDTH
    acc = jnp.zeros((q.shape[0], gw), F32)
    for h in range(N_HEADS):
        mask = _lane_mask(gw, h * HEAD_DIM, HEAD_DIM)
        qm = jnp.where(mask, q, jnp.zeros_like(q))
        o, l = _softmax_pv(qm, keys, bias_fn(h), vals)
        acc = acc + jnp.where(mask, o / l, 0.0)
    return acc


def _lambda(lam_ref, lam_init):
    lv = lam_ref[...]
    d1 = jnp.sum(lv[0:1, :] * lv[1:2, :], axis=-1, keepdims=True)
    d2 = jnp.sum(lv[2:3, :] * lv[3:4, :], axis=-1, keepdims=True)
    return jnp.exp(d1) - jnp.exp(d2) + lam_init


def _diff_heads(q, keys, vals, lam, subln, lam_init):
    gw = GROUP_WIDTH
    acc = jnp.zeros((q.shape[0], gw), F32)
    nones = [None] * len(keys)
    for h in range(N_HEADS):
        outs = []
        for part in range(2):
            pm = _lane_mask(gw, h * HEAD_DIM + part * DIFF_QK_DIM, DIFF_QK_DIM)
            qm = jnp.where(pm, q, jnp.zeros_like(q))
            o, l = _softmax_pv(qm, keys, nones, vals)
            outs.append(o / l)
        hm = _lane_mask(gw, h * HEAD_DIM, HEAD_DIM)
        a = jnp.where(hm, outs[0] - lam * outs[1], 0.0)
        ms = jnp.sum(a * a, axis=-1, keepdims=True) * (1.0 / HEAD_DIM)
        acc = acc + a * lax.rsqrt(ms + NORM_EPS)
    return acc * subln * (1.0 - lam_init)


def _na_kernel(q_ref, k_ref, v_ref, kc_ref, vc_ref, bias_ref, o_ref, *, grid_rows):
    j = pl.program_id(1)
    key_row0 = jnp.clip(j * NA_Q_ROWS - NA_ROWS // 2, 0, grid_rows - NA_KEY_ROWS)
    ks = pl.multiple_of(key_row0 * GRID_W, GRID_W)
    n_win = NA_KEY_ROWS * GRID_W
    kwin = k_ref[pl.ds(ks, n_win), :]
    vwin = v_ref[pl.ds(ks, n_win), :]
    acc = _na_heads(q_ref[...], [kwin, kc_ref[...]], lambda h: [bias_ref[0, h], None],
                    [vwin, vc_ref[...]])
    o_ref[...] = acc.astype(BF16)


def _na_call(qkv, bias_tab, n_batch, seq, total_rows, ctx_len):
    gw = GROUP_WIDTH
    qt = Q_TILE
    n_q = seq // qt
    grid_rows = seq // GRID_W
    ctx_blk0 = n_batch * seq // ctx_len

    def bias_case(j):
        return jnp.where(j == 0, 0, jnp.where(j == n_q - 1, 2, 1))

    return pl.pallas_call(
        functools.partial(_na_kernel, grid_rows=grid_rows),
        grid=(n_batch, n_q),
        in_specs=[pl.BlockSpec((qt, gw), lambda b, j: (b * n_q + j, 0)),
                  pl.BlockSpec((seq, gw), lambda b, j: (b, 1)),
                  pl.BlockSpec((seq, gw), lambda b, j: (b, 2)),
                  pl.BlockSpec((ctx_len, gw), lambda b, j: (ctx_blk0 + b, 1)),
                  pl.BlockSpec((ctx_len, gw), lambda b, j: (ctx_blk0 + b, 2)),
                  pl.BlockSpec((1,) + bias_tab.shape[1:], lambda b, j: (bias_case(j), 0, 0, 0))],
        out_specs=pl.BlockSpec((qt, gw), lambda b, j: (b * n_q + j, 0)),
        out_shape=jax.ShapeDtypeStruct((total_rows, gw), BF16),
        compiler_params=_cparams(2),
        name="na_attn",
    )(qkv, qkv, qkv, qkv, qkv, bias_tab)


def _df_kernel(q_ref, k_ref, v_ref, kc_ref, vc_ref, lam_ref, subln_ref, o_ref, *, lam_init):
    lam = _lambda(lam_ref, lam_init)
    acc = _diff_heads(q_ref[...], [kc_ref[...], k_ref[...]], [vc_ref[...], v_ref[...]], lam,
                      subln_ref[...], lam_init)
    o_ref[...] = acc.astype(BF16)


def _df_call(qkv, lam_vecs, subln, lam_init, n_batch, seq, total_rows, ctx_len):
    gw = GROUP_WIDTH
    qt = Q_TILE
    n_q = seq // qt
    ctx_blk0 = n_batch * seq // ctx_len
    return pl.pallas_call(
        functools.partial(_df_kernel, lam_init=lam_init),
        grid=(n_batch, n_q),
        in_specs=[pl.BlockSpec((qt, gw), lambda b, j: (b * n_q + j, 3)),
                  pl.BlockSpec((seq, gw), lambda b, j: (b, 4)),
                  pl.BlockSpec((seq, gw), lambda b, j: (b, 5)),
                  pl.BlockSpec((ctx_len, gw), lambda b, j: (ctx_blk0 + b, 4)),
                  pl.BlockSpec((ctx_len, gw), lambda b, j: (ctx_blk0 + b, 5)),
                  _full(lam_vecs.shape),
                  _full(subln.shape)],
        out_specs=pl.BlockSpec((qt, gw), lambda b, j: (b * n_q + j, 0)),
        out_shape=jax.ShapeDtypeStruct((total_rows, gw), BF16),
        compiler_params=_cparams(2),
        name="diff_attn",
    )(qkv, qkv, qkv, qkv, qkv, lam_vecs, subln)


def _ctx_attn_kernel(qkv_ref, lam_ref, subln_ref, ya_in, yd_in, ya_ref, yd_ref, *, lam_init):
    del ya_in, yd_in
    gw = GROUP_WIDTH
    col = lambda g: qkv_ref[:, g * gw:(g + 1) * gw]
    ya = _na_heads(col(0), [col(1)], lambda h: [None], [col(2)])
    ya_ref[...] = ya.astype(BF16)
    lam = _lambda(lam_ref, lam_init)
    yd = _diff_heads(col(3), [col(4)], [col(5)], lam, subln_ref[...], lam_init)
    yd_ref[...] = yd.astype(BF16)


def _ctx_attn_call(qkv, lam_vecs, subln, lam_init, ya, yd, n_batch, seq, ctx_len):
    gw = GROUP_WIDTH
    ctx_blk0 = n_batch * seq // ctx_len
    any_spec = pl.BlockSpec(memory_space=pl.ANY)
    out_spec = pl.BlockSpec((ctx_len, gw), lambda b: (ctx_blk0 + b, 0))
    return pl.pallas_call(
        functools.partial(_ctx_attn_kernel, lam_init=lam_init),
        grid=(n_batch,),
        in_specs=[pl.BlockSpec((ctx_len, 6 * gw), lambda b: (ctx_blk0 + b, 0)),
                  _full(lam_vecs.shape), _full(subln.shape), any_spec, any_spec],
        out_specs=[out_spec, out_spec],
        out_shape=[jax.ShapeDtypeStruct(ya.shape, BF16), jax.ShapeDtypeStruct(yd.shape, BF16)],
        input_output_aliases={3: 0, 4: 1},
        compiler_params=_cparams(1),
        name="ctx_attn",
    )(qkv, lam_vecs, subln, ya, yd)


def _pool_kernel(p_ref, w_ref, scale_ref, *rest, seq, aliased):
    o_ref, pad_ref = rest[-2], rest[-1]
    del aliased
    gw = GROUP_WIDTH
    halo = POOL_HALO
    pad_ref[0:halo, :] = jnp.zeros((halo, gw), F32)
    pad_ref[halo + seq:, :] = jnp.zeros((halo, gw), F32)
    pad_ref[halo:halo + seq, :] = p_ref[...]
    chunk = min(seq, 256)
    lane_group = lax.broadcasted_iota(jnp.int32, (1, gw), 1) // (gw // len(POOL_WINDOWS))
    for c0 in range(0, seq, chunk):
        def at(off):
            return pad_ref[halo + c0 + off:halo + c0 + off + chunk, :]
        pos = c0 + lax.broadcasted_iota(jnp.int32, (chunk, 1), 0)
        x = at(0)
        run = x
        mean = None
        lo_done, hi_done = 0, 0
        for g, win in enumerate(POOL_WINDOWS):
            half = win // 2
            for off in range(-half, -lo_done):
                run = run + at(off)
            for off in range(hi_done + 1, half):
                run = run + at(off)
            lo_done, hi_done = half, half - 1
            cnt = (jnp.minimum(pos + half, seq) - jnp.maximum(pos - half, 0)).astype(F32)
            m = run / cnt
            mean = m if mean is None else jnp.where(lane_group == g, m, mean)
        y = _dot((mean - x).astype(BF16), w_ref[...]) * scale_ref[...]
        o_ref[c0:c0 + chunk, :] = y.astype(BF16)


def _pool_call(pool_in, w_bd, scale, n_seq, seq, row_blk0, total_rows, prev=None):
    gw = GROUP_WIDTH
    in_specs = [pl.BlockSpec((seq, gw), lambda b: (row_blk0 + b, 0)), _full((gw, gw)), _full((1, gw))]
    args = [pool_in, w_bd, scale]
    aliases = {}
    if prev is not None:
        in_specs.append(pl.BlockSpec(memory_space=pl.ANY))
        args.append(prev)
        aliases = {3: 0}
    return pl.pallas_call(
        functools.partial(_pool_kernel, seq=seq, aliased=prev is not None),
        grid=(n_seq,),
        in_specs=in_specs,
        out_specs=pl.BlockSpec((seq, gw), lambda b: (row_blk0 + b, 0)),
        out_shape=jax.ShapeDtypeStruct((total_rows, gw), BF16),
        scratch_shapes=[pltpu.VMEM((seq + 2 * POOL_HALO, gw), F32)],
        input_output_aliases=aliases,
        compiler_params=_cparams(1),
        name="pool_mix",
    )(*args)


def _fft_kernel(t_ref, cl_ref, sl_ref, cc_ref, sc_ref, w_ref, *rest, norm):
    o_ref, a_ref, b_ref = rest[-3], rest[-2], rest[-1]

    @pl.when(pl.program_id(1) == 0)
    def _():
        t = t_ref[...]
        a_ref[...] = _dot(t, cc_ref[...]).astype(BF16)
        b_ref[...] = _dot(t, sc_ref[...]).astype(BF16)

    f = (_dot(cl_ref[...], a_ref[...]) - _dot(sl_ref[...], b_ref[...])) * norm
    o_ref[...] = _dot(f.astype(BF16), w_ref[...]).astype(BF16)


def _fft_call(fft_in, cl, sl, cc_bd, sc_bd, w_bd, n_seq, seq, row_blk0, total_rows, prev=None):
    gw = GROUP_WIDTH
    tk = min(FFT_TILE, seq)
    n_k = seq // tk
    in_specs = [pl.BlockSpec((seq, gw), lambda b, k: (row_blk0 + b, 0)),
                pl.BlockSpec((tk, seq), lambda b, k: (k, 0)),
                pl.BlockSpec((tk, seq), lambda b, k: (k, 0)),
                _full((gw, gw)), _full((gw, gw)), _full((gw, gw))]
    args = [fft_in, cl, sl, cc_bd, sc_bd, w_bd]
    aliases = {}
    if prev is not None:
        in_specs.append(pl.BlockSpec(memory_space=pl.ANY))
        args.append(prev)
        aliases = {6: 0}
    norm = 1.0 / math.sqrt(seq * (gw // 4))
    return pl.pallas_call(
        functools.partial(_fft_kernel, norm=norm),
        grid=(n_seq, n_k),
        in_specs=in_specs,
        out_specs=pl.BlockSpec((tk, gw), lambda b, k: ((row_blk0 + b) * n_k + k, 0)),
        out_shape=jax.ShapeDtypeStruct((total_rows, gw), BF16),
        scratch_shapes=[pltpu.VMEM((seq, gw), BF16), pltpu.VMEM((seq, gw), BF16)],
        input_output_aliases=aliases,
        compiler_params=_cparams(2),
        name="fourier_mix",
    )(*args)


def _out_kernel(ya_ref, yd_ref, yb_ref, yf_ref, w_ref, x_ref, g1_ref, sh2_ref, sc2_ref, gn2_ref,
                rw_ref, rb_ref, x1_ref, h2_ref, lg_ref):
    gw = GROUP_WIDTH
    acc = _dot(ya_ref[...], w_ref[0:gw, :])
    acc = acc + _dot(yd_ref[...], w_ref[gw:2 * gw, :])
    acc = acc + _dot(yb_ref[...], w_ref[2 * gw:3 * gw, :])
    acc = acc + _dot(yf_ref[...], w_ref[3 * gw:4 * gw, :])
    x1 = x_ref[...] + g1_ref[0] * acc
    x1_ref[...] = x1
    ms = jnp.mean(x1 * x1, axis=-1, keepdims=True)
    h2 = x1 * lax.rsqrt(ms + NORM_EPS) * gn2_ref[...] * (1.0 + sc2_ref[0]) + sh2_ref[0]
    lg_ref[...] = jnp.dot(h2, rw_ref[...], precision=HIGHEST, preferred_element_type=F32) + rb_ref[...]
    tm = x1.shape[0]
    for s in range(SLABS):
        h2_ref[pl.ds(s, tm, stride=SLABS), :] = h2[:, s * LANES:(s + 1) * LANES]


def _out_call(ya, yd, yb, yf, w_out_bf, xc, mod3, gn2, rw_pad, rb_pad, rows, n_lat_rows, seq):
    d = D_MODEL
    gw = GROUP_WIDTH
    tm = ROW_TILE
    tiles_per_seq = seq // tm
    n_batch = n_lat_rows // seq

    def mod_row(i):
        return jnp.minimum(i // tiles_per_seq, n_batch)

    mix_spec = pl.BlockSpec((tm, gw), lambda i: (i, 0))
    mod_spec = lambda chunk: pl.BlockSpec((1, 1, d), lambda i: (mod_row(i), 0, chunk))
    return pl.pallas_call(
        _out_kernel,
        grid=(rows // tm,),
        in_specs=[mix_spec, mix_spec, mix_spec, mix_spec, _full((4 * gw, d)),
                  pl.BlockSpec((tm, d), lambda i: (i, 0)),
                  mod_spec(2), mod_spec(3), mod_spec(4), _full((1, d)),
                  _full(rw_pad.shape), _full(rb_pad.shape)],
        out_specs=[pl.BlockSpec((tm, d), lambda i: (i, 0)),
                   pl.BlockSpec((tm * SLABS, LANES), lambda i: (i, 0)),
                   pl.BlockSpec((tm, LANES), lambda i: (i, 0))],
        out_shape=[jax.ShapeDtypeStruct((rows, d), F32),
                   jax.ShapeDtypeStruct((rows * SLABS, LANES), F32),
                   jax.ShapeDtypeStruct((rows, LANES), F32)],
        compiler_params=_cparams(1),
        name="out_proj",
    )(ya, yd, yb, yf, w_out_bf, xc, mod3, mod3, mod3, gn2, rw_pad, rb_pad)


def _route_kernel(lg_ref, idx_ref, gate_ref, rank_ref, cnt_ref, carry_ref):
    i = pl.program_id(0)

    @pl.when(i == 0)
    def _():
        carry_ref[...] = jnp.zeros_like(carry_ref)

    lg = lg_ref[...]
    tm = lg.shape[0]
    lane = lax.broadcasted_iota(jnp.int32, (tm, LANES), 1)
    vals, idxs = [], []
    onehot = jnp.zeros((tm, LANES), F32)
    for _ in range(TOP_K):
        m = lg.max(axis=-1, keepdims=True)
        idx = jnp.min(jnp.where(lg == m, lane, LANES), axis=-1, keepdims=True)
        sel = lane == idx
        onehot = onehot + sel.astype(F32)
        lg = jnp.where(sel, -jnp.inf, lg)
        vals.append(m)
        idxs.append(idx)
    exps = [jnp.exp(v - vals[0]) for v in vals]
    denom = exps[0] + exps[1] + exps[2] + exps[3]
    r_i = lax.broadcasted_iota(jnp.int32, (tm, tm), 0)
    c_i = lax.broadcasted_iota(jnp.int32, (tm, tm), 1)
    tri = (c_i < r_i).astype(BF16)
    before = _dot(tri, onehot.astype(BF16)) + carry_ref[...]
    col4 = lax.broadcasted_iota(jnp.int32, (tm, TOP_K), 1)
    idx_o = jnp.zeros((tm, TOP_K), jnp.int32)
    gate_o = jnp.zeros((tm, TOP_K), F32)
    rank_o = jnp.zeros((tm, TOP_K), F32)
    for k in range(TOP_K):
        rk = jnp.sum(jnp.where(lane == idxs[k], before, 0.0), axis=-1, keepdims=True)
        idx_o = jnp.where(col4 == k, idxs[k], idx_o)
        gate_o = jnp.where(col4 == k, exps[k] / denom, gate_o)
        rank_o = jnp.where(col4 == k, rk, rank_o)
    idx_ref[...] = idx_o
    gate_ref[...] = gate_o
    rank_ref[...] = rank_o.astype(jnp.int32)
    carry_ref[...] = carry_ref[...] + jnp.sum(onehot, axis=0, keepdims=True)
    cnt_ref[...] = carry_ref[...].astype(jnp.int32)


def _route_call(logits):
    n = logits.shape[0]
    tm = ROW_TILE
    k_spec = pl.BlockSpec((tm, TOP_K), lambda i: (i, 0))
    return pl.pallas_call(
        _route_kernel,
        grid=(n // tm,),
        in_specs=[pl.BlockSpec((tm, LANES), lambda i: (i, 0))],
        out_specs=[k_spec, k_spec, k_spec, _full((1, LANES))],
        out_shape=[jax.ShapeDtypeStruct((n, TOP_K), jnp.int32),
                   jax.ShapeDtypeStruct((n, TOP_K), F32),
                   jax.ShapeDtypeStruct((n, TOP_K), jnp.int32),
                   jax.ShapeDtypeStruct((1, LANES), jnp.int32)],
        scratch_shapes=[pltpu.VMEM((1, LANES), F32)],
        compiler_params=_cparams(1),
        name="route",
    )(logits)


def _row_copy(src_ref, src_row, dst_ref, dst_row, sem):
    return pltpu.make_async_copy(
        src_ref.at[pl.ds(pl.multiple_of(src_row * SLABS, SLABS), SLABS), :],
        dst_ref.at[pl.ds(pl.multiple_of(dst_row * SLABS, SLABS), SLABS), :], sem)


def _pad_copy(zero_ref, hs_ref, start_row, n_rows, sem):
    return pltpu.make_async_copy(
        zero_ref.at[pl.ds(0, n_rows * SLABS), :],
        hs_ref.at[pl.ds(pl.multiple_of(start_row * SLABS, SLABS), n_rows * SLABS), :], sem)


def _dispatch_kernel(fill_ref, pad_ref, dest_hbm, h2_ref, hs_ref, dest_smem, zero_ref, sem, isem):
    i = pl.program_id(0)
    n_assign = TOKEN_TILE * TOP_K
    idx_copy = pltpu.make_async_copy(dest_hbm.at[pl.ds(i * n_assign, n_assign)], dest_smem, isem)
    idx_copy.start()

    @pl.when(i == 0)
    def _():
        zero_ref[...] = jnp.zeros_like(zero_ref)
        bits = [1 << b for b in reversed(range(int(math.log2(EXPERT_BLOCK))))]
        for phase in ("start", "wait"):
            def fill(e, carry, phase=phase):
                pos = fill_ref[e]
                pad = pad_ref[e]
                for bit in bits:
                    @pl.when((pad & bit) != 0)
                    def _(pos=pos, bit=bit):
                        cp = _pad_copy(zero_ref, hs_ref, pos, bit, sem)
                        cp.start() if phase == "start" else cp.wait()
                    pos = pos + (pad & bit)
                return carry

            lax.fori_loop(0, N_EXPERTS, fill, 0)

    idx_copy.wait()

    def start(t, carry):
        for k in range(TOP_K):
            _row_copy(h2_ref, t, hs_ref, dest_smem[t * TOP_K + k], sem).start()
        return carry

    lax.fori_loop(0, TOKEN_TILE, start, 0)

    def wait(t, carry):
        for k in range(TOP_K):
            _row_copy(h2_ref, 0, hs_ref, 0, sem).wait()
        return carry

    lax.fori_loop(0, TOKEN_TILE, wait, 0)


def _dispatch_call(fill_start, pad_len, dest_flat, h2_slabs, n_slots):
    n = h2_slabs.shape[0] // SLABS
    tt = TOKEN_TILE
    grid_spec = pltpu.PrefetchScalarGridSpec(
        num_scalar_prefetch=2,
        grid=(n // tt,),
        in_specs=[pl.BlockSpec(memory_space=pl.ANY),
                  pl.BlockSpec((tt * SLABS, LANES), lambda i, *_: (i, 0))],
        out_specs=pl.BlockSpec(memory_space=pl.ANY),
        scratch_shapes=[pltpu.SMEM((tt * TOP_K,), jnp.int32),
                        pltpu.VMEM((EXPERT_BLOCK // 2 * SLABS, LANES), F32),
                        pltpu.SemaphoreType.DMA, pltpu.SemaphoreType.DMA],
    )
    return pl.pallas_call(
        _dispatch_kernel,
        grid_spec=grid_spec,
        out_shape=jax.ShapeDtypeStruct((n_slots * SLABS, LANES), F32),
        compiler_params=_cparams(1),
        name="dispatch",
    )(fill_start, pad_len, dest_flat, h2_slabs)


def _expert_kernel(be_ref, nv_ref, hs_ref, w1_ref, b1_ref, w2_ref, b2_ref, ys_ref, h_ref):
    del be_ref
    i = pl.program_id(0)

    @pl.when(i < nv_ref[0])
    def _():
        blk = EXPERT_BLOCK
        for s in range(SLABS):
            h_ref[:, s * LANES:(s + 1) * LANES] = hs_ref[pl.ds(s, blk, stride=SLABS), :].astype(BF16)
        u = _dot(h_ref[...], w1_ref[0]) + b1_ref[0]
        glu = jnp.minimum(u[:, :D_FF], SWIGLU_LIMIT)
        lin = jnp.clip(u[:, D_FF:], -SWIGLU_LIMIT, SWIGLU_LIMIT)
        act = glu * jax.nn.sigmoid(SWIGLU_ALPHA * glu) * (lin + 1.0)
        y = _dot(act.astype(BF16), w2_ref[0]) + b2_ref[0]
        for s in range(SLABS):
            ys_ref[pl.ds(s, blk, stride=SLABS), :] = y[:, s * LANES:(s + 1) * LANES]


def _expert_call(block_e, n_valid, hs, w1p, b1p, w2b, b2, n_blocks):
    blk = EXPERT_BLOCK
    d = D_MODEL

    def row_blk(i, be, nv):
        return (jnp.minimum(i, nv[0] - 1), 0)

    grid_spec = pltpu.PrefetchScalarGridSpec(
        num_scalar_prefetch=2,
        grid=(n_blocks,),
        in_specs=[pl.BlockSpec((blk * SLABS, LANES), row_blk),
                  pl.BlockSpec((1, d, 2 * D_FF), lambda i, be, nv: (be[i], 0, 0)),
                  pl.BlockSpec((1, 1, 2 * D_FF), lambda i, be, nv: (be[i], 0, 0)),
                  pl.BlockSpec((1, D_FF, d), lambda i, be, nv: (be[i], 0, 0)),
                  pl.BlockSpec((1, 1, d), lambda i, be, nv: (be[i], 0, 0))],
        out_specs=pl.BlockSpec((blk * SLABS, LANES), row_blk),
        scratch_shapes=[pltpu.VMEM((blk, d), BF16)],
    )
    return pl.pallas_call(
        _expert_kernel,
        grid_spec=grid_spec,
        out_shape=jax.ShapeDtypeStruct(hs.shape, F32),
        compiler_params=_cparams(1),
        name="expert_ffn",
    )(block_e, n_valid, hs, w1p, b1p, w2b, b2)


def _combine_kernel(dest_hbm, ys_hbm, gate_ref, x_ref, g2_ref, o_ref, dest_smem, buf_ref, sem, isem):
    i = pl.program_id(0)
    tt = TOKEN_TILE
    n_assign = tt * TOP_K
    idx_copy = pltpu.make_async_copy(dest_hbm.at[pl.ds(i * n_assign, n_assign)], dest_smem, isem)
    idx_copy.start()
    idx_copy.wait()

    def start(t, carry):
        for k in range(TOP_K):
            _row_copy(ys_hbm, dest_smem[t * TOP_K + k], buf_ref, k * tt + t, sem).start()
        return carry

    lax.fori_loop(0, tt, start, 0)

    def wait(t, carry):
        for k in range(TOP_K):
            _row_copy(ys_hbm, 0, buf_ref, 0, sem).wait()
        return carry

    lax.fori_loop(0, tt, wait, 0)

    gate = gate_ref[...]
    g2 = g2_ref[0]
    for s in range(SLABS):
        cols = slice(s * LANES, (s + 1) * LANES)
        y = jnp.zeros((tt, LANES), F32)
        for k in range(TOP_K):
            y = y + gate[:, k:k + 1] * buf_ref[pl.ds(k * tt * SLABS + s, tt, stride=SLABS), :]
        o_ref[:, cols] = x_ref[:, cols] + g2[:, cols] * y


def _combine_call(dest_flat, ys, gate, x1, mod3, n_lat_rows, seq):
    n, d = x1.shape
    tt = TOKEN_TILE
    tiles_per_seq = seq // tt
    n_batch = n_lat_rows // seq

    def mod_row(i):
        return jnp.minimum(i // tiles_per_seq, n_batch)

    return pl.pallas_call(
        _combine_kernel,
        grid=(n // tt,),
        in_specs=[pl.BlockSpec(memory_space=pl.ANY), pl.BlockSpec(memory_space=pl.ANY),
                  pl.BlockSpec((tt, TOP_K), lambda i: (i, 0)),
                  pl.BlockSpec((tt, d), lambda i: (i, 0)),
                  pl.BlockSpec((1, 1, d), lambda i: (mod_row(i), 0, 5))],
        out_specs=pl.BlockSpec((tt, d), lambda i: (i, 0)),
        out_shape=jax.ShapeDtypeStruct((n, d), F32),
        scratch_shapes=[pltpu.SMEM((tt * TOP_K,), jnp.int32),
                        pltpu.VMEM((TOP_K * tt * SLABS, LANES), F32),
                        pltpu.SemaphoreType.DMA, pltpu.SemaphoreType.DMA],
        compiler_params=_cparams(1),
        name="combine",
    )(dest_flat, ys, gate, x1, mod3)


def _block_diag(blocks):
    g, a, b = blocks.shape
    eye = jnp.eye(g, dtype=blocks.dtype)
    return (eye[:, None, :, None] * blocks[:, :, None, :]).reshape(g * a, g * b)


def _rope_tables(seq, extra_rows):
    half = DIFF_QK_DIM // 2
    inv = ROPE_THETA ** (-jnp.arange(0, half, 2, dtype=F32) / half)
    pos = jnp.arange(seq)
    c = jnp.arange(GROUP_WIDTH)
    dd = c % DIFF_QK_DIM
    axis_pos = jnp.where((dd < half)[None, :], (pos // GRID_W)[:, None], (pos % GRID_W)[:, None])
    ang = axis_pos.astype(F32) * inv[dd % (half // 2)][None, :]
    sign = jnp.where((dd % half) < half // 2, -1.0, 1.0).astype(F32)
    cos_t = jnp.concatenate([jnp.cos(ang), jnp.ones((extra_rows, GROUP_WIDTH), F32)], axis=0)
    sin_t = jnp.concatenate([jnp.sin(ang) * sign[None, :], jnp.zeros((extra_rows, GROUP_WIDTH), F32)], axis=0)
    return cos_t, sin_t


def _dft_tables(n):
    k = jnp.arange(n, dtype=jnp.int32)
    ang = ((k[:, None] * k[None, :]) % n).astype(F32) * (2.0 * math.pi / n)
    return jnp.cos(ang), jnp.sin(ang)


def _na_bias_tables(rpb, grid_rows):
    w = GRID_W
    col = np.arange(w)
    col_start = np.clip(col - NA_COLS // 2, 0, w - NA_COLS)
    in_win = (col[None, :] >= col_start[:, None]) & (col[None, :] < col_start[:, None] + NA_COLS)
    rel_c = np.clip(col[None, :] - col[:, None], 1 - NA_COLS, NA_COLS - 1) + NA_COLS - 1
    n_tiles = grid_rows // NA_Q_ROWS
    cases = (0, 1, n_tiles - 1)
    rel_r = np.zeros((3, NA_Q_ROWS, NA_KEY_ROWS), np.int32)
    valid = np.zeros((3, NA_Q_ROWS, NA_KEY_ROWS), bool)
    for ci, tile in enumerate(cases):
        r0 = tile * NA_Q_ROWS
        k0 = int(np.clip(r0 - NA_ROWS // 2, 0, grid_rows - NA_KEY_ROWS))
        for j in range(NA_Q_ROWS):
            start = int(np.clip(r0 + j - NA_ROWS // 2, 0, grid_rows - NA_ROWS))
            for i in range(NA_KEY_ROWS):
                kr = k0 + i
                valid[ci, j, i] = start <= kr < start + NA_ROWS
                rel_r[ci, j, i] = np.clip(kr - (r0 + j) + NA_ROWS - 1, 0, 2 * NA_ROWS - 2)
    t = rpb.astype(F32)[:, :, rel_c]
    t = jnp.where(in_win[None, None], t, MASK_VALUE)
    b = t[:, rel_r]
    b = jnp.where(valid[None, :, :, :, None, None], b, MASK_VALUE)
    b = b.transpose(1, 0, 2, 4, 3, 5)
    return b.reshape(3, rpb.shape[0], NA_Q_ROWS * w, NA_KEY_ROWS * w)


def _moe(h2_slabs, logits, x1, mod3, w1p, b1p, w2b, b2, n_lat_rows, seq):
    n = x1.shape[0]
    blk = EXPERT_BLOCK
    idx, gate, rank, counts = _route_call(logits)
    counts = counts[0, :N_EXPERTS]
    padded = (counts + blk - 1) // blk * blk
    padded_end = jnp.cumsum(padded)
    padded_start = padded_end - padded
    dest = (padded_start[idx] + rank).reshape(n * TOP_K).astype(jnp.int32)
    n_blocks = n * TOP_K // blk + N_EXPERTS
    block_e = jnp.minimum(jnp.searchsorted(padded_end, jnp.arange(n_blocks) * blk, side='right'),
                          N_EXPERTS - 1).astype(jnp.int32)
    n_valid = (padded_end[-1:] // blk).astype(jnp.int32)
    fill_start = (padded_start + counts).astype(jnp.int32)
    pad_len = (padded - counts).astype(jnp.int32)
    hs = _dispatch_call(fill_start, pad_len, dest, h2_slabs, n_blocks * blk)
    ys = _expert_call(block_e, n_valid, hs, w1p, b1p, w2b, b2, n_blocks)
    return _combine_call(dest, ys, gate, x1, mod3, n_lat_rows, seq)


def kernel(x, c, ctx, c_ctx, w_ada, b_ada, g_norm1, g_norm2, w_in, w_out, na_q_gain, na_k_gain, na_rpb, diff_q_gain, diff_k_gain, diff_lambda_q1, diff_lambda_k1, diff_lambda_q2, diff_lambda_k2, diff_subln, pool_w, pool_scale, fft_w, router_w, router_b, moe_w1, moe_b1, moe_w2, moe_b2):
    n_batch, seq, d = x.shape
    ctx_len = ctx.shape[1]
    depth = w_ada.shape[0]
    gw = GROUP_WIDTH
    assert d == D_MODEL and seq % ROW_TILE == 0 and seq % ctx_len == 0 and ctx_len % TOKEN_TILE == 0
    assert (n_batch * ctx_len) % ROW_TILE == 0 and (seq // GRID_W) >= NA_KEY_ROWS
    n_lat = n_batch * seq
    n_ctx = n_batch * ctx_len

    xc = jnp.concatenate([x.reshape(n_lat, d), ctx.reshape(n_ctx, d)], axis=0)
    mod_rows = -(-(n_batch + 1) // SUBLANES) * SUBLANES
    cvec = jnp.zeros((mod_rows, d), F32).at[:n_batch].set(c).at[n_batch].set(c_ctx)
    mod = _ada_call(cvec, w_ada, b_ada)

    cos_t, sin_t = _rope_tables(seq, ROW_TILE)
    ones = lambda w: _block_diag(jnp.ones((gw // w, w, w), BF16))
    g64, g32 = ones(HEAD_DIM), ones(DIFF_QK_DIM)
    cl_lat, sl_lat = (t.astype(BF16) for t in _dft_tables(seq))
    cl_ctx, sl_ctx = (t.astype(BF16) for t in _dft_tables(ctx_len))
    cc, sc = _dft_tables(gw // 4)
    n_grp = fft_w.shape[1]
    cc_bd = _block_diag(jnp.broadcast_to(cc, (n_grp,) + cc.shape)).astype(BF16)
    sc_bd = _block_diag(jnp.broadcast_to(sc, (n_grp,) + sc.shape)).astype(BF16)

    for l in range(depth):
        ctx_out = l < depth - 1
        lam_init = 0.8 - 0.6 * math.exp(-0.3 * l)
        mod3 = mod[l].reshape(mod_rows, 1, 6 * d)
        tile = lambda v: jnp.tile(v.astype(F32), gw // v.shape[0])
        gains = jnp.stack([tile(na_q_gain[l]) * HEAD_DIM ** -0.5, tile(na_k_gain[l]),
                           tile(diff_q_gain[l]) * DIFF_QK_DIM ** -0.5, tile(diff_k_gain[l])]
                          + [jnp.zeros((gw,), F32)] * 4)
        lam_vecs = jnp.stack([diff_lambda_q1[l], diff_lambda_k1[l], diff_lambda_q2[l], diff_lambda_k2[l]]
                             + [jnp.zeros_like(diff_lambda_q1[l])] * 4).astype(F32)
        subln = tile(diff_subln[l]).reshape(1, gw)
        bias_tab = _na_bias_tables(na_rpb[l], seq // GRID_W)
        pool_bd = _block_diag(pool_w[l]).astype(BF16)
        fftw_bd = _block_diag(fft_w[l]).astype(BF16)
        rw_pad = jnp.zeros((d, LANES), F32).at[:, :N_EXPERTS].set(router_w[l])
        rb_pad = jnp.full((1, LANES), -jnp.inf, F32).at[0, :N_EXPERTS].set(router_b[l])
        n_e = moe_w1.shape[1]
        w1p = moe_w1[l].reshape(n_e, d, D_FF, 2).transpose(0, 1, 3, 2).reshape(n_e, d, 2 * D_FF).astype(BF16)
        b1p = moe_b1[l].reshape(n_e, D_FF, 2).transpose(0, 2, 1).reshape(n_e, 1, 2 * D_FF)
        w2b = moe_w2[l].astype(BF16)
        b2 = moe_b2[l].reshape(n_e, 1, d)

        qkv, pool_in, fft_in = _in_call(xc, mod3, g_norm1[l].reshape(1, d), w_in[l].astype(BF16), gains,
                                        cos_t, sin_t, g64, g32, n_lat, seq)
        rows = n_lat + n_ctx if ctx_out else n_lat
        ya = _na_call(qkv, bias_tab, n_batch, seq, rows, ctx_len)
        yd = _df_call(qkv, lam_vecs, subln, lam_init, n_batch, seq, rows, ctx_len)
        yb = _pool_call(pool_in, pool_bd, pool_scale[l].reshape(1, gw), n_batch, seq, 0, rows)
        yf = _fft_call(fft_in, cl_lat, sl_lat, cc_bd, sc_bd, fftw_bd, n_batch, seq, 0, rows)
        if ctx_out:
            ya, yd = _ctx_attn_call(qkv, lam_vecs, subln, lam_init, ya, yd, n_batch, seq, ctx_len)
            yb = _pool_call(pool_in, pool_bd, pool_scale[l].reshape(1, gw), n_batch, ctx_len,
                            n_lat // ctx_len, rows, prev=yb)
            yf = _fft_call(fft_in, cl_ctx, sl_ctx, cc_bd, sc_bd, fftw_bd, n_batch, ctx_len,
                           n_lat // ctx_len, rows, prev=yf)
        x1, h2_slabs, logits = _out_call(ya, yd, yb, yf, w_out[l].astype(BF16), xc, mod3,
                                         g_norm2[l].reshape(1, d), rw_pad, rb_pad, rows, n_lat, seq)
        x2 = _moe(h2_slabs, logits, x1, mod3, w1p, b1p, w2b, b2, n_lat, seq)
        xc = x2
    return xc[:n_lat].reshape(n_batch, seq, d)
```

```python
import functools
import math

import numpy as np
import jax
import jax.numpy as jnp
from jax import lax
from jax.experimental import pallas as pl
from jax.experimental.pallas import tpu as pltpu

F32 = jnp.float32
BF16 = jnp.bfloat16
HIGHEST = lax.Precision.HIGHEST

D_MODEL = 1024
DEPTH = 2
GRID_W = 64
HEAD_DIM = 64
GROUP_WIDTH = 256
N_HEADS = GROUP_WIDTH // HEAD_DIM
DIFF_QK_DIM = HEAD_DIM // 2
NA_ROWS = 8
NA_COLS = 16
POOL_WINDOWS = (2, 4, 8, 16)
POOL_HALO = max(POOL_WINDOWS) // 2
N_EXPERTS = 32
TOP_K = 4
D_FF = D_MODEL
SWIGLU_ALPHA = 1.702
SWIGLU_LIMIT = 7.0
ROPE_THETA = 10000.0
NORM_EPS = 1e-6
MASK_VALUE = -1e30

LANES = 128
SUBLANES = 8
ROW_TILE = 512
TOKEN_TILE = 256
EXPERT_BLOCK = 256
Q_TILE = 256
DIFF_Q_TILE = 512
LOG2_E = math.log2(math.e)
NA_Q_ROWS = Q_TILE // GRID_W
NA_KEY_ROWS = NA_ROWS + NA_Q_ROWS - 1
FFT_TILE = 512
SLABS = D_MODEL // LANES
VMEM_LIMIT = 48 * 1024 * 1024


def _cparams(n_axes):
    return pltpu.CompilerParams(dimension_semantics=("arbitrary",) * n_axes,
                                vmem_limit_bytes=VMEM_LIMIT)


def _dot(a, b):
    return jnp.dot(a, b, preferred_element_type=F32)


def _dot_nt(a, b):
    return lax.dot_general(a, b, (((1,), (1,)), ((), ())), preferred_element_type=F32)


def _full(shape):
    zeros = (0,) * len(shape)
    return pl.BlockSpec(shape, lambda *_: zeros)


def _ada_kernel(c_ref, w_ref, b_ref, o_ref):
    c = c_ref[...]
    act = c * jax.nn.sigmoid(c)
    o_ref[0] = jnp.dot(act, w_ref[0], precision=HIGHEST, preferred_element_type=F32) + b_ref[0]


def _ada_call(cvec, w_ada, b_ada):
    depth, d, n = w_ada.shape
    r = cvec.shape[0]
    tn = 1024
    return pl.pallas_call(
        _ada_kernel,
        grid=(depth, n // tn),
        in_specs=[_full((r, d)),
                  pl.BlockSpec((1, d, tn), lambda l, j: (l, 0, j)),
                  pl.BlockSpec((1, 1, tn), lambda l, j: (l, 0, j))],
        out_specs=pl.BlockSpec((1, r, tn), lambda l, j: (l, 0, j)),
        out_shape=jax.ShapeDtypeStruct((depth, r, n), F32),
        compiler_params=_cparams(2),
        name="ada_mod",
    )(cvec, w_ada, b_ada.reshape(depth, 1, n))


def _seg_mean_sq(p, gmat_ref, width):
    sq = p * p
    hi = sq.astype(BF16)
    lo = (sq - hi.astype(F32)).astype(BF16)
    g = gmat_ref[...]
    return (_dot(hi, g) + _dot(lo, g)) * (1.0 / width)


def _stream_specs(tm, d, n_lat_tiles, ctx_blk0):
    return [pl.BlockSpec((tm, d), lambda i: (jnp.minimum(i, n_lat_tiles - 1), 0)),
            pl.BlockSpec((tm, d), lambda i: (ctx_blk0 + jnp.maximum(i - n_lat_tiles, 0), 0))]


def _in_kernel(xa_ref, xb_ref, sh_ref, sc_ref, g_ref, w_ref, gains_ref, cos_ref, sin_ref, g64_ref,
               g32_ref, qkv_ref, pool_ref, fft_ref, *, n_lat_tiles):
    x = jnp.where(pl.program_id(0) < n_lat_tiles, xa_ref[...], xb_ref[...])
    ms = jnp.mean(x * x, axis=-1, keepdims=True)
    y = x * lax.rsqrt(ms + NORM_EPS) * g_ref[...]
    h = (y * (1.0 + sc_ref[0]) + sh_ref[0]).astype(BF16)
    gw = GROUP_WIDTH

    def proj(g):
        return _dot(h, w_ref[:, g * gw:(g + 1) * gw])

    def put(g, val):
        qkv_ref[:, g * gw:(g + 1) * gw] = val.astype(BF16)

    def normed(p, gmat_ref, width, row):
        return p * lax.rsqrt(_seg_mean_sq(p, gmat_ref, width) + NORM_EPS) * gains_ref[row:row + 1, :]

    lane = lax.broadcasted_iota(jnp.int32, (1, gw), 1)
    first_half = (lane % 16) < 8

    def rope(p):
        rot = jnp.where(first_half, pltpu.roll(p, gw - 8, 1), pltpu.roll(p, 8, 1))
        return p * cos_ref[...] + rot * sin_ref[...]

    put(0, normed(proj(0), g64_ref, HEAD_DIM, 0))
    put(1, normed(proj(1), g64_ref, HEAD_DIM, 1))
    put(2, proj(2))
    put(3, rope(normed(proj(3), g32_ref, DIFF_QK_DIM, 2)))
    put(4, rope(normed(proj(4), g32_ref, DIFF_QK_DIM, 3)))
    put(5, proj(5))
    pool_ref[...] = proj(6)
    fft_ref[...] = proj(7).astype(BF16)


def _in_call(xa, xb, ctx_blk0, rows, mod3, g1, w_in_bf, gains, cos_t, sin_t, g64, g32, n_lat_rows, seq):
    d = xa.shape[1]
    tm = ROW_TILE
    n_lat_tiles = n_lat_rows // tm
    tiles_per_seq = seq // tm
    n_batch = n_lat_rows // seq
    gw = GROUP_WIDTH

    def mod_row(i):
        return jnp.minimum(i // tiles_per_seq, n_batch)

    def tab_row(i):
        return jnp.where(i < n_lat_tiles, i % tiles_per_seq, tiles_per_seq)

    return pl.pallas_call(
        functools.partial(_in_kernel, n_lat_tiles=n_lat_tiles),
        grid=(rows // tm,),
        in_specs=_stream_specs(tm, d, n_lat_tiles, ctx_blk0) + [
                  pl.BlockSpec((1, 1, d), lambda i: (mod_row(i), 0, 0)),
                  pl.BlockSpec((1, 1, d), lambda i: (mod_row(i), 0, 1)),
                  _full((1, d)),
                  _full(w_in_bf.shape),
                  _full(gains.shape),
                  pl.BlockSpec((tm, gw), lambda i: (tab_row(i), 0)),
                  pl.BlockSpec((tm, gw), lambda i: (tab_row(i), 0)),
                  _full((gw, gw)),
                  _full((gw, gw))],
        out_specs=[pl.BlockSpec((tm, 6 * gw), lambda i: (i, 0)),
                   pl.BlockSpec((tm, gw), lambda i: (i, 0)),
                   pl.BlockSpec((tm, gw), lambda i: (i, 0))],
        out_shape=[jax.ShapeDtypeStruct((rows, 6 * gw), BF16),
                   jax.ShapeDtypeStruct((rows, gw), F32),
                   jax.ShapeDtypeStruct((rows, gw), BF16)],
        compiler_params=_cparams(1),
        name="in_proj",
    )(xa, xb, mod3, mod3, g1, w_in_bf, gains, cos_t, sin_t, g64, g32)


def _lane_mask(width, start, size):
    lane = lax.broadcasted_iota(jnp.int32, (1, width), 1)
    return (lane >= start) & (lane < start + size)


def _softmax_pv(qm, keys, biases, vals, sum_lane=None):
    scores = []
    for k, bias in zip(keys, biases):
        s = _dot_nt(qm, k)
        scores.append(s if bias is None else s + bias)
    m = scores[0].max(axis=-1, keepdims=True)
    for s in scores[1:]:
        m = jnp.maximum(m, s.max(axis=-1, keepdims=True))
    o = None
    l = None
    for s, v in zip(scores, vals):
        e = jnp.exp2(s - m)
        part = _dot(e.astype(BF16), v)
        o = part if o is None else o + part
        if sum_lane is None:
            ls = e.sum(axis=-1, keepdims=True)
            l = ls if l is None else l + ls
    if sum_lane is not None:
        lane = lax.broadcasted_iota(jnp.int32, (1, o.shape[1]), 1)
        l = jnp.sum(jnp.where(lane == sum_lane, o, 0.0), axis=-1, keepdims=True)
    return o, l


def _na_heads(q, keys, bias_fn, vals):
    gw = GROUP_WIDTH
    acc = jnp.zeros((q.shape[0], gw), F32)
    for h in range(N_HEADS):
        mask = _lane_mask(gw, h * HEAD_DIM, HEAD_DIM)
        qm = jnp.where(mask, q, jnp.zeros_like(q))
        o, l = _softmax_pv(qm, keys, bias_fn(h), vals)
        acc = acc + jnp.where(mask, o / l, 0.0)
    return acc


def _lambda(lam_ref, lam_init):
    lv = lam_ref[...]
    d1 = jnp.sum(lv[0:1, :] * lv[1:2, :], axis=-1, keepdims=True)
    d2 = jnp.sum(lv[2:3, :] * lv[3:4, :], axis=-1, keepdims=True)
    return jnp.exp(d1) - jnp.exp(d2) + lam_init


def _diff_heads(q, keys, vals, lam, subln, lam_init, acc_ref):
    gw = GROUP_WIDTH
    lane = lax.broadcasted_iota(jnp.int32, (1, gw), 1)
    nones = [None] * len(keys)
    acc_ref[...] = jnp.zeros_like(acc_ref)

    for h in range(N_HEADS):
        lo = h * HEAD_DIM
        hm = (lane >= lo) & (lane < lo + HEAD_DIM)
        sum_lane = (lo + HEAD_DIM) % gw
        vals_h = [jnp.where(hm, v, jnp.ones_like(v)) for v in vals]
        outs = []
        for part in range(2):
            plo = lo + part * DIFF_QK_DIM
            qm = jnp.where((lane >= plo) & (lane < plo + DIFF_QK_DIM), q, jnp.zeros_like(q))
            o, l = _softmax_pv(qm, keys, nones, vals_h, sum_lane)
            outs.append(o / l)
        a = jnp.where(hm, outs[0] - lam * outs[1], 0.0)
        ms = jnp.sum(a * a, axis=-1, keepdims=True) * (1.0 / HEAD_DIM)
        acc_ref[...] += a * lax.rsqrt(ms + NORM_EPS)
    return acc_ref[...] * subln * (1.0 - lam_init)


def _na_kernel(q_ref, k_ref, v_ref, kc_ref, vc_ref, bias_ref, o_ref, *, grid_rows):
    j = pl.program_id(1)
    key_row0 = jnp.clip(j * NA_Q_ROWS - NA_ROWS // 2, 0, grid_rows - NA_KEY_ROWS)
    ks = pl.multiple_of(key_row0 * GRID_W, GRID_W)
    n_win = NA_KEY_ROWS * GRID_W
    kwin = k_ref[pl.ds(ks, n_win), :]
    vwin = v_ref[pl.ds(ks, n_win), :]
    acc = _na_heads(q_ref[...], [kwin, kc_ref[...]], lambda h: [bias_ref[0, h], None],
                    [vwin, vc_ref[...]])
    o_ref[...] = acc.astype(BF16)


def _na_call(qkv, bias_tab, n_batch, seq, total_rows, ctx_len):
    gw = GROUP_WIDTH
    qt = Q_TILE
    n_q = seq // qt
    grid_rows = seq // GRID_W
    ctx_blk0 = n_batch * seq // ctx_len

    def bias_case(j):
        return jnp.where(j == 0, 0, jnp.where(j == n_q - 1, 2, 1))

    return pl.pallas_call(
        functools.partial(_na_kernel, grid_rows=grid_rows),
        grid=(n_batch, n_q),
        in_specs=[pl.BlockSpec((qt, gw), lambda b, j: (b * n_q + j, 0)),
                  pl.BlockSpec((seq, gw), lambda b, j: (b, 1)),
                  pl.BlockSpec((seq, gw), lambda b, j: (b, 2)),
                  pl.BlockSpec((ctx_len, gw), lambda b, j: (ctx_blk0 + b, 1)),
                  pl.BlockSpec((ctx_len, gw), lambda b, j: (ctx_blk0 + b, 2)),
                  pl.BlockSpec((1,) + bias_tab.shape[1:], lambda b, j: (bias_case(j), 0, 0, 0))],
        out_specs=pl.BlockSpec((qt, gw), lambda b, j: (b * n_q + j, 0)),
        out_shape=jax.ShapeDtypeStruct((total_rows, gw), BF16),
        compiler_params=_cparams(2),
        name="na_attn",
    )(qkv, qkv, qkv, qkv, qkv, bias_tab)


def _df_kernel(q_ref, k_ref, v_ref, kc_ref, vc_ref, lam_ref, subln_ref, o_ref, acc_ref, *, lam_init):
    lam = _lambda(lam_ref, lam_init)
    acc = _diff_heads(q_ref[...], [kc_ref[...], k_ref[...]], [vc_ref[...], v_ref[...]], lam,
                      subln_ref[...], lam_init, acc_ref)
    o_ref[...] = acc.astype(BF16)


def _df_call(qkv, lam_vecs, subln, lam_init, n_batch, seq, total_rows, ctx_len):
    gw = GROUP_WIDTH
    qt = DIFF_Q_TILE
    n_q = seq // qt
    ctx_blk0 = n_batch * seq // ctx_len
    return pl.pallas_call(
        functools.partial(_df_kernel, lam_init=lam_init),
        grid=(n_batch, n_q),
        in_specs=[pl.BlockSpec((qt, gw), lambda b, j: (b * n_q + j, 3)),
                  pl.BlockSpec((seq, gw), lambda b, j: (b, 4)),
                  pl.BlockSpec((seq, gw), lambda b, j: (b, 5)),
                  pl.BlockSpec((ctx_len, gw), lambda b, j: (ctx_blk0 + b, 4)),
                  pl.BlockSpec((ctx_len, gw), lambda b, j: (ctx_blk0 + b, 5)),
                  _full(lam_vecs.shape),
                  _full(subln.shape)],
        out_specs=pl.BlockSpec((qt, gw), lambda b, j: (b * n_q + j, 0)),
        out_shape=jax.ShapeDtypeStruct((total_rows, gw), BF16),
        scratch_shapes=[pltpu.VMEM((qt, gw), F32)],
        compiler_params=_cparams(2),
        name="diff_attn",
    )(qkv, qkv, qkv, qkv, qkv, lam_vecs, subln)


def _ctx_attn_kernel(qkv_ref, lam_ref, subln_ref, ya_in, yd_in, ya_ref, yd_ref, acc_ref, *, lam_init):
    del ya_in, yd_in
    gw = GROUP_WIDTH
    col = lambda g: qkv_ref[:, g * gw:(g + 1) * gw]
    ya = _na_heads(col(0), [col(1)], lambda h: [None], [col(2)])
    ya_ref[...] = ya.astype(BF16)
    lam = _lambda(lam_ref, lam_init)
    yd = _diff_heads(col(3), [col(4)], [col(5)], lam, subln_ref[...], lam_init, acc_ref)
    yd_ref[...] = yd.astype(BF16)


def _ctx_attn_call(qkv, lam_vecs, subln, lam_init, ya, yd, n_batch, seq, ctx_len):
    gw = GROUP_WIDTH
    ctx_blk0 = n_batch * seq // ctx_len
    any_spec = pl.BlockSpec(memory_space=pl.ANY)
    out_spec = pl.BlockSpec((ctx_len, gw), lambda b: (ctx_blk0 + b, 0))
    return pl.pallas_call(
        functools.partial(_ctx_attn_kernel, lam_init=lam_init),
        grid=(n_batch,),
        in_specs=[pl.BlockSpec((ctx_len, 6 * gw), lambda b: (ctx_blk0 + b, 0)),
                  _full(lam_vecs.shape), _full(subln.shape), any_spec, any_spec],
        out_specs=[out_spec, out_spec],
        out_shape=[jax.ShapeDtypeStruct(ya.shape, BF16), jax.ShapeDtypeStruct(yd.shape, BF16)],
        input_output_aliases={3: 0, 4: 1},
        scratch_shapes=[pltpu.VMEM((ctx_len, gw), F32)],
        compiler_params=_cparams(1),
        name="ctx_attn",
    )(qkv, lam_vecs, subln, ya, yd)


def _pool_kernel(p_ref, w_ref, scale_ref, *rest, seq, aliased):
    o_ref, pad_ref = rest[-2], rest[-1]
    del aliased
    gw = GROUP_WIDTH
    halo = POOL_HALO
    pad_ref[0:halo, :] = jnp.zeros((halo, gw), F32)
    pad_ref[halo + seq:, :] = jnp.zeros((halo, gw), F32)
    pad_ref[halo:halo + seq, :] = p_ref[...]
    chunk = min(seq, 256)
    lane_group = lax.broadcasted_iota(jnp.int32, (1, gw), 1) // (gw // len(POOL_WINDOWS))
    for c0 in range(0, seq, chunk):
        def at(off):
            return pad_ref[halo + c0 + off:halo + c0 + off + chunk, :]
        pos = c0 + lax.broadcasted_iota(jnp.int32, (chunk, 1), 0)
        x = at(0)
        run = x
        mean = None
        lo_done, hi_done = 0, 0
        for g, win in enumerate(POOL_WINDOWS):
            half = win // 2
            for off in range(-half, -lo_done):
                run = run + at(off)
            for off in range(hi_done + 1, half):
                run = run + at(off)
            lo_done, hi_done = half, half - 1
            cnt = (jnp.minimum(pos + half, seq) - jnp.maximum(pos - half, 0)).astype(F32)
            m = run / cnt
            mean = m if mean is None else jnp.where(lane_group == g, m, mean)
        y = _dot((mean - x).astype(BF16), w_ref[...]) * scale_ref[...]
        o_ref[c0:c0 + chunk, :] = y.astype(BF16)


def _pool_call(pool_in, w_bd, scale, n_seq, seq, row_blk0, total_rows, prev=None):
    gw = GROUP_WIDTH
    in_specs = [pl.BlockSpec((seq, gw), lambda b: (row_blk0 + b, 0)), _full((gw, gw)), _full((1, gw))]
    args = [pool_in, w_bd, scale]
    aliases = {}
    if prev is not None:
        in_specs.append(pl.BlockSpec(memory_space=pl.ANY))
        args.append(prev)
        aliases = {3: 0}
    return pl.pallas_call(
        functools.partial(_pool_kernel, seq=seq, aliased=prev is not None),
        grid=(n_seq,),
        in_specs=in_specs,
        out_specs=pl.BlockSpec((seq, gw), lambda b: (row_blk0 + b, 0)),
        out_shape=jax.ShapeDtypeStruct((total_rows, gw), BF16),
        scratch_shapes=[pltpu.VMEM((seq + 2 * POOL_HALO, gw), F32)],
        input_output_aliases=aliases,
        compiler_params=_cparams(1),
        name="pool_mix",
    )(*args)


def _fft_kernel(t_ref, cl_ref, sl_ref, cc_ref, sc_ref, w_ref, *rest, norm):
    o_ref, a_ref, b_ref = rest[-3], rest[-2], rest[-1]

    @pl.when(pl.program_id(1) == 0)
    def _():
        t = t_ref[...]
        a_ref[...] = _dot(t, cc_ref[...]).astype(BF16)
        b_ref[...] = _dot(t, sc_ref[...]).astype(BF16)

    f = (_dot(cl_ref[...], a_ref[...]) - _dot(sl_ref[...], b_ref[...])) * norm
    o_ref[...] = _dot(f.astype(BF16), w_ref[...]).astype(BF16)


def _fft_call(fft_in, cl, sl, cc_bd, sc_bd, w_bd, n_seq, seq, row_blk0, total_rows, prev=None):
    gw = GROUP_WIDTH
    tk = min(FFT_TILE, seq)
    n_k = seq // tk
    in_specs = [pl.BlockSpec((seq, gw), lambda b, k: (row_blk0 + b, 0)),
                pl.BlockSpec((tk, seq), lambda b, k: (k, 0)),
                pl.BlockSpec((tk, seq), lambda b, k: (k, 0)),
                _full((gw, gw)), _full((gw, gw)), _full((gw, gw))]
    args = [fft_in, cl, sl, cc_bd, sc_bd, w_bd]
    aliases = {}
    if prev is not None:
        in_specs.append(pl.BlockSpec(memory_space=pl.ANY))
        args.append(prev)
        aliases = {6: 0}
    norm = 1.0 / math.sqrt(seq * (gw // 4))
    return pl.pallas_call(
        functools.partial(_fft_kernel, norm=norm),
        grid=(n_seq, n_k),
        in_specs=in_specs,
        out_specs=pl.BlockSpec((tk, gw), lambda b, k: ((row_blk0 + b) * n_k + k, 0)),
        out_shape=jax.ShapeDtypeStruct((total_rows, gw), BF16),
        scratch_shapes=[pltpu.VMEM((seq, gw), BF16), pltpu.VMEM((seq, gw), BF16)],
        input_output_aliases=aliases,
        compiler_params=_cparams(2),
        name="fourier_mix",
    )(*args)


def _out_kernel(ya_ref, yd_ref, yb_ref, yf_ref, w_ref, xa_ref, xb_ref, g1_ref, sh2_ref, sc2_ref, gn2_ref,
                rw_ref, rb_ref, x1_ref, h2_ref, lg_ref, *, n_lat_tiles):
    gw = GROUP_WIDTH
    x = jnp.where(pl.program_id(0) < n_lat_tiles, xa_ref[...], xb_ref[...])
    acc = _dot(ya_ref[...], w_ref[0:gw, :])
    acc = acc + _dot(yd_ref[...], w_ref[gw:2 * gw, :])
    acc = acc + _dot(yb_ref[...], w_ref[2 * gw:3 * gw, :])
    acc = acc + _dot(yf_ref[...], w_ref[3 * gw:4 * gw, :])
    x1 = x + g1_ref[0] * acc
    x1_ref[...] = x1
    ms = jnp.mean(x1 * x1, axis=-1, keepdims=True)
    h2 = x1 * lax.rsqrt(ms + NORM_EPS) * gn2_ref[...] * (1.0 + sc2_ref[0]) + sh2_ref[0]
    lg_ref[...] = jnp.dot(h2, rw_ref[...], precision=HIGHEST, preferred_element_type=F32) + rb_ref[...]
    tm = x1.shape[0]
    for s in range(SLABS):
        h2_ref[pl.ds(s, tm, stride=SLABS), :] = h2[:, s * LANES:(s + 1) * LANES]


def _out_call(ya, yd, yb, yf, w_out_bf, xa, xb, ctx_blk0, mod3, gn2, rw_pad, rb_pad, rows, n_lat_rows, seq):
    d = D_MODEL
    gw = GROUP_WIDTH
    tm = ROW_TILE
    n_lat_tiles = n_lat_rows // tm
    tiles_per_seq = seq // tm
    n_batch = n_lat_rows // seq

    def mod_row(i):
        return jnp.minimum(i // tiles_per_seq, n_batch)

    mix_spec = pl.BlockSpec((tm, gw), lambda i: (i, 0))
    mod_spec = lambda chunk: pl.BlockSpec((1, 1, d), lambda i: (mod_row(i), 0, chunk))
    return pl.pallas_call(
        functools.partial(_out_kernel, n_lat_tiles=n_lat_tiles),
        grid=(rows // tm,),
        in_specs=[mix_spec, mix_spec, mix_spec, mix_spec, _full((4 * gw, d))]
                 + _stream_specs(tm, d, n_lat_tiles, ctx_blk0) + [
                  mod_spec(2), mod_spec(3), mod_spec(4), _full((1, d)),
                  _full(rw_pad.shape), _full(rb_pad.shape)],
        out_specs=[pl.BlockSpec((tm, d), lambda i: (i, 0)),
                   pl.BlockSpec((tm * SLABS, LANES), lambda i: (i, 0)),
                   pl.BlockSpec((tm, LANES), lambda i: (i, 0))],
        out_shape=[jax.ShapeDtypeStruct((rows, d), F32),
                   jax.ShapeDtypeStruct((rows * SLABS, LANES), F32),
                   jax.ShapeDtypeStruct((rows, LANES), F32)],
        compiler_params=_cparams(1),
        name="out_proj",
    )(ya, yd, yb, yf, w_out_bf, xa, xb, mod3, mod3, mod3, gn2, rw_pad, rb_pad)


def _route_kernel(lg_ref, idx_ref, gate_ref, rank_ref, cnt_ref, carry_ref):
    i = pl.program_id(0)

    @pl.when(i == 0)
    def _():
        carry_ref[...] = jnp.zeros_like(carry_ref)

    lg = lg_ref[...]
    tm = lg.shape[0]
    lane = lax.broadcasted_iota(jnp.int32, (tm, LANES), 1)
    vals, idxs = [], []
    onehot = jnp.zeros((tm, LANES), F32)
    for _ in range(TOP_K):
        m = lg.max(axis=-1, keepdims=True)
        idx = jnp.min(jnp.where(lg == m, lane, LANES), axis=-1, keepdims=True)
        sel = lane == idx
        onehot = onehot + sel.astype(F32)
        lg = jnp.where(sel, -jnp.inf, lg)
        vals.append(m)
        idxs.append(idx)
    exps = [jnp.exp(v - vals[0]) for v in vals]
    denom = exps[0] + exps[1] + exps[2] + exps[3]
    r_i = lax.broadcasted_iota(jnp.int32, (tm, tm), 0)
    c_i = lax.broadcasted_iota(jnp.int32, (tm, tm), 1)
    tri = (c_i < r_i).astype(BF16)
    before = _dot(tri, onehot.astype(BF16)) + carry_ref[...]
    col4 = lax.broadcasted_iota(jnp.int32, (tm, TOP_K), 1)
    idx_o = jnp.zeros((tm, TOP_K), jnp.int32)
    gate_o = jnp.zeros((tm, TOP_K), F32)
    rank_o = jnp.zeros((tm, TOP_K), F32)
    for k in range(TOP_K):
        rk = jnp.sum(jnp.where(lane == idxs[k], before, 0.0), axis=-1, keepdims=True)
        idx_o = jnp.where(col4 == k, idxs[k], idx_o)
        gate_o = jnp.where(col4 == k, exps[k] / denom, gate_o)
        rank_o = jnp.where(col4 == k, rk, rank_o)
    idx_ref[...] = idx_o
    gate_ref[...] = gate_o
    rank_ref[...] = rank_o.astype(jnp.int32)
    carry_ref[...] = carry_ref[...] + jnp.sum(onehot, axis=0, keepdims=True)
    cnt_ref[...] = carry_ref[...].astype(jnp.int32)


def _route_call(logits):
    n = logits.shape[0]
    tm = ROW_TILE
    k_spec = pl.BlockSpec((tm, TOP_K), lambda i: (i, 0))
    return pl.pallas_call(
        _route_kernel,
        grid=(n // tm,),
        in_specs=[pl.BlockSpec((tm, LANES), lambda i: (i, 0))],
        out_specs=[k_spec, k_spec, k_spec, _full((1, LANES))],
        out_shape=[jax.ShapeDtypeStruct((n, TOP_K), jnp.int32),
                   jax.ShapeDtypeStruct((n, TOP_K), F32),
                   jax.ShapeDtypeStruct((n, TOP_K), jnp.int32),
                   jax.ShapeDtypeStruct((1, LANES), jnp.int32)],
        scratch_shapes=[pltpu.VMEM((1, LANES), F32)],
        compiler_params=_cparams(1),
        name="route",
    )(logits)


def _row_copy(src_ref, src_row, dst_ref, dst_row, sem):
    return pltpu.make_async_copy(
        src_ref.at[pl.ds(pl.multiple_of(src_row * SLABS, SLABS), SLABS), :],
        dst_ref.at[pl.ds(pl.multiple_of(dst_row * SLABS, SLABS), SLABS), :], sem)


def _pad_copy(zero_ref, hs_ref, start_row, n_rows, sem):
    return pltpu.make_async_copy(
        zero_ref.at[pl.ds(0, n_rows * SLABS), :],
        hs_ref.at[pl.ds(pl.multiple_of(start_row * SLABS, SLABS), n_rows * SLABS), :], sem)


def _dispatch_kernel(fill_ref, pad_ref, dest_hbm, h2_ref, hs_ref, dest_smem, zero_ref, sem, isem):
    i = pl.program_id(0)
    n_assign = TOKEN_TILE * TOP_K
    idx_copy = pltpu.make_async_copy(dest_hbm.at[pl.ds(i * n_assign, n_assign)], dest_smem, isem)
    idx_copy.start()

    @pl.when(i == 0)
    def _():
        zero_ref[...] = jnp.zeros_like(zero_ref)
        bits = [1 << b for b in reversed(range(int(math.log2(EXPERT_BLOCK))))]
        for phase in ("start", "wait"):
            def fill(e, carry, phase=phase):
                pos = fill_ref[e]
                pad = pad_ref[e]
                for bit in bits:
                    @pl.when((pad & bit) != 0)
                    def _(pos=pos, bit=bit):
                        cp = _pad_copy(zero_ref, hs_ref, pos, bit, sem)
                        cp.start() if phase == "start" else cp.wait()
                    pos = pos + (pad & bit)
                return carry

            lax.fori_loop(0, N_EXPERTS, fill, 0)

    idx_copy.wait()

    def start(t, carry):
        for k in range(TOP_K):
            _row_copy(h2_ref, t, hs_ref, dest_smem[t * TOP_K + k], sem).start()
        return carry

    lax.fori_loop(0, TOKEN_TILE, start, 0)

    def wait(t, carry):
        for k in range(TOP_K):
            _row_copy(h2_ref, 0, hs_ref, 0, sem).wait()
        return carry

    lax.fori_loop(0, TOKEN_TILE, wait, 0)


def _dispatch_call(fill_start, pad_len, dest_flat, h2_slabs, n_slots):
    n = h2_slabs.shape[0] // SLABS
    tt = TOKEN_TILE
    grid_spec = pltpu.PrefetchScalarGridSpec(
        num_scalar_prefetch=2,
        grid=(n // tt,),
        in_specs=[pl.BlockSpec(memory_space=pl.ANY),
                  pl.BlockSpec((tt * SLABS, LANES), lambda i, *_: (i, 0))],
        out_specs=pl.BlockSpec(memory_space=pl.ANY),
        scratch_shapes=[pltpu.SMEM((tt * TOP_K,), jnp.int32),
                        pltpu.VMEM((EXPERT_BLOCK // 2 * SLABS, LANES), F32),
                        pltpu.SemaphoreType.DMA, pltpu.SemaphoreType.DMA],
    )
    return pl.pallas_call(
        _dispatch_kernel,
        grid_spec=grid_spec,
        out_shape=jax.ShapeDtypeStruct((n_slots * SLABS, LANES), F32),
        compiler_params=_cparams(1),
        name="dispatch",
    )(fill_start, pad_len, dest_flat, h2_slabs)


GLU_BLOCK = 2 * LANES


def _expert_kernel(be_ref, first_ref, nv_ref, hs_ref, w1_ref, b1_ref, w2_ref, b2_ref, perm_ref,
                   ys_ref, h_ref, act_ref, w1p_ref, w2p_ref):
    del be_ref
    i = pl.program_id(0)
    n_glu = 2 * D_FF // GLU_BLOCK

    @pl.when((first_ref[i] != 0) & (i < nv_ref[0]))
    def _():
        for b in range(n_glu):
            cols = slice(b * GLU_BLOCK, (b + 1) * GLU_BLOCK)
            w1p_ref[:, cols] = _dot(w1_ref[0, :, cols].astype(BF16), perm_ref[...]).astype(BF16)
        w2p_ref[...] = w2_ref[0].astype(BF16)

    @pl.when(i < nv_ref[0])
    def _():
        blk = EXPERT_BLOCK
        for s in range(SLABS):
            h_ref[:, s * LANES:(s + 1) * LANES] = hs_ref[pl.ds(s, blk, stride=SLABS), :].astype(BF16)
        h = h_ref[...]
        for b in range(n_glu):
            cols = slice(b * GLU_BLOCK, (b + 1) * GLU_BLOCK)
            u = _dot(h, w1p_ref[:, cols]) + b1_ref[0, :, cols]
            glu = jnp.minimum(u[:, :LANES], SWIGLU_LIMIT)
            lin = jnp.clip(u[:, LANES:], -SWIGLU_LIMIT, SWIGLU_LIMIT)
            act = glu * jax.nn.sigmoid(SWIGLU_ALPHA * glu) * (lin + 1.0)
            act_ref[:, b * LANES:(b + 1) * LANES] = act.astype(BF16)
        y = _dot(act_ref[...], w2p_ref[...]) + b2_ref[0]
        for s in range(SLABS):
            ys_ref[pl.ds(s, blk, stride=SLABS), :] = y[:, s * LANES:(s + 1) * LANES]


def _expert_call(block_e, first, n_valid, hs, w1, b1p, w2, b2, perm, layer, n_blocks):
    blk = EXPERT_BLOCK
    d = D_MODEL

    def row_blk(i, be, first, nv):
        return (jnp.minimum(i, nv[0] - 1), 0)

    def expert_blk(i, be, first, nv):
        return (be[i], 0, 0)

    def weight_blk(i, be, first, nv):
        return (layer, be[i], 0, 0)

    grid_spec = pltpu.PrefetchScalarGridSpec(
        num_scalar_prefetch=3,
        grid=(n_blocks,),
        in_specs=[pl.BlockSpec((blk * SLABS, LANES), row_blk),
                  pl.BlockSpec((None, 1, d, 2 * D_FF), weight_blk),
                  pl.BlockSpec((1, 1, 2 * D_FF), expert_blk),
                  pl.BlockSpec((None, 1, D_FF, d), weight_blk),
                  pl.BlockSpec((1, 1, d), expert_blk),
                  pl.BlockSpec((GLU_BLOCK, GLU_BLOCK), lambda i, *_: (0, 0))],
        out_specs=pl.BlockSpec((blk * SLABS, LANES), row_blk),
        scratch_shapes=[pltpu.VMEM((blk, d), BF16), pltpu.VMEM((blk, D_FF), BF16),
                        pltpu.VMEM((d, 2 * D_FF), BF16), pltpu.VMEM((D_FF, d), BF16)],
    )
    return pl.pallas_call(
        _expert_kernel,
        grid_spec=grid_spec,
        out_shape=jax.ShapeDtypeStruct(hs.shape, F32),
        compiler_params=_cparams(1),
        name="expert_ffn",
    )(block_e, first, n_valid, hs, w1, b1p, w2, b2, perm)


def _combine_kernel(dest_hbm, ys_hbm, gate_ref, x_ref, g2_ref, o_ref, dest_smem, buf_ref, sem, isem):
    i = pl.program_id(0)
    tt = TOKEN_TILE
    n_assign = tt * TOP_K
    idx_copy = pltpu.make_async_copy(dest_hbm.at[pl.ds(i * n_assign, n_assign)], dest_smem, isem)
    idx_copy.start()
    idx_copy.wait()

    def start(t, carry):
        for k in range(TOP_K):
            _row_copy(ys_hbm, dest_smem[t * TOP_K + k], buf_ref, k * tt + t, sem).start()
        return carry

    lax.fori_loop(0, tt, start, 0)

    def wait(t, carry):
        for k in range(TOP_K):
            _row_copy(ys_hbm, 0, buf_ref, 0, sem).wait()
        return carry

    lax.fori_loop(0, tt, wait, 0)

    gate = gate_ref[...]
    g2 = g2_ref[0]
    for s in range(SLABS):
        cols = slice(s * LANES, (s + 1) * LANES)
        y = jnp.zeros((tt, LANES), F32)
        for k in range(TOP_K):
            y = y + gate[:, k:k + 1] * buf_ref[pl.ds(k * tt * SLABS + s, tt, stride=SLABS), :]
        o_ref[:, cols] = x_ref[:, cols] + g2[:, cols] * y


def _combine_call(dest_flat, ys, gate, x1, mod3, n_lat_rows, seq):
    n, d = x1.shape
    tt = TOKEN_TILE
    tiles_per_seq = seq // tt
    n_batch = n_lat_rows // seq

    def mod_row(i):
        return jnp.minimum(i // tiles_per_seq, n_batch)

    return pl.pallas_call(
        _combine_kernel,
        grid=(n // tt,),
        in_specs=[pl.BlockSpec(memory_space=pl.ANY), pl.BlockSpec(memory_space=pl.ANY),
                  pl.BlockSpec((tt, TOP_K), lambda i: (i, 0)),
                  pl.BlockSpec((tt, d), lambda i: (i, 0)),
                  pl.BlockSpec((1, 1, d), lambda i: (mod_row(i), 0, 5))],
        out_specs=pl.BlockSpec((tt, d), lambda i: (i, 0)),
        out_shape=jax.ShapeDtypeStruct((n, d), F32),
        scratch_shapes=[pltpu.SMEM((tt * TOP_K,), jnp.int32),
                        pltpu.VMEM((TOP_K * tt * SLABS, LANES), F32),
                        pltpu.SemaphoreType.DMA, pltpu.SemaphoreType.DMA],
        compiler_params=_cparams(1),
        name="combine",
    )(dest_flat, ys, gate, x1, mod3)


def _block_diag(blocks):
    g, a, b = blocks.shape
    eye = jnp.eye(g, dtype=blocks.dtype)
    return (eye[:, None, :, None] * blocks[:, :, None, :]).reshape(g * a, g * b)


def _rope_tables(seq, extra_rows):
    half = DIFF_QK_DIM // 2
    inv = ROPE_THETA ** (-jnp.arange(0, half, 2, dtype=F32) / half)
    pos = jnp.arange(seq)
    c = jnp.arange(GROUP_WIDTH)
    dd = c % DIFF_QK_DIM
    axis_pos = jnp.where((dd < half)[None, :], (pos // GRID_W)[:, None], (pos % GRID_W)[:, None])
    ang = axis_pos.astype(F32) * inv[dd % (half // 2)][None, :]
    sign = jnp.where((dd % half) < half // 2, -1.0, 1.0).astype(F32)
    cos_t = jnp.concatenate([jnp.cos(ang), jnp.ones((extra_rows, GROUP_WIDTH), F32)], axis=0)
    sin_t = jnp.concatenate([jnp.sin(ang) * sign[None, :], jnp.zeros((extra_rows, GROUP_WIDTH), F32)], axis=0)
    return cos_t, sin_t


def _dft_tables(n):
    k = jnp.arange(n, dtype=jnp.int32)
    ang = ((k[:, None] * k[None, :]) % n).astype(F32) * (2.0 * math.pi / n)
    return jnp.cos(ang), jnp.sin(ang)


def _na_bias_tables(rpb, grid_rows):
    w = GRID_W
    col = np.arange(w)
    col_start = np.clip(col - NA_COLS // 2, 0, w - NA_COLS)
    in_win = (col[None, :] >= col_start[:, None]) & (col[None, :] < col_start[:, None] + NA_COLS)
    rel_c = np.clip(col[None, :] - col[:, None], 1 - NA_COLS, NA_COLS - 1) + NA_COLS - 1
    n_tiles = grid_rows // NA_Q_ROWS
    cases = (0, 1, n_tiles - 1)
    rel_r = np.zeros((3, NA_Q_ROWS, NA_KEY_ROWS), np.int32)
    valid = np.zeros((3, NA_Q_ROWS, NA_KEY_ROWS), bool)
    for ci, tile in enumerate(cases):
        r0 = tile * NA_Q_ROWS
        k0 = int(np.clip(r0 - NA_ROWS // 2, 0, grid_rows - NA_KEY_ROWS))
        for j in range(NA_Q_ROWS):
            start = int(np.clip(r0 + j - NA_ROWS // 2, 0, grid_rows - NA_ROWS))
            for i in range(NA_KEY_ROWS):
                kr = k0 + i
                valid[ci, j, i] = start <= kr < start + NA_ROWS
                rel_r[ci, j, i] = np.clip(kr - (r0 + j) + NA_ROWS - 1, 0, 2 * NA_ROWS - 2)
    t = rpb.astype(F32)[:, :, rel_c]
    t = jnp.where(in_win[None, None], t, MASK_VALUE)
    b = t[:, rel_r]
    b = jnp.where(valid[None, :, :, :, None, None], b, MASK_VALUE)
    b = b.transpose(1, 0, 2, 4, 3, 5)
    return b.reshape(3, rpb.shape[0], NA_Q_ROWS * w, NA_KEY_ROWS * w)


def _moe(h2_slabs, logits, x1, mod3, w1, b1p, w2, b2, perm, layer, n_lat_rows, seq):
    n = x1.shape[0]
    blk = EXPERT_BLOCK
    idx, gate, rank, counts = _route_call(logits)
    counts = counts[0, :N_EXPERTS]
    padded = (counts + blk - 1) // blk * blk
    padded_end = jnp.cumsum(padded)
    padded_start = padded_end - padded
    dest = (padded_start[idx] + rank).reshape(n * TOP_K).astype(jnp.int32)
    n_blocks = n * TOP_K // blk + N_EXPERTS
    block_row0 = jnp.arange(n_blocks, dtype=jnp.int32) * blk
    block_e = jnp.minimum(jnp.sum((padded_end[None, :] <= block_row0[:, None]).astype(jnp.int32), axis=1),
                          N_EXPERTS - 1).astype(jnp.int32)
    first = jnp.concatenate([jnp.ones((1,), jnp.int32),
                             (block_e[1:] != block_e[:-1]).astype(jnp.int32)])
    n_valid = (padded_end[-1:] // blk).astype(jnp.int32)
    fill_start = (padded_start + counts).astype(jnp.int32)
    pad_len = (padded - counts).astype(jnp.int32)
    hs = _dispatch_call(fill_start, pad_len, dest, h2_slabs, n_blocks * blk)
    ys = _expert_call(block_e, first, n_valid, hs, w1, b1p, w2, b2, perm, layer, n_blocks)
    return _combine_call(dest, ys, gate, x1, mod3, n_lat_rows, seq)


def kernel(x, c, ctx, c_ctx, w_ada, b_ada, g_norm1, g_norm2, w_in, w_out, na_q_gain, na_k_gain, na_rpb, diff_q_gain, diff_k_gain, diff_lambda_q1, diff_lambda_k1, diff_lambda_q2, diff_lambda_k2, diff_subln, pool_w, pool_scale, fft_w, router_w, router_b, moe_w1, moe_b1, moe_w2, moe_b2):
    n_batch, seq, d = x.shape
    ctx_len = ctx.shape[1]
    depth = w_ada.shape[0]
    gw = GROUP_WIDTH
    assert d == D_MODEL and seq % ROW_TILE == 0 and seq % ctx_len == 0 and ctx_len % TOKEN_TILE == 0
    assert (n_batch * ctx_len) % ROW_TILE == 0 and (seq // GRID_W) >= NA_KEY_ROWS
    n_lat = n_batch * seq
    n_ctx = n_batch * ctx_len

    xa, xb, ctx_blk0 = x.reshape(n_lat, d), ctx.reshape(n_ctx, d), 0
    mod_rows = -(-(n_batch + 1) // SUBLANES) * SUBLANES
    cvec = jnp.zeros((mod_rows, d), F32).at[:n_batch].set(c).at[n_batch].set(c_ctx)
    mod = _ada_call(cvec, w_ada, b_ada)

    cos_t, sin_t = _rope_tables(seq, ROW_TILE)
    ones = lambda w: _block_diag(jnp.ones((gw // w, w, w), BF16))
    g64, g32 = ones(HEAD_DIM), ones(DIFF_QK_DIM)
    cl_lat, sl_lat = (t.astype(BF16) for t in _dft_tables(seq))
    cl_ctx, sl_ctx = (t.astype(BF16) for t in _dft_tables(ctx_len))
    cc, sc = _dft_tables(gw // 4)
    n_grp = fft_w.shape[1]
    cc_bd = _block_diag(jnp.broadcast_to(cc, (n_grp,) + cc.shape)).astype(BF16)
    sc_bd = _block_diag(jnp.broadcast_to(sc, (n_grp,) + sc.shape)).astype(BF16)
    src = np.arange(GLU_BLOCK)
    dst = np.where(src % 2 == 0, src // 2, LANES + src // 2)
    perm_np = np.zeros((GLU_BLOCK, GLU_BLOCK), np.float32)
    perm_np[src, dst] = 1.0
    glu_perm = jnp.asarray(perm_np, BF16)

    for l in range(depth):
        ctx_out = l < depth - 1
        lam_init = 0.8 - 0.6 * math.exp(-0.3 * l)
        mod3 = mod[l].reshape(mod_rows, 1, 6 * d)
        tile = lambda v: jnp.tile(v.astype(F32), gw // v.shape[0])
        gains = jnp.stack([tile(na_q_gain[l]) * (HEAD_DIM ** -0.5 * LOG2_E), tile(na_k_gain[l]),
                           tile(diff_q_gain[l]) * (DIFF_QK_DIM ** -0.5 * LOG2_E), tile(diff_k_gain[l])]
                          + [jnp.zeros((gw,), F32)] * 4)
        lam_vecs = jnp.stack([diff_lambda_q1[l], diff_lambda_k1[l], diff_lambda_q2[l], diff_lambda_k2[l]]
                             + [jnp.zeros_like(diff_lambda_q1[l])] * 4).astype(F32)
        subln = tile(diff_subln[l]).reshape(1, gw)
        bias_tab = _na_bias_tables(na_rpb[l] * LOG2_E, seq // GRID_W)
        pool_bd = _block_diag(pool_w[l]).astype(BF16)
        fftw_bd = _block_diag(fft_w[l]).astype(BF16)
        rw_pad = jnp.zeros((d, LANES), F32).at[:, :N_EXPERTS].set(router_w[l])
        rb_pad = jnp.full((1, LANES), -jnp.inf, F32).at[0, :N_EXPERTS].set(router_b[l])
        n_e = moe_w1.shape[1]
        b1p = moe_b1[l].reshape(n_e, 2 * D_FF // GLU_BLOCK, LANES, 2).transpose(0, 1, 3, 2)
        b1p = b1p.reshape(n_e, 1, 2 * D_FF)
        b2 = moe_b2[l].reshape(n_e, 1, d)

        qkv, pool_in, fft_in = _in_call(xa, xb, ctx_blk0, n_lat + n_ctx, mod3, g_norm1[l].reshape(1, d),
                                        w_in[l].astype(BF16), gains, cos_t, sin_t, g64, g32, n_lat, seq)
        rows = n_lat + n_ctx if ctx_out else n_lat
        ya = _na_call(qkv, bias_tab, n_batch, seq, rows, ctx_len)
        yd = _df_call(qkv, lam_vecs, subln, lam_init, n_batch, seq, rows, ctx_len)
        yb = _pool_call(pool_in, pool_bd, pool_scale[l].reshape(1, gw), n_batch, seq, 0, rows)
        yf = _fft_call(fft_in, cl_lat, sl_lat, cc_bd, sc_bd, fftw_bd, n_batch, seq, 0, rows)
        if ctx_out:
            ya, yd = _ctx_attn_call(qkv, lam_vecs, subln, lam_init, ya, yd, n_batch, seq, ctx_len)
            yb = _pool_call(pool_in, pool_bd, pool_scale[l].reshape(1, gw), n_batch, ctx_len,
                            n_lat // ctx_len, rows, prev=yb)
            yf = _fft_call(fft_in, cl_ctx, sl_ctx, cc_bd, sc_bd, fftw_bd, n_batch, ctx_len,
                           n_lat // ctx_len, rows, prev=yf)
        x1, h2_slabs, logits = _out_call(ya, yd, yb, yf, w_out[l].astype(BF16), xa, xb, ctx_blk0, mod3,
                                         g_norm2[l].reshape(1, d), rw_pad, rb_pad, rows, n_lat, seq)
        x2 = _moe(h2_slabs, logits, x1, mod3, moe_w1, b1p, moe_w2, b2, glu_perm, l, n_lat, seq)
        xa, xb, ctx_blk0 = x2, x2, n_lat // ROW_TILE
    return x2[:n_lat].reshape(n_batch, seq, d)
```

```python
import functools
import math

import numpy as np
import jax
import jax.numpy as jnp
from jax import lax
from jax.experimental import pallas as pl
from jax.experimental.pallas import tpu as pltpu

F32 = jnp.float32
BF16 = jnp.bfloat16
HIGHEST = lax.Precision.HIGHEST

D_MODEL = 1024
DEPTH = 2
GRID_W = 64
HEAD_DIM = 64
GROUP_WIDTH = 256
N_HEADS = GROUP_WIDTH // HEAD_DIM
DIFF_QK_DIM = HEAD_DIM // 2
NA_ROWS = 8
NA_COLS = 16
POOL_WINDOWS = (2, 4, 8, 16)
POOL_HALO = max(POOL_WINDOWS) // 2
N_EXPERTS = 32
TOP_K = 4
D_FF = D_MODEL
SWIGLU_ALPHA = 1.702
SWIGLU_LIMIT = 7.0
ROPE_THETA = 10000.0
NORM_EPS = 1e-6
MASK_VALUE = -1e30

LANES = 128
SUBLANES = 8
ROW_TILE = 512
TOKEN_TILE = 256
EXPERT_BLOCK = 256
Q_TILE = 256
DIFF_Q_TILE = 512
LOG2_E = math.log2(math.e)
NA_Q_ROWS = Q_TILE // GRID_W
NA_KEY_ROWS = NA_ROWS + NA_Q_ROWS - 1
FFT_TILE = 512
SLABS = D_MODEL // LANES
ISSUE_UNROLL = 4
VMEM_LIMIT = 48 * 1024 * 1024


def _cparams(n_axes):
    return pltpu.CompilerParams(dimension_semantics=("arbitrary",) * n_axes,
                                vmem_limit_bytes=VMEM_LIMIT)


def _dot(a, b):
    return jnp.dot(a, b, preferred_element_type=F32)


def _dot_nt(a, b):
    return lax.dot_general(a, b, (((1,), (1,)), ((), ())), preferred_element_type=F32)


def _full(shape):
    zeros = (0,) * len(shape)
    return pl.BlockSpec(shape, lambda *_: zeros)


def _ada_kernel(c_ref, w_ref, b_ref, o_ref):
    c = c_ref[...]
    act = c * jax.nn.sigmoid(c)
    o_ref[0] = jnp.dot(act, w_ref[0], precision=HIGHEST, preferred_element_type=F32) + b_ref[0]


def _ada_call(cvec, w_ada, b_ada):
    depth, d, n = w_ada.shape
    r = cvec.shape[0]
    tn = 1024
    return pl.pallas_call(
        _ada_kernel,
        grid=(depth, n // tn),
        in_specs=[_full((r, d)),
                  pl.BlockSpec((1, d, tn), lambda l, j: (l, 0, j)),
                  pl.BlockSpec((1, 1, tn), lambda l, j: (l, 0, j))],
        out_specs=pl.BlockSpec((1, r, tn), lambda l, j: (l, 0, j)),
        out_shape=jax.ShapeDtypeStruct((depth, r, n), F32),
        compiler_params=_cparams(2),
        name="ada_mod",
    )(cvec, w_ada, b_ada.reshape(depth, 1, n))


def _seg_mean_sq(p, gmat_ref, width):
    sq = p * p
    hi = sq.astype(BF16)
    lo = (sq - hi.astype(F32)).astype(BF16)
    g = gmat_ref[...]
    return (_dot(hi, g) + _dot(lo, g)) * (1.0 / width)


def _stream_specs(tm, d, n_lat_tiles, ctx_blk0):
    return [pl.BlockSpec((tm, d), lambda i: (jnp.minimum(i, n_lat_tiles - 1), 0)),
            pl.BlockSpec((tm, d), lambda i: (ctx_blk0 + jnp.maximum(i - n_lat_tiles, 0), 0))]


def _in_kernel(xa_ref, xb_ref, sh_ref, sc_ref, g_ref, w_ref, gains_ref, cos_ref, sin_ref, g64_ref,
               g32_ref, qkv_ref, pool_ref, fft_ref, *, n_lat_tiles):
    x = jnp.where(pl.program_id(0) < n_lat_tiles, xa_ref[...], xb_ref[...])
    ms = jnp.mean(x * x, axis=-1, keepdims=True)
    y = x * lax.rsqrt(ms + NORM_EPS) * g_ref[...]
    h = (y * (1.0 + sc_ref[0]) + sh_ref[0]).astype(BF16)
    gw = GROUP_WIDTH

    def proj(g):
        return _dot(h, w_ref[:, g * gw:(g + 1) * gw])

    def put(g, val):
        qkv_ref[:, g * gw:(g + 1) * gw] = val.astype(BF16)

    def normed(p, gmat_ref, width, row):
        return p * lax.rsqrt(_seg_mean_sq(p, gmat_ref, width) + NORM_EPS) * gains_ref[row:row + 1, :]

    lane = lax.broadcasted_iota(jnp.int32, (1, gw), 1)
    first_half = (lane % 16) < 8

    def rope(p):
        rot = jnp.where(first_half, pltpu.roll(p, gw - 8, 1), pltpu.roll(p, 8, 1))
        return p * cos_ref[...] + rot * sin_ref[...]

    put(0, normed(proj(0), g64_ref, HEAD_DIM, 0))
    put(1, normed(proj(1), g64_ref, HEAD_DIM, 1))
    put(2, proj(2))
    put(3, rope(normed(proj(3), g32_ref, DIFF_QK_DIM, 2)))
    put(4, rope(normed(proj(4), g32_ref, DIFF_QK_DIM, 3)))
    put(5, proj(5))
    pool_ref[...] = proj(6)
    fft_ref[...] = proj(7).astype(BF16)


def _in_call(xa, xb, ctx_blk0, rows, mod3, g1, w_in_bf, gains, cos_t, sin_t, g64, g32, n_lat_rows, seq):
    d = xa.shape[1]
    tm = ROW_TILE
    n_lat_tiles = n_lat_rows // tm
    tiles_per_seq = seq // tm
    n_batch = n_lat_rows // seq
    gw = GROUP_WIDTH

    def mod_row(i):
        return jnp.minimum(i // tiles_per_seq, n_batch)

    def tab_row(i):
        return jnp.where(i < n_lat_tiles, i % tiles_per_seq, tiles_per_seq)

    return pl.pallas_call(
        functools.partial(_in_kernel, n_lat_tiles=n_lat_tiles),
        grid=(rows // tm,),
        in_specs=_stream_specs(tm, d, n_lat_tiles, ctx_blk0) + [
                  pl.BlockSpec((1, 1, d), lambda i: (mod_row(i), 0, 0)),
                  pl.BlockSpec((1, 1, d), lambda i: (mod_row(i), 0, 1)),
                  _full((1, d)),
                  _full(w_in_bf.shape),
                  _full(gains.shape),
                  pl.BlockSpec((tm, gw), lambda i: (tab_row(i), 0)),
                  pl.BlockSpec((tm, gw), lambda i: (tab_row(i), 0)),
                  _full((gw, gw)),
                  _full((gw, gw))],
        out_specs=[pl.BlockSpec((tm, 6 * gw), lambda i: (i, 0)),
                   pl.BlockSpec((tm, gw), lambda i: (i, 0)),
                   pl.BlockSpec((tm, gw), lambda i: (i, 0))],
        out_shape=[jax.ShapeDtypeStruct((rows, 6 * gw), BF16),
                   jax.ShapeDtypeStruct((rows, gw), F32),
                   jax.ShapeDtypeStruct((rows, gw), BF16)],
        compiler_params=_cparams(1),
        name="in_proj",
    )(xa, xb, mod3, mod3, g1, w_in_bf, gains, cos_t, sin_t, g64, g32)


def _lane_mask(width, start, size):
    lane = lax.broadcasted_iota(jnp.int32, (1, width), 1)
    return (lane >= start) & (lane < start + size)


def _softmax_pv(qm, keys, biases, vals, sum_lane=None):
    scores = []
    for k, bias in zip(keys, biases):
        s = _dot_nt(qm, k)
        scores.append(s if bias is None else s + bias)
    m = scores[0].max(axis=-1, keepdims=True)
    for s in scores[1:]:
        m = jnp.maximum(m, s.max(axis=-1, keepdims=True))
    o = None
    l = None
    for s, v in zip(scores, vals):
        e = jnp.exp2(s - m)
        part = _dot(e.astype(BF16), v)
        o = part if o is None else o + part
        if sum_lane is None:
            ls = e.sum(axis=-1, keepdims=True)
            l = ls if l is None else l + ls
    if sum_lane is not None:
        lane = lax.broadcasted_iota(jnp.int32, (1, o.shape[1]), 1)
        l = jnp.sum(jnp.where(lane == sum_lane, o, 0.0), axis=-1, keepdims=True)
    return o, l


def _na_heads(q, keys, bias_fn, vals):
    gw = GROUP_WIDTH
    acc = jnp.zeros((q.shape[0], gw), F32)
    for h in range(N_HEADS):
        mask = _lane_mask(gw, h * HEAD_DIM, HEAD_DIM)
        qm = jnp.where(mask, q, jnp.zeros_like(q))
        o, l = _softmax_pv(qm, keys, bias_fn(h), vals)
        acc = acc + jnp.where(mask, o / l, 0.0)
    return acc


def _lambda(lam_ref, lam_init):
    lv = lam_ref[...]
    d1 = jnp.sum(lv[0:1, :] * lv[1:2, :], axis=-1, keepdims=True)
    d2 = jnp.sum(lv[2:3, :] * lv[3:4, :], axis=-1, keepdims=True)
    return jnp.exp(d1) - jnp.exp(d2) + lam_init


def _diff_heads(q, keys, vals, lam, subln, lam_init, acc_ref):
    gw = GROUP_WIDTH
    lane = lax.broadcasted_iota(jnp.int32, (1, gw), 1)
    nones = [None] * len(keys)
    acc_ref[...] = jnp.zeros_like(acc_ref)

    for h in range(N_HEADS):
        lo = h * HEAD_DIM
        hm = (lane >= lo) & (lane < lo + HEAD_DIM)
        sum_lane = (lo + HEAD_DIM) % gw
        vals_h = [jnp.where(hm, v, jnp.ones_like(v)) for v in vals]
        outs = []
        for part in range(2):
            plo = lo + part * DIFF_QK_DIM
            qm = jnp.where((lane >= plo) & (lane < plo + DIFF_QK_DIM), q, jnp.zeros_like(q))
            o, l = _softmax_pv(qm, keys, nones, vals_h, sum_lane)
            outs.append(o / l)
        a = jnp.where(hm, outs[0] - lam * outs[1], 0.0)
        ms = jnp.sum(a * a, axis=-1, keepdims=True) * (1.0 / HEAD_DIM)
        acc_ref[...] += a * lax.rsqrt(ms + NORM_EPS)
    return acc_ref[...] * subln * (1.0 - lam_init)


def _na_kernel(q_ref, k_ref, v_ref, kc_ref, vc_ref, bias_ref, o_ref, *, grid_rows):
    j = pl.program_id(1)
    key_row0 = jnp.clip(j * NA_Q_ROWS - NA_ROWS // 2, 0, grid_rows - NA_KEY_ROWS)
    ks = pl.multiple_of(key_row0 * GRID_W, GRID_W)
    n_win = NA_KEY_ROWS * GRID_W
    kwin = k_ref[pl.ds(ks, n_win), :]
    vwin = v_ref[pl.ds(ks, n_win), :]
    acc = _na_heads(q_ref[...], [kwin, kc_ref[...]], lambda h: [bias_ref[0, h], None],
                    [vwin, vc_ref[...]])
    o_ref[...] = acc.astype(BF16)


def _na_call(qkv, bias_tab, n_batch, seq, total_rows, ctx_len):
    gw = GROUP_WIDTH
    qt = Q_TILE
    n_q = seq // qt
    grid_rows = seq // GRID_W
    ctx_blk0 = n_batch * seq // ctx_len

    def bias_case(j):
        return jnp.where(j == 0, 0, jnp.where(j == n_q - 1, 2, 1))

    return pl.pallas_call(
        functools.partial(_na_kernel, grid_rows=grid_rows),
        grid=(n_batch, n_q),
        in_specs=[pl.BlockSpec((qt, gw), lambda b, j: (b * n_q + j, 0)),
                  pl.BlockSpec((seq, gw), lambda b, j: (b, 1)),
                  pl.BlockSpec((seq, gw), lambda b, j: (b, 2)),
                  pl.BlockSpec((ctx_len, gw), lambda b, j: (ctx_blk0 + b, 1)),
                  pl.BlockSpec((ctx_len, gw), lambda b, j: (ctx_blk0 + b, 2)),
                  pl.BlockSpec((1,) + bias_tab.shape[1:], lambda b, j: (bias_case(j), 0, 0, 0))],
        out_specs=pl.BlockSpec((qt, gw), lambda b, j: (b * n_q + j, 0)),
        out_shape=jax.ShapeDtypeStruct((total_rows, gw), BF16),
        compiler_params=_cparams(2),
        name="na_attn",
    )(qkv, qkv, qkv, qkv, qkv, bias_tab)


def _df_kernel(q_ref, k_ref, v_ref, kc_ref, vc_ref, lam_ref, subln_ref, o_ref, acc_ref, *, lam_init):
    lam = _lambda(lam_ref, lam_init)
    acc = _diff_heads(q_ref[...], [kc_ref[...], k_ref[...]], [vc_ref[...], v_ref[...]], lam,
                      subln_ref[...], lam_init, acc_ref)
    o_ref[...] = acc.astype(BF16)


def _df_call(qkv, lam_vecs, subln, lam_init, n_batch, seq, total_rows, ctx_len):
    gw = GROUP_WIDTH
    qt = DIFF_Q_TILE
    n_q = seq // qt
    ctx_blk0 = n_batch * seq // ctx_len
    return pl.pallas_call(
        functools.partial(_df_kernel, lam_init=lam_init),
        grid=(n_batch, n_q),
        in_specs=[pl.BlockSpec((qt, gw), lambda b, j: (b * n_q + j, 3)),
                  pl.BlockSpec((seq, gw), lambda b, j: (b, 4)),
                  pl.BlockSpec((seq, gw), lambda b, j: (b, 5)),
                  pl.BlockSpec((ctx_len, gw), lambda b, j: (ctx_blk0 + b, 4)),
                  pl.BlockSpec((ctx_len, gw), lambda b, j: (ctx_blk0 + b, 5)),
                  _full(lam_vecs.shape),
                  _full(subln.shape)],
        out_specs=pl.BlockSpec((qt, gw), lambda b, j: (b * n_q + j, 0)),
        out_shape=jax.ShapeDtypeStruct((total_rows, gw), BF16),
        scratch_shapes=[pltpu.VMEM((qt, gw), F32)],
        compiler_params=_cparams(2),
        name="diff_attn",
    )(qkv, qkv, qkv, qkv, qkv, lam_vecs, subln)


def _ctx_attn_kernel(qkv_ref, lam_ref, subln_ref, ya_in, yd_in, ya_ref, yd_ref, acc_ref, *, lam_init):
    del ya_in, yd_in
    gw = GROUP_WIDTH
    col = lambda g: qkv_ref[:, g * gw:(g + 1) * gw]
    ya = _na_heads(col(0), [col(1)], lambda h: [None], [col(2)])
    ya_ref[...] = ya.astype(BF16)
    lam = _lambda(lam_ref, lam_init)
    yd = _diff_heads(col(3), [col(4)], [col(5)], lam, subln_ref[...], lam_init, acc_ref)
    yd_ref[...] = yd.astype(BF16)


def _ctx_attn_call(qkv, lam_vecs, subln, lam_init, ya, yd, n_batch, seq, ctx_len):
    gw = GROUP_WIDTH
    ctx_blk0 = n_batch * seq // ctx_len
    any_spec = pl.BlockSpec(memory_space=pl.ANY)
    out_spec = pl.BlockSpec((ctx_len, gw), lambda b: (ctx_blk0 + b, 0))
    return pl.pallas_call(
        functools.partial(_ctx_attn_kernel, lam_init=lam_init),
        grid=(n_batch,),
        in_specs=[pl.BlockSpec((ctx_len, 6 * gw), lambda b: (ctx_blk0 + b, 0)),
                  _full(lam_vecs.shape), _full(subln.shape), any_spec, any_spec],
        out_specs=[out_spec, out_spec],
        out_shape=[jax.ShapeDtypeStruct(ya.shape, BF16), jax.ShapeDtypeStruct(yd.shape, BF16)],
        input_output_aliases={3: 0, 4: 1},
        scratch_shapes=[pltpu.VMEM((ctx_len, gw), F32)],
        compiler_params=_cparams(1),
        name="ctx_attn",
    )(qkv, lam_vecs, subln, ya, yd)


def _pool_kernel(p_ref, w_ref, scale_ref, *rest, seq, aliased):
    o_ref, pad_ref = rest[-2], rest[-1]
    del aliased
    gw = GROUP_WIDTH
    halo = POOL_HALO
    pad_ref[0:halo, :] = jnp.zeros((halo, gw), F32)
    pad_ref[halo + seq:, :] = jnp.zeros((halo, gw), F32)
    pad_ref[halo:halo + seq, :] = p_ref[...]
    chunk = min(seq, 256)
    lane_group = lax.broadcasted_iota(jnp.int32, (1, gw), 1) // (gw // len(POOL_WINDOWS))
    for c0 in range(0, seq, chunk):
        def at(off):
            return pad_ref[halo + c0 + off:halo + c0 + off + chunk, :]
        pos = c0 + lax.broadcasted_iota(jnp.int32, (chunk, 1), 0)
        x = at(0)
        run = x
        mean = None
        lo_done, hi_done = 0, 0
        for g, win in enumerate(POOL_WINDOWS):
            half = win // 2
            for off in range(-half, -lo_done):
                run = run + at(off)
            for off in range(hi_done + 1, half):
                run = run + at(off)
            lo_done, hi_done = half, half - 1
            cnt = (jnp.minimum(pos + half, seq) - jnp.maximum(pos - half, 0)).astype(F32)
            m = run / cnt
            mean = m if mean is None else jnp.where(lane_group == g, m, mean)
        y = _dot((mean - x).astype(BF16), w_ref[...]) * scale_ref[...]
        o_ref[c0:c0 + chunk, :] = y.astype(BF16)


def _pool_call(pool_in, w_bd, scale, n_seq, seq, row_blk0, total_rows, prev=None):
    gw = GROUP_WIDTH
    in_specs = [pl.BlockSpec((seq, gw), lambda b: (row_blk0 + b, 0)), _full((gw, gw)), _full((1, gw))]
    args = [pool_in, w_bd, scale]
    aliases = {}
    if prev is not None:
        in_specs.append(pl.BlockSpec(memory_space=pl.ANY))
        args.append(prev)
        aliases = {3: 0}
    return pl.pallas_call(
        functools.partial(_pool_kernel, seq=seq, aliased=prev is not None),
        grid=(n_seq,),
        in_specs=in_specs,
        out_specs=pl.BlockSpec((seq, gw), lambda b: (row_blk0 + b, 0)),
        out_shape=jax.ShapeDtypeStruct((total_rows, gw), BF16),
        scratch_shapes=[pltpu.VMEM((seq + 2 * POOL_HALO, gw), F32)],
        input_output_aliases=aliases,
        compiler_params=_cparams(1),
        name="pool_mix",
    )(*args)


def _fft_kernel(t_ref, cl_ref, sl_ref, cc_ref, sc_ref, w_ref, *rest, norm):
    o_ref, a_ref, b_ref = rest[-3], rest[-2], rest[-1]

    @pl.when(pl.program_id(1) == 0)
    def _():
        t = t_ref[...]
        a_ref[...] = _dot(t, cc_ref[...]).astype(BF16)
        b_ref[...] = _dot(t, sc_ref[...]).astype(BF16)

    f = (_dot(cl_ref[...], a_ref[...]) - _dot(sl_ref[...], b_ref[...])) * norm
    o_ref[...] = _dot(f.astype(BF16), w_ref[...]).astype(BF16)


def _fft_call(fft_in, cl, sl, cc_bd, sc_bd, w_bd, n_seq, seq, row_blk0, total_rows, prev=None):
    gw = GROUP_WIDTH
    tk = min(FFT_TILE, seq)
    n_k = seq // tk
    in_specs = [pl.BlockSpec((seq, gw), lambda b, k: (row_blk0 + b, 0)),
                pl.BlockSpec((tk, seq), lambda b, k: (k, 0)),
                pl.BlockSpec((tk, seq), lambda b, k: (k, 0)),
                _full((gw, gw)), _full((gw, gw)), _full((gw, gw))]
    args = [fft_in, cl, sl, cc_bd, sc_bd, w_bd]
    aliases = {}
    if prev is not None:
        in_specs.append(pl.BlockSpec(memory_space=pl.ANY))
        args.append(prev)
        aliases = {6: 0}
    norm = 1.0 / math.sqrt(seq * (gw // 4))
    return pl.pallas_call(
        functools.partial(_fft_kernel, norm=norm),
        grid=(n_seq, n_k),
        in_specs=in_specs,
        out_specs=pl.BlockSpec((tk, gw), lambda b, k: ((row_blk0 + b) * n_k + k, 0)),
        out_shape=jax.ShapeDtypeStruct((total_rows, gw), BF16),
        scratch_shapes=[pltpu.VMEM((seq, gw), BF16), pltpu.VMEM((seq, gw), BF16)],
        input_output_aliases=aliases,
        compiler_params=_cparams(2),
        name="fourier_mix",
    )(*args)


def _out_kernel(ya_ref, yd_ref, yb_ref, yf_ref, w_ref, xa_ref, xb_ref, g1_ref, sh2_ref, sc2_ref, gn2_ref,
                rw_ref, rb_ref, x1_ref, h2_ref, lg_ref, *, n_lat_tiles):
    gw = GROUP_WIDTH
    x = jnp.where(pl.program_id(0) < n_lat_tiles, xa_ref[...], xb_ref[...])
    acc = _dot(ya_ref[...], w_ref[0:gw, :])
    acc = acc + _dot(yd_ref[...], w_ref[gw:2 * gw, :])
    acc = acc + _dot(yb_ref[...], w_ref[2 * gw:3 * gw, :])
    acc = acc + _dot(yf_ref[...], w_ref[3 * gw:4 * gw, :])
    x1 = x + g1_ref[0] * acc
    x1_ref[...] = x1
    ms = jnp.mean(x1 * x1, axis=-1, keepdims=True)
    h2 = x1 * lax.rsqrt(ms + NORM_EPS) * gn2_ref[...] * (1.0 + sc2_ref[0]) + sh2_ref[0]
    h_hi = h2.astype(BF16)
    h_lo = (h2 - h_hi.astype(F32)).astype(BF16)
    by_hi = _dot(h_hi, rw_ref[...])
    by_lo = _dot(h_lo, rw_ref[:, :LANES])
    lg_ref[...] = by_hi[:, :LANES] + by_hi[:, LANES:] + by_lo + rb_ref[...]
    tm = x1.shape[0]
    for s in range(SLABS):
        h2_ref[pl.ds(s, tm, stride=SLABS), :] = h2[:, s * LANES:(s + 1) * LANES]


def _out_call(ya, yd, yb, yf, w_out_bf, xa, xb, ctx_blk0, mod3, gn2, rw_pad, rb_pad, rows, n_lat_rows, seq):
    d = D_MODEL
    gw = GROUP_WIDTH
    tm = ROW_TILE
    n_lat_tiles = n_lat_rows // tm
    tiles_per_seq = seq // tm
    n_batch = n_lat_rows // seq

    def mod_row(i):
        return jnp.minimum(i // tiles_per_seq, n_batch)

    mix_spec = pl.BlockSpec((tm, gw), lambda i: (i, 0))
    mod_spec = lambda chunk: pl.BlockSpec((1, 1, d), lambda i: (mod_row(i), 0, chunk))
    return pl.pallas_call(
        functools.partial(_out_kernel, n_lat_tiles=n_lat_tiles),
        grid=(rows // tm,),
        in_specs=[mix_spec, mix_spec, mix_spec, mix_spec, _full((4 * gw, d))]
                 + _stream_specs(tm, d, n_lat_tiles, ctx_blk0) + [
                  mod_spec(2), mod_spec(3), mod_spec(4), _full((1, d)),
                  _full(rw_pad.shape), _full(rb_pad.shape)],
        out_specs=[pl.BlockSpec((tm, d), lambda i: (i, 0)),
                   pl.BlockSpec((tm * SLABS, LANES), lambda i: (i, 0)),
                   pl.BlockSpec((tm, LANES), lambda i: (i, 0))],
        out_shape=[jax.ShapeDtypeStruct((rows, d), F32),
                   jax.ShapeDtypeStruct((rows * SLABS, LANES), F32),
                   jax.ShapeDtypeStruct((rows, LANES), F32)],
        compiler_params=_cparams(1),
        name="out_proj",
    )(ya, yd, yb, yf, w_out_bf, xa, xb, mod3, mod3, mod3, gn2, rw_pad, rb_pad)


def _route_kernel(lg_ref, idx_ref, gate_ref, rank_ref, cnt_ref, carry_ref):
    i = pl.program_id(0)

    @pl.when(i == 0)
    def _():
        carry_ref[...] = jnp.zeros_like(carry_ref)

    lg = lg_ref[...]
    tm = lg.shape[0]
    lane = lax.broadcasted_iota(jnp.int32, (tm, LANES), 1)
    vals, idxs = [], []
    onehot = jnp.zeros((tm, LANES), F32)
    for _ in range(TOP_K):
        m = lg.max(axis=-1, keepdims=True)
        idx = jnp.min(jnp.where(lg == m, lane, LANES), axis=-1, keepdims=True)
        sel = lane == idx
        onehot = onehot + sel.astype(F32)
        lg = jnp.where(sel, -jnp.inf, lg)
        vals.append(m)
        idxs.append(idx)
    exps = [jnp.exp(v - vals[0]) for v in vals]
    denom = exps[0] + exps[1] + exps[2] + exps[3]
    r_i = lax.broadcasted_iota(jnp.int32, (tm, tm), 0)
    c_i = lax.broadcasted_iota(jnp.int32, (tm, tm), 1)
    tri = (c_i < r_i).astype(BF16)
    before = _dot(tri, onehot.astype(BF16)) + carry_ref[...]
    col4 = lax.broadcasted_iota(jnp.int32, (tm, TOP_K), 1)
    idx_o = jnp.zeros((tm, TOP_K), jnp.int32)
    gate_o = jnp.zeros((tm, TOP_K), F32)
    rank_o = jnp.zeros((tm, TOP_K), F32)
    for k in range(TOP_K):
        rk = jnp.sum(jnp.where(lane == idxs[k], before, 0.0), axis=-1, keepdims=True)
        idx_o = jnp.where(col4 == k, idxs[k], idx_o)
        gate_o = jnp.where(col4 == k, exps[k] / denom, gate_o)
        rank_o = jnp.where(col4 == k, rk, rank_o)
    idx_ref[...] = idx_o
    gate_ref[...] = gate_o
    rank_ref[...] = rank_o.astype(jnp.int32)
    carry_ref[...] = carry_ref[...] + jnp.sum(onehot, axis=0, keepdims=True)
    cnt_ref[...] = carry_ref[...].astype(jnp.int32)


def _route_call(logits):
    n = logits.shape[0]
    tm = ROW_TILE
    k_spec = pl.BlockSpec((tm, TOP_K), lambda i: (i, 0))
    return pl.pallas_call(
        _route_kernel,
        grid=(n // tm,),
        in_specs=[pl.BlockSpec((tm, LANES), lambda i: (i, 0))],
        out_specs=[k_spec, k_spec, k_spec, _full((1, LANES))],
        out_shape=[jax.ShapeDtypeStruct((n, TOP_K), jnp.int32),
                   jax.ShapeDtypeStruct((n, TOP_K), F32),
                   jax.ShapeDtypeStruct((n, TOP_K), jnp.int32),
                   jax.ShapeDtypeStruct((1, LANES), jnp.int32)],
        scratch_shapes=[pltpu.VMEM((1, LANES), F32)],
        compiler_params=_cparams(1),
        name="route",
    )(logits)


def _row_copy(src_ref, src_row, dst_ref, dst_row, sem):
    return pltpu.make_async_copy(
        src_ref.at[pl.ds(pl.multiple_of(src_row * SLABS, SLABS), SLABS), :],
        dst_ref.at[pl.ds(pl.multiple_of(dst_row * SLABS, SLABS), SLABS), :], sem)


def _pad_copy(zero_ref, hs_ref, start_row, n_rows, sem):
    return pltpu.make_async_copy(
        zero_ref.at[pl.ds(0, n_rows * SLABS), :],
        hs_ref.at[pl.ds(pl.multiple_of(start_row * SLABS, SLABS), n_rows * SLABS), :], sem)


def _dispatch_kernel(fill_ref, pad_ref, dest_hbm, h2_ref, hs_ref, dest_smem, zero_ref, sem, isem):
    i = pl.program_id(0)
    n_assign = TOKEN_TILE * TOP_K
    idx_copy = pltpu.make_async_copy(dest_hbm.at[pl.ds(i * n_assign, n_assign)], dest_smem, isem)
    idx_copy.start()

    @pl.when(i == 0)
    def _():
        zero_ref[...] = jnp.zeros_like(zero_ref)
        bits = [1 << b for b in reversed(range(int(math.log2(EXPERT_BLOCK))))]
        for phase in ("start", "wait"):
            def fill(e, carry, phase=phase):
                pos = fill_ref[e]
                pad = pad_ref[e]
                for bit in bits:
                    @pl.when((pad & bit) != 0)
                    def _(pos=pos, bit=bit):
                        cp = _pad_copy(zero_ref, hs_ref, pos, bit, sem)
                        cp.start() if phase == "start" else cp.wait()
                    pos = pos + (pad & bit)
                return carry

            lax.fori_loop(0, N_EXPERTS, fill, 0)

    idx_copy.wait()

    def start(t, carry):
        for k in range(TOP_K):
            _row_copy(h2_ref, t, hs_ref, dest_smem[t * TOP_K + k], sem).start()
        return carry

    lax.fori_loop(0, TOKEN_TILE, start, 0, unroll=ISSUE_UNROLL)
    for _ in range(TOP_K):
        pltpu.make_async_copy(h2_ref, hs_ref.at[pl.ds(0, TOKEN_TILE * SLABS), :], sem).wait()


def _dispatch_call(fill_start, pad_len, dest_flat, h2_slabs, n_slots):
    n = h2_slabs.shape[0] // SLABS
    tt = TOKEN_TILE
    grid_spec = pltpu.PrefetchScalarGridSpec(
        num_scalar_prefetch=2,
        grid=(n // tt,),
        in_specs=[pl.BlockSpec(memory_space=pl.ANY),
                  pl.BlockSpec((tt * SLABS, LANES), lambda i, *_: (i, 0))],
        out_specs=pl.BlockSpec(memory_space=pl.ANY),
        scratch_shapes=[pltpu.SMEM((tt * TOP_K,), jnp.int32),
                        pltpu.VMEM((EXPERT_BLOCK // 2 * SLABS, LANES), F32),
                        pltpu.SemaphoreType.DMA, pltpu.SemaphoreType.DMA],
    )
    return pl.pallas_call(
        _dispatch_kernel,
        grid_spec=grid_spec,
        out_shape=jax.ShapeDtypeStruct((n_slots * SLABS, LANES), F32),
        compiler_params=_cparams(1),
        name="dispatch",
    )(fill_start, pad_len, dest_flat, h2_slabs)


GLU_BLOCK = 2 * LANES


def _expert_kernel(be_ref, first_ref, nv_ref, hs_ref, w1_ref, b1_ref, w2_ref, b2_ref, perm_ref,
                   ys_ref, h_ref, act_ref, w1p_ref, w2p_ref):
    del be_ref
    i = pl.program_id(0)
    n_glu = 2 * D_FF // GLU_BLOCK

    @pl.when((first_ref[i] != 0) & (i < nv_ref[0]))
    def _():
        for b in range(n_glu):
            cols = slice(b * GLU_BLOCK, (b + 1) * GLU_BLOCK)
            w1p_ref[:, cols] = _dot(w1_ref[0, :, cols].astype(BF16), perm_ref[...]).astype(BF16)
        w2p_ref[...] = w2_ref[0].astype(BF16)

    @pl.when(i < nv_ref[0])
    def _():
        blk = EXPERT_BLOCK
        for s in range(SLABS):
            h_ref[:, s * LANES:(s + 1) * LANES] = hs_ref[pl.ds(s, blk, stride=SLABS), :].astype(BF16)
        h = h_ref[...]
        for b in range(n_glu):
            cols = slice(b * GLU_BLOCK, (b + 1) * GLU_BLOCK)
            u = _dot(h, w1p_ref[:, cols]) + b1_ref[0, :, cols]
            glu = jnp.minimum(u[:, :LANES], SWIGLU_LIMIT)
            lin = jnp.clip(u[:, LANES:], -SWIGLU_LIMIT, SWIGLU_LIMIT)
            act = glu * jax.nn.sigmoid(SWIGLU_ALPHA * glu) * (lin + 1.0)
            act_ref[:, b * LANES:(b + 1) * LANES] = act.astype(BF16)
        y = _dot(act_ref[...], w2p_ref[...]) + b2_ref[0]
        for s in range(SLABS):
            ys_ref[pl.ds(s, blk, stride=SLABS), :] = y[:, s * LANES:(s + 1) * LANES]


def _expert_call(block_e, first, n_valid, hs, w1, b1p, w2, b2, perm, layer, n_blocks):
    blk = EXPERT_BLOCK
    d = D_MODEL

    def row_blk(i, be, first, nv):
        return (jnp.minimum(i, nv[0] - 1), 0)

    def expert_blk(i, be, first, nv):
        return (be[i], 0, 0)

    def weight_blk(i, be, first, nv):
        return (layer, be[i], 0, 0)

    grid_spec = pltpu.PrefetchScalarGridSpec(
        num_scalar_prefetch=3,
        grid=(n_blocks,),
        in_specs=[pl.BlockSpec((blk * SLABS, LANES), row_blk),
                  pl.BlockSpec((None, 1, d, 2 * D_FF), weight_blk),
                  pl.BlockSpec((1, 1, 2 * D_FF), expert_blk),
                  pl.BlockSpec((None, 1, D_FF, d), weight_blk),
                  pl.BlockSpec((1, 1, d), expert_blk),
                  pl.BlockSpec((GLU_BLOCK, GLU_BLOCK), lambda i, *_: (0, 0))],
        out_specs=pl.BlockSpec((blk * SLABS, LANES), row_blk),
        scratch_shapes=[pltpu.VMEM((blk, d), BF16), pltpu.VMEM((blk, D_FF), BF16),
                        pltpu.VMEM((d, 2 * D_FF), BF16), pltpu.VMEM((D_FF, d), BF16)],
    )
    return pl.pallas_call(
        _expert_kernel,
        grid_spec=grid_spec,
        out_shape=jax.ShapeDtypeStruct(hs.shape, F32),
        compiler_params=_cparams(1),
        name="expert_ffn",
    )(block_e, first, n_valid, hs, w1, b1p, w2, b2, perm)


def _combine_kernel(dest_hbm, ys_hbm, gate_ref, x_ref, g2_ref, o_ref,
                    idx0, idx1, buf0, buf1, sems, isems):
    i = pl.program_id(0)
    n_tiles = pl.num_programs(0)
    tt = TOKEN_TILE
    n_assign = tt * TOP_K
    idx_refs, buf_refs = (idx0, idx1), (buf0, buf1)

    def idx_copy(tile, s):
        return pltpu.make_async_copy(dest_hbm.at[pl.ds(tile * n_assign, n_assign)], idx_refs[s],
                                     isems.at[s])

    def issue(s):
        def start(t, carry):
            for k in range(TOP_K):
                _row_copy(ys_hbm, idx_refs[s][t * TOP_K + k], buf_refs[s], k * tt + t, sems.at[s]).start()
            return carry
        lax.fori_loop(0, tt, start, 0, unroll=ISSUE_UNROLL)

    @pl.when(i == 0)
    def _():
        idx_copy(0, 0).start()
        idx_copy(0, 0).wait()
        issue(0)

        @pl.when(n_tiles > 1)
        def _():
            idx_copy(1, 1).start()

    for s in (0, 1):
        @pl.when(i % 2 == s)
        def _(s=s):
            @pl.when(i + 1 < n_tiles)
            def _():
                idx_copy(i + 1, 1 - s).wait()
                issue(1 - s)

            pltpu.make_async_copy(ys_hbm.at[pl.ds(0, n_assign * SLABS), :], buf_refs[s], sems.at[s]).wait()

            @pl.when(i + 2 < n_tiles)
            def _():
                idx_copy(i + 2, s).start()

            gate = gate_ref[...]
            g2 = g2_ref[0]
            for sl in range(SLABS):
                cols = slice(sl * LANES, (sl + 1) * LANES)
                y = jnp.zeros((tt, LANES), F32)
                for k in range(TOP_K):
                    y = y + gate[:, k:k + 1] * buf_refs[s][pl.ds(k * tt * SLABS + sl, tt, stride=SLABS), :]
                o_ref[:, cols] = x_ref[:, cols] + g2[:, cols] * y


def _combine_call(dest_flat, ys, gate, x1, mod3, n_lat_rows, seq):
    n, d = x1.shape
    tt = TOKEN_TILE
    tiles_per_seq = seq // tt
    n_batch = n_lat_rows // seq

    def mod_row(i):
        return jnp.minimum(i // tiles_per_seq, n_batch)

    return pl.pallas_call(
        _combine_kernel,
        grid=(n // tt,),
        in_specs=[pl.BlockSpec(memory_space=pl.ANY), pl.BlockSpec(memory_space=pl.ANY),
                  pl.BlockSpec((tt, TOP_K), lambda i: (i, 0)),
                  pl.BlockSpec((tt, d), lambda i: (i, 0)),
                  pl.BlockSpec((1, 1, d), lambda i: (mod_row(i), 0, 5))],
        out_specs=pl.BlockSpec((tt, d), lambda i: (i, 0)),
        out_shape=jax.ShapeDtypeStruct((n, d), F32),
        scratch_shapes=[pltpu.SMEM((tt * TOP_K,), jnp.int32), pltpu.SMEM((tt * TOP_K,), jnp.int32),
                        pltpu.VMEM((TOP_K * tt * SLABS, LANES), F32),
                        pltpu.VMEM((TOP_K * tt * SLABS, LANES), F32),
                        pltpu.SemaphoreType.DMA((2,)), pltpu.SemaphoreType.DMA((2,))],
        compiler_params=_cparams(1),
        name="combine",
    )(dest_flat, ys, gate, x1, mod3)


def _block_diag(blocks):
    g, a, b = blocks.shape
    eye = jnp.eye(g, dtype=blocks.dtype)
    return (eye[:, None, :, None] * blocks[:, :, None, :]).reshape(g * a, g * b)


def _rope_tables(seq, extra_rows):
    half = DIFF_QK_DIM // 2
    inv = ROPE_THETA ** (-jnp.arange(0, half, 2, dtype=F32) / half)
    pos = jnp.arange(seq)
    c = jnp.arange(GROUP_WIDTH)
    dd = c % DIFF_QK_DIM
    axis_pos = jnp.where((dd < half)[None, :], (pos // GRID_W)[:, None], (pos % GRID_W)[:, None])
    ang = axis_pos.astype(F32) * inv[dd % (half // 2)][None, :]
    sign = jnp.where((dd % half) < half // 2, -1.0, 1.0).astype(F32)
    cos_t = jnp.concatenate([jnp.cos(ang), jnp.ones((extra_rows, GROUP_WIDTH), F32)], axis=0)
    sin_t = jnp.concatenate([jnp.sin(ang) * sign[None, :], jnp.zeros((extra_rows, GROUP_WIDTH), F32)], axis=0)
    return cos_t, sin_t


def _dft_tables(n, dtype):
    k = np.arange(n, dtype=np.int64)
    ang = ((k[:, None] * k[None, :]) % n).astype(np.float64) * (2.0 * math.pi / n)
    return jnp.asarray(np.cos(ang), dtype), jnp.asarray(np.sin(ang), dtype)


def _na_bias_tables(rpb, grid_rows):
    w = GRID_W
    col = np.arange(w)
    col_start = np.clip(col - NA_COLS // 2, 0, w - NA_COLS)
    in_win = (col[None, :] >= col_start[:, None]) & (col[None, :] < col_start[:, None] + NA_COLS)
    rel_c = np.clip(col[None, :] - col[:, None], 1 - NA_COLS, NA_COLS - 1) + NA_COLS - 1
    n_tiles = grid_rows // NA_Q_ROWS
    cases = (0, 1, n_tiles - 1)
    rel_r = np.zeros((3, NA_Q_ROWS, NA_KEY_ROWS), np.int32)
    valid = np.zeros((3, NA_Q_ROWS, NA_KEY_ROWS), bool)
    for ci, tile in enumerate(cases):
        r0 = tile * NA_Q_ROWS
        k0 = int(np.clip(r0 - NA_ROWS // 2, 0, grid_rows - NA_KEY_ROWS))
        for j in range(NA_Q_ROWS):
            start = int(np.clip(r0 + j - NA_ROWS // 2, 0, grid_rows - NA_ROWS))
            for i in range(NA_KEY_ROWS):
                kr = k0 + i
                valid[ci, j, i] = start <= kr < start + NA_ROWS
                rel_r[ci, j, i] = np.clip(kr - (r0 + j) + NA_ROWS - 1, 0, 2 * NA_ROWS - 2)
    del rel_c
    edge = w - NA_COLS
    ext = jnp.pad(rpb.astype(F32), ((0, 0), (0, 0), (edge, edge)), mode='edge')
    t = jnp.stack([ext[:, :, w - 1 - q:2 * w - 1 - q] for q in range(w)], axis=2)
    t = jnp.where(in_win[None, None], t, MASK_VALUE)
    masked = jnp.full((rpb.shape[0], w, w), MASK_VALUE, F32)
    b = jnp.stack([jnp.stack([jnp.stack([t[:, rel_r[c, j, i]] if valid[c, j, i] else masked
                                         for i in range(NA_KEY_ROWS)], axis=0)
                              for j in range(NA_Q_ROWS)], axis=0)
                   for c in range(3)], axis=0)
    b = b.transpose(0, 3, 1, 4, 2, 5)
    return b.reshape(3, rpb.shape[0], NA_Q_ROWS * w, NA_KEY_ROWS * w)


def _moe(h2_slabs, logits, x1, mod3, w1, b1p, w2, b2, perm, layer, n_lat_rows, seq):
    n = x1.shape[0]
    blk = EXPERT_BLOCK
    idx, gate, rank, counts = _route_call(logits)
    counts = counts[0, :N_EXPERTS]
    padded = (counts + blk - 1) // blk * blk
    padded_end = jnp.cumsum(padded)
    padded_start = padded_end - padded
    dest = (jnp.take(padded_start, idx.reshape(n * TOP_K)) + rank.reshape(n * TOP_K)).astype(jnp.int32)
    n_blocks = n * TOP_K // blk + N_EXPERTS
    block_row0 = jnp.arange(n_blocks, dtype=jnp.int32) * blk
    block_e = jnp.minimum(jnp.sum((padded_end[None, :] <= block_row0[:, None]).astype(jnp.int32), axis=1),
                          N_EXPERTS - 1).astype(jnp.int32)
    first = jnp.concatenate([jnp.ones((1,), jnp.int32),
                             (block_e[1:] != block_e[:-1]).astype(jnp.int32)])
    n_valid = (padded_end[-1:] // blk).astype(jnp.int32)
    fill_start = (padded_start + counts).astype(jnp.int32)
    pad_len = (padded - counts).astype(jnp.int32)
    hs = _dispatch_call(fill_start, pad_len, dest, h2_slabs, n_blocks * blk)
    ys = _expert_call(block_e, first, n_valid, hs, w1, b1p, w2, b2, perm, layer, n_blocks)
    return _combine_call(dest, ys, gate, x1, mod3, n_lat_rows, seq)


def kernel(x, c, ctx, c_ctx, w_ada, b_ada, g_norm1, g_norm2, w_in, w_out, na_q_gain, na_k_gain, na_rpb, diff_q_gain, diff_k_gain, diff_lambda_q1, diff_lambda_k1, diff_lambda_q2, diff_lambda_k2, diff_subln, pool_w, pool_scale, fft_w, router_w, router_b, moe_w1, moe_b1, moe_w2, moe_b2):
    n_batch, seq, d = x.shape
    ctx_len = ctx.shape[1]
    depth = w_ada.shape[0]
    gw = GROUP_WIDTH
    assert d == D_MODEL and seq % ROW_TILE == 0 and seq % ctx_len == 0 and ctx_len % TOKEN_TILE == 0
    assert (n_batch * ctx_len) % ROW_TILE == 0 and (seq // GRID_W) >= NA_KEY_ROWS
    n_lat = n_batch * seq
    n_ctx = n_batch * ctx_len

    xa, xb, ctx_blk0 = x.reshape(n_lat, d), ctx.reshape(n_ctx, d), 0
    mod_rows = -(-(n_batch + 1) // SUBLANES) * SUBLANES
    cvec = jnp.zeros((mod_rows, d), F32).at[:n_batch].set(c).at[n_batch].set(c_ctx)
    mod = _ada_call(cvec, w_ada, b_ada)

    cos_t, sin_t = _rope_tables(seq, ROW_TILE)
    ones = lambda w: _block_diag(jnp.ones((gw // w, w, w), BF16))
    g64, g32 = ones(HEAD_DIM), ones(DIFF_QK_DIM)
    cl_lat, sl_lat = _dft_tables(seq, BF16)
    cl_ctx, sl_ctx = _dft_tables(ctx_len, BF16)
    cc, sc = _dft_tables(gw // 4, F32)
    n_grp = fft_w.shape[1]
    cc_bd = _block_diag(jnp.broadcast_to(cc, (n_grp,) + cc.shape)).astype(BF16)
    sc_bd = _block_diag(jnp.broadcast_to(sc, (n_grp,) + sc.shape)).astype(BF16)
    src = np.arange(GLU_BLOCK)
    dst = np.where(src % 2 == 0, src // 2, LANES + src // 2)
    perm_np = np.zeros((GLU_BLOCK, GLU_BLOCK), np.float32)
    perm_np[src, dst] = 1.0
    glu_perm = jnp.asarray(perm_np, BF16)

    for l in range(depth):
        ctx_out = l < depth - 1
        lam_init = 0.8 - 0.6 * math.exp(-0.3 * l)
        mod3 = mod[l].reshape(mod_rows, 1, 6 * d)
        tile = lambda v: jnp.tile(v.astype(F32), gw // v.shape[0])
        gains = jnp.stack([tile(na_q_gain[l]) * (HEAD_DIM ** -0.5 * LOG2_E), tile(na_k_gain[l]),
                           tile(diff_q_gain[l]) * (DIFF_QK_DIM ** -0.5 * LOG2_E), tile(diff_k_gain[l])]
                          + [jnp.zeros((gw,), F32)] * 4)
        lam_vecs = jnp.stack([diff_lambda_q1[l], diff_lambda_k1[l], diff_lambda_q2[l], diff_lambda_k2[l]]
                             + [jnp.zeros_like(diff_lambda_q1[l])] * 4).astype(F32)
        subln = tile(diff_subln[l]).reshape(1, gw)
        bias_tab = _na_bias_tables(na_rpb[l] * LOG2_E, seq // GRID_W)
        pool_bd = _block_diag(pool_w[l]).astype(BF16)
        fftw_bd = _block_diag(fft_w[l]).astype(BF16)
        rw_f32 = jnp.zeros((d, LANES), F32).at[:, :N_EXPERTS].set(router_w[l])
        rw_hi = rw_f32.astype(BF16)
        rw_pad = jnp.concatenate([rw_hi, (rw_f32 - rw_hi.astype(F32)).astype(BF16)], axis=1)
        rb_pad = jnp.full((1, LANES), -jnp.inf, F32).at[0, :N_EXPERTS].set(router_b[l])
        n_e = moe_w1.shape[1]
        b1p = moe_b1[l].reshape(n_e, 2 * D_FF // GLU_BLOCK, LANES, 2).transpose(0, 1, 3, 2)
        b1p = b1p.reshape(n_e, 1, 2 * D_FF)
        b2 = moe_b2[l].reshape(n_e, 1, d)

        qkv, pool_in, fft_in = _in_call(xa, xb, ctx_blk0, n_lat + n_ctx, mod3, g_norm1[l].reshape(1, d),
                                        w_in[l].astype(BF16), gains, cos_t, sin_t, g64, g32, n_lat, seq)
        rows = n_lat + n_ctx if ctx_out else n_lat
        ya = _na_call(qkv, bias_tab, n_batch, seq, rows, ctx_len)
        yd = _df_call(qkv, lam_vecs, subln, lam_init, n_batch, seq, rows, ctx_len)
        yb = _pool_call(pool_in, pool_bd, pool_scale[l].reshape(1, gw), n_batch, seq, 0, rows)
        yf = _fft_call(fft_in, cl_lat, sl_lat, cc_bd, sc_bd, fftw_bd, n_batch, seq, 0, rows)
        if ctx_out:
            ya, yd = _ctx_attn_call(qkv, lam_vecs, subln, lam_init, ya, yd, n_batch, seq, ctx_len)
            yb = _pool_call(pool_in, pool_bd, pool_scale[l].reshape(1, gw), n_batch, ctx_len,
                            n_lat // ctx_len, rows, prev=yb)
            yf = _fft_call(fft_in, cl_ctx, sl_ctx, cc_bd, sc_bd, fftw_bd, n_batch, ctx_len,
                           n_lat // ctx_len, rows, prev=yf)
        x1, h2_slabs, logits = _out_call(ya, yd, yb, yf, w_out[l].astype(BF16), xa, xb, ctx_blk0, mod3,
                                         g_norm2[l].reshape(1, d), rw_pad, rb_pad, rows, n_lat, seq)
        x2 = _moe(h2_slabs, logits, x1, mod3, moe_w1, b1p, moe_w2, b2, glu_perm, l, n_lat, seq)
        xa, xb, ctx_blk0 = x2, x2, n_lat // ROW_TILE
    return x2[:n_lat].reshape(n_batch, seq, d)
```

```python
import functools
import math

import numpy as np
import jax
import jax.numpy as jnp
from jax import lax
from jax.experimental import pallas as pl
from jax.experimental.pallas import tpu as pltpu

F32 = jnp.float32
BF16 = jnp.bfloat16
HIGHEST = lax.Precision.HIGHEST

D_MODEL = 1024
DEPTH = 2
GRID_W = 64
HEAD_DIM = 64
GROUP_WIDTH = 256
N_HEADS = GROUP_WIDTH // HEAD_DIM
DIFF_QK_DIM = HEAD_DIM // 2
NA_ROWS = 8
NA_COLS = 16
POOL_WINDOWS = (2, 4, 8, 16)
POOL_HALO = max(POOL_WINDOWS) // 2
N_EXPERTS = 32
TOP_K = 4
D_FF = D_MODEL
SWIGLU_ALPHA = 1.702
SWIGLU_LIMIT = 7.0
ROPE_THETA = 10000.0
NORM_EPS = 1e-6
MASK_VALUE = -1e30

LANES = 128
SUBLANES = 8
ROW_TILE = 512
TOKEN_TILE = 256
EXPERT_BLOCK = 512
Q_TILE = 256
DIFF_Q_TILE = 512
LOG2_E = math.log2(math.e)
NA_Q_ROWS = Q_TILE // GRID_W
NA_KEY_ROWS = NA_ROWS + NA_Q_ROWS - 1
FFT_TILE = 512
SLABS = D_MODEL // LANES
ISSUE_UNROLL = 4
VMEM_LIMIT = 48 * 1024 * 1024
EXPERT_VMEM_LIMIT = 56 * 1024 * 1024


def _cparams(n_axes, vmem_limit=VMEM_LIMIT):
    return pltpu.CompilerParams(dimension_semantics=("arbitrary",) * n_axes,
                                vmem_limit_bytes=vmem_limit)


def _dot(a, b):
    return jnp.dot(a, b, preferred_element_type=F32)


def _dot_nt(a, b):
    return lax.dot_general(a, b, (((1,), (1,)), ((), ())), preferred_element_type=F32)


def _full(shape):
    zeros = (0,) * len(shape)
    return pl.BlockSpec(shape, lambda *_: zeros)


def _ada_kernel(c_ref, w_ref, b_ref, o_ref):
    c = c_ref[...]
    act = c * jax.nn.sigmoid(c)
    o_ref[0] = jnp.dot(act, w_ref[0], precision=HIGHEST, preferred_element_type=F32) + b_ref[0]


def _ada_call(cvec, w_ada, b_ada):
    depth, d, n = w_ada.shape
    r = cvec.shape[0]
    tn = 1024
    return pl.pallas_call(
        _ada_kernel,
        grid=(depth, n // tn),
        in_specs=[_full((r, d)),
                  pl.BlockSpec((1, d, tn), lambda l, j: (l, 0, j)),
                  pl.BlockSpec((1, 1, tn), lambda l, j: (l, 0, j))],
        out_specs=pl.BlockSpec((1, r, tn), lambda l, j: (l, 0, j)),
        out_shape=jax.ShapeDtypeStruct((depth, r, n), F32),
        compiler_params=_cparams(2),
        name="ada_mod",
    )(cvec, w_ada, b_ada.reshape(depth, 1, n))


def _seg_mean_sq(p, gmat_ref, width):
    sq = p * p
    hi = sq.astype(BF16)
    lo = (sq - hi.astype(F32)).astype(BF16)
    g = gmat_ref[...]
    return (_dot(hi, g) + _dot(lo, g)) * (1.0 / width)


def _stream_specs(tm, d, n_lat_tiles, ctx_blk0):
    return [pl.BlockSpec((tm, d), lambda i: (jnp.minimum(i, n_lat_tiles - 1), 0)),
            pl.BlockSpec((tm, d), lambda i: (ctx_blk0 + jnp.maximum(i - n_lat_tiles, 0), 0))]


def _in_kernel(xa_ref, xb_ref, sh_ref, sc_ref, g_ref, w_ref, gains_ref, cos_ref, sin_ref, g64_ref,
               g32_ref, qkv_ref, pool_ref, fft_ref, *, n_lat_tiles):
    x = jnp.where(pl.program_id(0) < n_lat_tiles, xa_ref[...], xb_ref[...])
    ms = jnp.mean(x * x, axis=-1, keepdims=True)
    y = x * lax.rsqrt(ms + NORM_EPS) * g_ref[...]
    h = (y * (1.0 + sc_ref[0]) + sh_ref[0]).astype(BF16)
    gw = GROUP_WIDTH

    def proj(g):
        return _dot(h, w_ref[:, g * gw:(g + 1) * gw])

    def put(g, val):
        qkv_ref[:, g * gw:(g + 1) * gw] = val.astype(BF16)

    def normed(p, gmat_ref, width, row):
        return p * lax.rsqrt(_seg_mean_sq(p, gmat_ref, width) + NORM_EPS) * gains_ref[row:row + 1, :]

    lane = lax.broadcasted_iota(jnp.int32, (1, gw), 1)
    first_half = (lane % 16) < 8

    def rope(p):
        rot = jnp.where(first_half, pltpu.roll(p, gw - 8, 1), pltpu.roll(p, 8, 1))
        return p * cos_ref[...] + rot * sin_ref[...]

    put(0, normed(proj(0), g64_ref, HEAD_DIM, 0))
    put(1, normed(proj(1), g64_ref, HEAD_DIM, 1))
    put(2, proj(2))
    put(3, rope(normed(proj(3), g32_ref, DIFF_QK_DIM, 2)))
    put(4, rope(normed(proj(4), g32_ref, DIFF_QK_DIM, 3)))
    put(5, proj(5))
    pool_ref[...] = proj(6)
    fft_ref[...] = proj(7).astype(BF16)


def _in_call(xa, xb, ctx_blk0, rows, mod3, g1, w_in_bf, gains, cos_t, sin_t, g64, g32, n_lat_rows, seq):
    d = xa.shape[1]
    tm = ROW_TILE
    n_lat_tiles = n_lat_rows // tm
    tiles_per_seq = seq // tm
    n_batch = n_lat_rows // seq
    gw = GROUP_WIDTH

    def mod_row(i):
        return jnp.minimum(i // tiles_per_seq, n_batch)

    def tab_row(i):
        return jnp.where(i < n_lat_tiles, i % tiles_per_seq, tiles_per_seq)

    return pl.pallas_call(
        functools.partial(_in_kernel, n_lat_tiles=n_lat_tiles),
        grid=(rows // tm,),
        in_specs=_stream_specs(tm, d, n_lat_tiles, ctx_blk0) + [
                  pl.BlockSpec((1, 1, d), lambda i: (mod_row(i), 0, 0)),
                  pl.BlockSpec((1, 1, d), lambda i: (mod_row(i), 0, 1)),
                  _full((1, d)),
                  _full(w_in_bf.shape),
                  _full(gains.shape),
                  pl.BlockSpec((tm, gw), lambda i: (tab_row(i), 0)),
                  pl.BlockSpec((tm, gw), lambda i: (tab_row(i), 0)),
                  _full((gw, gw)),
                  _full((gw, gw))],
        out_specs=[pl.BlockSpec((tm, 6 * gw), lambda i: (i, 0)),
                   pl.BlockSpec((tm, gw), lambda i: (i, 0)),
                   pl.BlockSpec((tm, gw), lambda i: (i, 0))],
        out_shape=[jax.ShapeDtypeStruct((rows, 6 * gw), BF16),
                   jax.ShapeDtypeStruct((rows, gw), F32),
                   jax.ShapeDtypeStruct((rows, gw), BF16)],
        compiler_params=_cparams(1),
        name="in_proj",
    )(xa, xb, mod3, mod3, g1, w_in_bf, gains, cos_t, sin_t, g64, g32)


def _lane_mask(width, start, size):
    lane = lax.broadcasted_iota(jnp.int32, (1, width), 1)
    return (lane >= start) & (lane < start + size)


def _softmax_pv(qm, keys, biases, vals, sum_lane=None):
    scores = []
    for k, bias in zip(keys, biases):
        s = _dot_nt(qm, k)
        scores.append(s if bias is None else s + bias)
    m = scores[0].max(axis=-1, keepdims=True)
    for s in scores[1:]:
        m = jnp.maximum(m, s.max(axis=-1, keepdims=True))
    o = None
    l = None
    for s, v in zip(scores, vals):
        e = jnp.exp2(s - m)
        part = _dot(e.astype(BF16), v)
        o = part if o is None else o + part
        if sum_lane is None:
            ls = e.sum(axis=-1, keepdims=True)
            l = ls if l is None else l + ls
    if sum_lane is not None:
        lane = lax.broadcasted_iota(jnp.int32, (1, o.shape[1]), 1)
        l = jnp.sum(jnp.where(lane == sum_lane, o, 0.0), axis=-1, keepdims=True)
    return o, l


def _na_heads(q, keys, bias_fn, vals):
    gw = GROUP_WIDTH
    acc = jnp.zeros((q.shape[0], gw), F32)
    for h in range(N_HEADS):
        mask = _lane_mask(gw, h * HEAD_DIM, HEAD_DIM)
        qm = jnp.where(mask, q, jnp.zeros_like(q))
        o, l = _softmax_pv(qm, keys, bias_fn(h), vals)
        acc = acc + jnp.where(mask, o / l, 0.0)
    return acc


def _lambda(lam_ref, lam_init):
    lv = lam_ref[...]
    d1 = jnp.sum(lv[0:1, :] * lv[1:2, :], axis=-1, keepdims=True)
    d2 = jnp.sum(lv[2:3, :] * lv[3:4, :], axis=-1, keepdims=True)
    return jnp.exp(d1) - jnp.exp(d2) + lam_init


def _diff_heads(q, keys, vals, lam, subln, lam_init, acc_ref):
    gw = GROUP_WIDTH
    lane = lax.broadcasted_iota(jnp.int32, (1, gw), 1)
    nones = [None] * len(keys)
    acc_ref[...] = jnp.zeros_like(acc_ref)

    del nones

    def scores_of(h, part):
        plo = h * HEAD_DIM + part * DIFF_QK_DIM
        qm = jnp.where((lane >= plo) & (lane < plo + DIFF_QK_DIM), q, jnp.zeros_like(q))
        scores = [_dot_nt(qm, k) for k in keys]
        m = scores[0].max(axis=-1, keepdims=True)
        for s in scores[1:]:
            m = jnp.maximum(m, s.max(axis=-1, keepdims=True))
        return scores, m

    def attend(h, scores, m):
        lo = h * HEAD_DIM
        hm = (lane >= lo) & (lane < lo + HEAD_DIM)
        o = None
        for s, v in zip(scores, vals):
            part = _dot(jnp.exp2(s - m).astype(BF16), jnp.where(hm, v, jnp.ones_like(v)))
            o = part if o is None else o + part
        l = jnp.sum(jnp.where(lane == (lo + HEAD_DIM) % gw, o, 0.0), axis=-1, keepdims=True)
        return o / l

    combos = [(h, part) for h in range(N_HEADS) for part in range(2)]
    pending = [scores_of(*combos[0])]
    outs = []
    for c, (h, part) in enumerate(combos):
        if c + 1 < len(combos):
            pending.append(scores_of(*combos[c + 1]))
        outs.append(attend(h, *pending.pop(0)))
        if part == 1:
            lo = h * HEAD_DIM
            hm = (lane >= lo) & (lane < lo + HEAD_DIM)
            a = jnp.where(hm, outs[-2] - lam * outs[-1], 0.0)
            ms = jnp.sum(a * a, axis=-1, keepdims=True) * (1.0 / HEAD_DIM)
            acc_ref[...] += a * lax.rsqrt(ms + NORM_EPS)
    return acc_ref[...] * subln * (1.0 - lam_init)


def _na_kernel(q_ref, k_ref, v_ref, kc_ref, vc_ref, bias_ref, o_ref, *, grid_rows):
    j = pl.program_id(1)
    key_row0 = jnp.clip(j * NA_Q_ROWS - NA_ROWS // 2, 0, grid_rows - NA_KEY_ROWS)
    ks = pl.multiple_of(key_row0 * GRID_W, GRID_W)
    n_win = NA_KEY_ROWS * GRID_W
    kwin = k_ref[pl.ds(ks, n_win), :]
    vwin = v_ref[pl.ds(ks, n_win), :]
    acc = _na_heads(q_ref[...], [kwin, kc_ref[...]], lambda h: [bias_ref[0, h], None],
                    [vwin, vc_ref[...]])
    o_ref[...] = acc.astype(BF16)


def _na_call(qkv, bias_tab, n_batch, seq, total_rows, ctx_len):
    gw = GROUP_WIDTH
    qt = Q_TILE
    n_q = seq // qt
    grid_rows = seq // GRID_W
    ctx_blk0 = n_batch * seq // ctx_len

    def bias_case(j):
        return jnp.where(j == 0, 0, jnp.where(j == n_q - 1, 2, 1))

    return pl.pallas_call(
        functools.partial(_na_kernel, grid_rows=grid_rows),
        grid=(n_batch, n_q),
        in_specs=[pl.BlockSpec((qt, gw), lambda b, j: (b * n_q + j, 0)),
                  pl.BlockSpec((seq, gw), lambda b, j: (b, 1)),
                  pl.BlockSpec((seq, gw), lambda b, j: (b, 2)),
                  pl.BlockSpec((ctx_len, gw), lambda b, j: (ctx_blk0 + b, 1)),
                  pl.BlockSpec((ctx_len, gw), lambda b, j: (ctx_blk0 + b, 2)),
                  pl.BlockSpec((1,) + bias_tab.shape[1:], lambda b, j: (bias_case(j), 0, 0, 0))],
        out_specs=pl.BlockSpec((qt, gw), lambda b, j: (b * n_q + j, 0)),
        out_shape=jax.ShapeDtypeStruct((total_rows, gw), BF16),
        compiler_params=_cparams(2),
        name="na_attn",
    )(qkv, qkv, qkv, qkv, qkv, bias_tab)


def _df_kernel(q_ref, k_ref, v_ref, kc_ref, vc_ref, lam_ref, subln_ref, o_ref, acc_ref, *, lam_init):
    lam = _lambda(lam_ref, lam_init)
    acc = _diff_heads(q_ref[...], [kc_ref[...], k_ref[...]], [vc_ref[...], v_ref[...]], lam,
                      subln_ref[...], lam_init, acc_ref)
    o_ref[...] = acc.astype(BF16)


def _df_call(qkv, lam_vecs, subln, lam_init, n_batch, seq, total_rows, ctx_len):
    gw = GROUP_WIDTH
    qt = DIFF_Q_TILE
    n_q = seq // qt
    ctx_blk0 = n_batch * seq // ctx_len
    return pl.pallas_call(
        functools.partial(_df_kernel, lam_init=lam_init),
        grid=(n_batch, n_q),
        in_specs=[pl.BlockSpec((qt, gw), lambda b, j: (b * n_q + j, 3)),
                  pl.BlockSpec((seq, gw), lambda b, j: (b, 4)),
                  pl.BlockSpec((seq, gw), lambda b, j: (b, 5)),
                  pl.BlockSpec((ctx_len, gw), lambda b, j: (ctx_blk0 + b, 4)),
                  pl.BlockSpec((ctx_len, gw), lambda b, j: (ctx_blk0 + b, 5)),
                  _full(lam_vecs.shape),
                  _full(subln.shape)],
        out_specs=pl.BlockSpec((qt, gw), lambda b, j: (b * n_q + j, 0)),
        out_shape=jax.ShapeDtypeStruct((total_rows, gw), BF16),
        scratch_shapes=[pltpu.VMEM((qt, gw), F32)],
        compiler_params=_cparams(2),
        name="diff_attn",
    )(qkv, qkv, qkv, qkv, qkv, lam_vecs, subln)


def _ctx_attn_kernel(qkv_ref, lam_ref, subln_ref, ya_in, yd_in, ya_ref, yd_ref, acc_ref, *, lam_init):
    del ya_in, yd_in
    gw = GROUP_WIDTH
    col = lambda g: qkv_ref[:, g * gw:(g + 1) * gw]
    ya = _na_heads(col(0), [col(1)], lambda h: [None], [col(2)])
    ya_ref[...] = ya.astype(BF16)
    lam = _lambda(lam_ref, lam_init)
    yd = _diff_heads(col(3), [col(4)], [col(5)], lam, subln_ref[...], lam_init, acc_ref)
    yd_ref[...] = yd.astype(BF16)


def _ctx_attn_call(qkv, lam_vecs, subln, lam_init, ya, yd, n_batch, seq, ctx_len):
    gw = GROUP_WIDTH
    ctx_blk0 = n_batch * seq // ctx_len
    any_spec = pl.BlockSpec(memory_space=pl.ANY)
    out_spec = pl.BlockSpec((ctx_len, gw), lambda b: (ctx_blk0 + b, 0))
    return pl.pallas_call(
        functools.partial(_ctx_attn_kernel, lam_init=lam_init),
        grid=(n_batch,),
        in_specs=[pl.BlockSpec((ctx_len, 6 * gw), lambda b: (ctx_blk0 + b, 0)),
                  _full(lam_vecs.shape), _full(subln.shape), any_spec, any_spec],
        out_specs=[out_spec, out_spec],
        out_shape=[jax.ShapeDtypeStruct(ya.shape, BF16), jax.ShapeDtypeStruct(yd.shape, BF16)],
        input_output_aliases={3: 0, 4: 1},
        scratch_shapes=[pltpu.VMEM((ctx_len, gw), F32)],
        compiler_params=_cparams(1),
        name="ctx_attn",
    )(qkv, lam_vecs, subln, ya, yd)


def _pool_kernel(p_ref, w_ref, scale_ref, *rest, seq, aliased):
    o_ref, pad_ref = rest[-2], rest[-1]
    del aliased
    gw = GROUP_WIDTH
    halo = POOL_HALO
    pad_ref[0:halo, :] = jnp.zeros((halo, gw), F32)
    pad_ref[halo + seq:, :] = jnp.zeros((halo, gw), F32)
    pad_ref[halo:halo + seq, :] = p_ref[...]
    chunk = min(seq, 256)
    lane_group = lax.broadcasted_iota(jnp.int32, (1, gw), 1) // (gw // len(POOL_WINDOWS))
    for c0 in range(0, seq, chunk):
        def at(off):
            return pad_ref[halo + c0 + off:halo + c0 + off + chunk, :]
        pos = c0 + lax.broadcasted_iota(jnp.int32, (chunk, 1), 0)
        x = at(0)
        run = x
        mean = None
        lo_done, hi_done = 0, 0
        for g, win in enumerate(POOL_WINDOWS):
            half = win // 2
            for off in range(-half, -lo_done):
                run = run + at(off)
            for off in range(hi_done + 1, half):
                run = run + at(off)
            lo_done, hi_done = half, half - 1
            cnt = (jnp.minimum(pos + half, seq) - jnp.maximum(pos - half, 0)).astype(F32)
            m = run / cnt
            mean = m if mean is None else jnp.where(lane_group == g, m, mean)
        y = _dot((mean - x).astype(BF16), w_ref[...]) * scale_ref[...]
        o_ref[c0:c0 + chunk, :] = y.astype(BF16)


def _pool_call(pool_in, w_bd, scale, n_seq, seq, row_blk0, total_rows, prev=None):
    gw = GROUP_WIDTH
    in_specs = [pl.BlockSpec((seq, gw), lambda b: (row_blk0 + b, 0)), _full((gw, gw)), _full((1, gw))]
    args = [pool_in, w_bd, scale]
    aliases = {}
    if prev is not None:
        in_specs.append(pl.BlockSpec(memory_space=pl.ANY))
        args.append(prev)
        aliases = {3: 0}
    return pl.pallas_call(
        functools.partial(_pool_kernel, seq=seq, aliased=prev is not None),
        grid=(n_seq,),
        in_specs=in_specs,
        out_specs=pl.BlockSpec((seq, gw), lambda b: (row_blk0 + b, 0)),
        out_shape=jax.ShapeDtypeStruct((total_rows, gw), BF16),
        scratch_shapes=[pltpu.VMEM((seq + 2 * POOL_HALO, gw), F32)],
        input_output_aliases=aliases,
        compiler_params=_cparams(1),
        name="pool_mix",
    )(*args)


def _fft_kernel(t_ref, cl_ref, sl_ref, cc_ref, sc_ref, w_ref, *rest, norm):
    o_ref, a_ref, b_ref = rest[-3], rest[-2], rest[-1]

    @pl.when(pl.program_id(1) == 0)
    def _():
        t = t_ref[...]
        a_ref[...] = _dot(t, cc_ref[...]).astype(BF16)
        b_ref[...] = _dot(t, sc_ref[...]).astype(BF16)

    f = (_dot(cl_ref[...], a_ref[...]) - _dot(sl_ref[...], b_ref[...])) * norm
    o_ref[...] = _dot(f.astype(BF16), w_ref[...]).astype(BF16)


def _fft_call(fft_in, cl, sl, cc_bd, sc_bd, w_bd, n_seq, seq, row_blk0, total_rows, prev=None):
    gw = GROUP_WIDTH
    tk = min(FFT_TILE, seq)
    n_k = seq // tk
    in_specs = [pl.BlockSpec((seq, gw), lambda b, k: (row_blk0 + b, 0)),
                pl.BlockSpec((tk, seq), lambda b, k: (k, 0)),
                pl.BlockSpec((tk, seq), lambda b, k: (k, 0)),
                _full((gw, gw)), _full((gw, gw)), _full((gw, gw))]
    args = [fft_in, cl, sl, cc_bd, sc_bd, w_bd]
    aliases = {}
    if prev is not None:
        in_specs.append(pl.BlockSpec(memory_space=pl.ANY))
        args.append(prev)
        aliases = {6: 0}
    norm = 1.0 / math.sqrt(seq * (gw // 4))
    return pl.pallas_call(
        functools.partial(_fft_kernel, norm=norm),
        grid=(n_seq, n_k),
        in_specs=in_specs,
        out_specs=pl.BlockSpec((tk, gw), lambda b, k: ((row_blk0 + b) * n_k + k, 0)),
        out_shape=jax.ShapeDtypeStruct((total_rows, gw), BF16),
        scratch_shapes=[pltpu.VMEM((seq, gw), BF16), pltpu.VMEM((seq, gw), BF16)],
        input_output_aliases=aliases,
        compiler_params=_cparams(2),
        name="fourier_mix",
    )(*args)


def _out_kernel(ya_ref, yd_ref, yb_ref, yf_ref, w_ref, xa_ref, xb_ref, g1_ref, sh2_ref, sc2_ref, gn2_ref,
                rw_ref, rb_ref, x1_ref, h2_ref, lg_ref, *, n_lat_tiles):
    gw = GROUP_WIDTH
    x = jnp.where(pl.program_id(0) < n_lat_tiles, xa_ref[...], xb_ref[...])
    acc = _dot(ya_ref[...], w_ref[0:gw, :])
    acc = acc + _dot(yd_ref[...], w_ref[gw:2 * gw, :])
    acc = acc + _dot(yb_ref[...], w_ref[2 * gw:3 * gw, :])
    acc = acc + _dot(yf_ref[...], w_ref[3 * gw:4 * gw, :])
    x1 = x + g1_ref[0] * acc
    x1_ref[...] = x1
    ms = jnp.mean(x1 * x1, axis=-1, keepdims=True)
    h2 = x1 * lax.rsqrt(ms + NORM_EPS) * gn2_ref[...] * (1.0 + sc2_ref[0]) + sh2_ref[0]
    h_hi = h2.astype(BF16)
    h_lo = (h2 - h_hi.astype(F32)).astype(BF16)
    by_hi = _dot(h_hi, rw_ref[...])
    by_lo = _dot(h_lo, rw_ref[:, :LANES])
    lg_ref[...] = by_hi[:, :LANES] + by_hi[:, LANES:] + by_lo + rb_ref[...]
    tm = x1.shape[0]
    for s in range(SLABS):
        h2_ref[pl.ds(s, tm, stride=SLABS), :] = h2[:, s * LANES:(s + 1) * LANES]


def _out_call(ya, yd, yb, yf, w_out_bf, xa, xb, ctx_blk0, mod3, gn2, rw_pad, rb_pad, rows, n_lat_rows, seq):
    d = D_MODEL
    gw = GROUP_WIDTH
    tm = ROW_TILE
    n_lat_tiles = n_lat_rows // tm
    tiles_per_seq = seq // tm
    n_batch = n_lat_rows // seq

    def mod_row(i):
        return jnp.minimum(i // tiles_per_seq, n_batch)

    mix_spec = pl.BlockSpec((tm, gw), lambda i: (i, 0))
    mod_spec = lambda chunk: pl.BlockSpec((1, 1, d), lambda i: (mod_row(i), 0, chunk))
    return pl.pallas_call(
        functools.partial(_out_kernel, n_lat_tiles=n_lat_tiles),
        grid=(rows // tm,),
        in_specs=[mix_spec, mix_spec, mix_spec, mix_spec, _full((4 * gw, d))]
                 + _stream_specs(tm, d, n_lat_tiles, ctx_blk0) + [
                  mod_spec(2), mod_spec(3), mod_spec(4), _full((1, d)),
                  _full(rw_pad.shape), _full(rb_pad.shape)],
        out_specs=[pl.BlockSpec((tm, d), lambda i: (i, 0)),
                   pl.BlockSpec((tm * SLABS, LANES), lambda i: (i, 0)),
                   pl.BlockSpec((tm, LANES), lambda i: (i, 0))],
        out_shape=[jax.ShapeDtypeStruct((rows, d), F32),
                   jax.ShapeDtypeStruct((rows * SLABS, LANES), F32),
                   jax.ShapeDtypeStruct((rows, LANES), F32)],
        compiler_params=_cparams(1),
        name="out_proj",
    )(ya, yd, yb, yf, w_out_bf, xa, xb, mod3, mod3, mod3, gn2, rw_pad, rb_pad)


def _route_kernel(lg_ref, idx_ref, gate_ref, rank_ref, cnt_ref, carry_ref):
    i = pl.program_id(0)

    @pl.when(i == 0)
    def _():
        carry_ref[...] = jnp.zeros_like(carry_ref)

    lg = lg_ref[...]
    tm = lg.shape[0]
    lane = lax.broadcasted_iota(jnp.int32, (tm, LANES), 1)
    vals, idxs = [], []
    onehot = jnp.zeros((tm, LANES), F32)
    for _ in range(TOP_K):
        m = lg.max(axis=-1, keepdims=True)
        idx = jnp.min(jnp.where(lg == m, lane, LANES), axis=-1, keepdims=True)
        sel = lane == idx
        onehot = onehot + sel.astype(F32)
        lg = jnp.where(sel, -jnp.inf, lg)
        vals.append(m)
        idxs.append(idx)
    exps = [jnp.exp(v - vals[0]) for v in vals]
    denom = exps[0] + exps[1] + exps[2] + exps[3]
    r_i = lax.broadcasted_iota(jnp.int32, (tm, tm), 0)
    c_i = lax.broadcasted_iota(jnp.int32, (tm, tm), 1)
    tri = (c_i < r_i).astype(BF16)
    before = _dot(tri, onehot.astype(BF16)) + carry_ref[...]
    col4 = lax.broadcasted_iota(jnp.int32, (tm, TOP_K), 1)
    idx_o = jnp.zeros((tm, TOP_K), jnp.int32)
    gate_o = jnp.zeros((tm, TOP_K), F32)
    rank_o = jnp.zeros((tm, TOP_K), F32)
    for k in range(TOP_K):
        rk = jnp.sum(jnp.where(lane == idxs[k], before, 0.0), axis=-1, keepdims=True)
        idx_o = jnp.where(col4 == k, idxs[k], idx_o)
        gate_o = jnp.where(col4 == k, exps[k] / denom, gate_o)
        rank_o = jnp.where(col4 == k, rk, rank_o)
    idx_ref[...] = idx_o
    gate_ref[...] = gate_o
    rank_ref[...] = rank_o.astype(jnp.int32)
    carry_ref[...] = carry_ref[...] + jnp.sum(onehot, axis=0, keepdims=True)
    cnt_ref[...] = carry_ref[...].astype(jnp.int32)


def _route_call(logits):
    n = logits.shape[0]
    tm = ROW_TILE
    k_spec = pl.BlockSpec((tm, TOP_K), lambda i: (i, 0))
    return pl.pallas_call(
        _route_kernel,
        grid=(n // tm,),
        in_specs=[pl.BlockSpec((tm, LANES), lambda i: (i, 0))],
        out_specs=[k_spec, k_spec, k_spec, _full((1, LANES))],
        out_shape=[jax.ShapeDtypeStruct((n, TOP_K), jnp.int32),
                   jax.ShapeDtypeStruct((n, TOP_K), F32),
                   jax.ShapeDtypeStruct((n, TOP_K), jnp.int32),
                   jax.ShapeDtypeStruct((1, LANES), jnp.int32)],
        scratch_shapes=[pltpu.VMEM((1, LANES), F32)],
        compiler_params=_cparams(1),
        name="route",
    )(logits)


def _row_copy(src_ref, src_row, dst_ref, dst_row, sem):
    return pltpu.make_async_copy(
        src_ref.at[pl.ds(pl.multiple_of(src_row * SLABS, SLABS), SLABS), :],
        dst_ref.at[pl.ds(pl.multiple_of(dst_row * SLABS, SLABS), SLABS), :], sem)


def _pad_copy(zero_ref, hs_ref, start_row, n_rows, sem):
    return pltpu.make_async_copy(
        zero_ref.at[pl.ds(0, n_rows * SLABS), :],
        hs_ref.at[pl.ds(pl.multiple_of(start_row * SLABS, SLABS), n_rows * SLABS), :], sem)


def _dispatch_kernel(fill_ref, pad_ref, dest_hbm, h2_ref, hs_ref, dest_smem, zero_ref, sem, isem):
    i = pl.program_id(0)
    n_assign = TOKEN_TILE * TOP_K
    idx_copy = pltpu.make_async_copy(dest_hbm.at[pl.ds(i * n_assign, n_assign)], dest_smem, isem)
    idx_copy.start()

    @pl.when(i == 0)
    def _():
        zero_ref[...] = jnp.zeros_like(zero_ref)
        bits = [1 << b for b in reversed(range(int(math.log2(EXPERT_BLOCK))))]
        for phase in ("start", "wait"):
            def fill(e, carry, phase=phase):
                pos = fill_ref[e]
                pad = pad_ref[e]
                for bit in bits:
                    @pl.when((pad & bit) != 0)
                    def _(pos=pos, bit=bit):
                        cp = _pad_copy(zero_ref, hs_ref, pos, bit, sem)
                        cp.start() if phase == "start" else cp.wait()
                    pos = pos + (pad & bit)
                return carry

            lax.fori_loop(0, N_EXPERTS, fill, 0)

    idx_copy.wait()

    def start(t, carry):
        for k in range(TOP_K):
            _row_copy(h2_ref, t, hs_ref, dest_smem[t * TOP_K + k], sem).start(priority=k % 2)
        return carry

    lax.fori_loop(0, TOKEN_TILE, start, 0, unroll=ISSUE_UNROLL)
    for _ in range(TOP_K):
        pltpu.make_async_copy(h2_ref, hs_ref.at[pl.ds(0, TOKEN_TILE * SLABS), :], sem).wait()


def _dispatch_call(fill_start, pad_len, dest_flat, h2_slabs, n_slots):
    n = h2_slabs.shape[0] // SLABS
    tt = TOKEN_TILE
    grid_spec = pltpu.PrefetchScalarGridSpec(
        num_scalar_prefetch=2,
        grid=(n // tt,),
        in_specs=[pl.BlockSpec(memory_space=pl.ANY),
                  pl.BlockSpec((tt * SLABS, LANES), lambda i, *_: (i, 0))],
        out_specs=pl.BlockSpec(memory_space=pl.ANY),
        scratch_shapes=[pltpu.SMEM((tt * TOP_K,), jnp.int32),
                        pltpu.VMEM((EXPERT_BLOCK // 2 * SLABS, LANES), F32),
                        pltpu.SemaphoreType.DMA, pltpu.SemaphoreType.DMA],
    )
    return pl.pallas_call(
        _dispatch_kernel,
        grid_spec=grid_spec,
        out_shape=jax.ShapeDtypeStruct((n_slots * SLABS, LANES), F32),
        compiler_params=_cparams(1),
        name="dispatch",
    )(fill_start, pad_len, dest_flat, h2_slabs)


GLU_BLOCK = 2 * LANES


def _expert_kernel(be_ref, first_ref, nv_ref, hs_ref, w1_ref, b1_ref, w2_ref, b2_ref, perm_ref,
                   ys_ref, h_ref, act_ref, w1p_ref, w2p_ref):
    del be_ref
    i = pl.program_id(0)
    n_glu = 2 * D_FF // GLU_BLOCK

    @pl.when((first_ref[i] != 0) & (i < nv_ref[0]))
    def _():
        for b in range(n_glu):
            cols = slice(b * GLU_BLOCK, (b + 1) * GLU_BLOCK)
            w1p_ref[:, cols] = _dot(w1_ref[0, :, cols].astype(BF16), perm_ref[...]).astype(BF16)
        w2p_ref[...] = w2_ref[0].astype(BF16)

    @pl.when(i < nv_ref[0])
    def _():
        blk = EXPERT_BLOCK
        for s in range(SLABS):
            h_ref[:, s * LANES:(s + 1) * LANES] = hs_ref[pl.ds(s, blk, stride=SLABS), :].astype(BF16)
        h = h_ref[...]
        for b in range(n_glu):
            cols = slice(b * GLU_BLOCK, (b + 1) * GLU_BLOCK)
            u = _dot(h, w1p_ref[:, cols]) + b1_ref[0, :, cols]
            glu = jnp.minimum(u[:, :LANES], SWIGLU_LIMIT)
            lin = jnp.clip(u[:, LANES:], -SWIGLU_LIMIT, SWIGLU_LIMIT)
            act = glu * jax.nn.sigmoid(SWIGLU_ALPHA * glu) * (lin + 1.0)
            act_ref[:, b * LANES:(b + 1) * LANES] = act.astype(BF16)
        y = _dot(act_ref[...], w2p_ref[...]) + b2_ref[0]
        for s in range(SLABS):
            ys_ref[pl.ds(s, blk, stride=SLABS), :] = y[:, s * LANES:(s + 1) * LANES]


def _expert_call(block_e, first, n_valid, hs, w1, b1p, w2, b2, perm, layer, n_blocks):
    blk = EXPERT_BLOCK
    d = D_MODEL

    def row_blk(i, be, first, nv):
        return (jnp.minimum(i, nv[0] - 1), 0)

    def expert_blk(i, be, first, nv):
        return (be[i], 0, 0)

    def weight_blk(i, be, first, nv):
        return (layer, be[i], 0, 0)

    grid_spec = pltpu.PrefetchScalarGridSpec(
        num_scalar_prefetch=3,
        grid=(n_blocks,),
        in_specs=[pl.BlockSpec((blk * SLABS, LANES), row_blk),
                  pl.BlockSpec((None, 1, d, 2 * D_FF), weight_blk),
                  pl.BlockSpec((1, 1, 2 * D_FF), expert_blk),
                  pl.BlockSpec((None, 1, D_FF, d), weight_blk),
                  pl.BlockSpec((1, 1, d), expert_blk),
                  pl.BlockSpec((GLU_BLOCK, GLU_BLOCK), lambda i, *_: (0, 0))],
        out_specs=pl.BlockSpec((blk * SLABS, LANES), row_blk),
        scratch_shapes=[pltpu.VMEM((blk, d), BF16), pltpu.VMEM((blk, D_FF), BF16),
                        pltpu.VMEM((d, 2 * D_FF), BF16), pltpu.VMEM((D_FF, d), BF16)],
    )
    return pl.pallas_call(
        _expert_kernel,
        grid_spec=grid_spec,
        out_shape=jax.ShapeDtypeStruct(hs.shape, F32),
        compiler_params=_cparams(1, EXPERT_VMEM_LIMIT),
        name="expert_ffn",
    )(block_e, first, n_valid, hs, w1, b1p, w2, b2, perm)


def _combine_kernel(dest_hbm, ys_hbm, gate_ref, x_ref, g2_ref, o_ref,
                    idx0, idx1, buf0, buf1, sems, isems):
    i = pl.program_id(0)
    n_tiles = pl.num_programs(0)
    tt = TOKEN_TILE
    n_assign = tt * TOP_K
    idx_refs, buf_refs = (idx0, idx1), (buf0, buf1)

    def idx_copy(tile, s):
        return pltpu.make_async_copy(dest_hbm.at[pl.ds(tile * n_assign, n_assign)], idx_refs[s],
                                     isems.at[s])

    def issue(s):
        def start(t, carry):
            for k in range(TOP_K):
                _row_copy(ys_hbm, idx_refs[s][t * TOP_K + k], buf_refs[s], k * tt + t,
                          sems.at[s]).start(priority=k % 2)
            return carry
        lax.fori_loop(0, tt, start, 0, unroll=ISSUE_UNROLL)

    @pl.when(i == 0)
    def _():
        idx_copy(0, 0).start()
        idx_copy(0, 0).wait()
        issue(0)

        @pl.when(n_tiles > 1)
        def _():
            idx_copy(1, 1).start()

    for s in (0, 1):
        @pl.when(i % 2 == s)
        def _(s=s):
            @pl.when(i + 1 < n_tiles)
            def _():
                idx_copy(i + 1, 1 - s).wait()
                issue(1 - s)

            pltpu.make_async_copy(ys_hbm.at[pl.ds(0, n_assign * SLABS), :], buf_refs[s], sems.at[s]).wait()

            @pl.when(i + 2 < n_tiles)
            def _():
                idx_copy(i + 2, s).start()

            gate = gate_ref[...]
            g2 = g2_ref[0]
            for sl in range(SLABS):
                cols = slice(sl * LANES, (sl + 1) * LANES)
                y = jnp.zeros((tt, LANES), F32)
                for k in range(TOP_K):
                    y = y + gate[:, k:k + 1] * buf_refs[s][pl.ds(k * tt * SLABS + sl, tt, stride=SLABS), :]
                o_ref[:, cols] = x_ref[:, cols] + g2[:, cols] * y


def _combine_call(dest_flat, ys, gate, x1, mod3, n_lat_rows, seq):
    n, d = x1.shape
    tt = TOKEN_TILE
    tiles_per_seq = seq // tt
    n_batch = n_lat_rows // seq

    def mod_row(i):
        return jnp.minimum(i // tiles_per_seq, n_batch)

    return pl.pallas_call(
        _combine_kernel,
        grid=(n // tt,),
        in_specs=[pl.BlockSpec(memory_space=pl.ANY), pl.BlockSpec(memory_space=pl.ANY),
                  pl.BlockSpec((tt, TOP_K), lambda i: (i, 0)),
                  pl.BlockSpec((tt, d), lambda i: (i, 0)),
                  pl.BlockSpec((1, 1, d), lambda i: (mod_row(i), 0, 5))],
        out_specs=pl.BlockSpec((tt, d), lambda i: (i, 0)),
        out_shape=jax.ShapeDtypeStruct((n, d), F32),
        scratch_shapes=[pltpu.SMEM((tt * TOP_K,), jnp.int32), pltpu.SMEM((tt * TOP_K,), jnp.int32),
                        pltpu.VMEM((TOP_K * tt * SLABS, LANES), F32),
                        pltpu.VMEM((TOP_K * tt * SLABS, LANES), F32),
                        pltpu.SemaphoreType.DMA((2,)), pltpu.SemaphoreType.DMA((2,))],
        compiler_params=_cparams(1),
        name="combine",
    )(dest_flat, ys, gate, x1, mod3)


def _block_diag(blocks):
    g, a, b = blocks.shape
    eye = jnp.eye(g, dtype=blocks.dtype)
    return (eye[:, None, :, None] * blocks[:, :, None, :]).reshape(g * a, g * b)


def _rope_tables(seq, extra_rows):
    half = DIFF_QK_DIM // 2
    inv = ROPE_THETA ** (-jnp.arange(0, half, 2, dtype=F32) / half)
    pos = jnp.arange(seq)
    c = jnp.arange(GROUP_WIDTH)
    dd = c % DIFF_QK_DIM
    axis_pos = jnp.where((dd < half)[None, :], (pos // GRID_W)[:, None], (pos % GRID_W)[:, None])
    ang = axis_pos.astype(F32) * inv[dd % (half // 2)][None, :]
    sign = jnp.where((dd % half) < half // 2, -1.0, 1.0).astype(F32)
    cos_t = jnp.concatenate([jnp.cos(ang), jnp.ones((extra_rows, GROUP_WIDTH), F32)], axis=0)
    sin_t = jnp.concatenate([jnp.sin(ang) * sign[None, :], jnp.zeros((extra_rows, GROUP_WIDTH), F32)], axis=0)
    return cos_t, sin_t


def _dft_tables(n, dtype):
    k = np.arange(n, dtype=np.int64)
    ang = ((k[:, None] * k[None, :]) % n).astype(np.float64) * (2.0 * math.pi / n)
    return jnp.asarray(np.cos(ang), dtype), jnp.asarray(np.sin(ang), dtype)


def _na_bias_tables(rpb, grid_rows):
    w = GRID_W
    col = np.arange(w)
    col_start = np.clip(col - NA_COLS // 2, 0, w - NA_COLS)
    in_win = (col[None, :] >= col_start[:, None]) & (col[None, :] < col_start[:, None] + NA_COLS)
    rel_c = np.clip(col[None, :] - col[:, None], 1 - NA_COLS, NA_COLS - 1) + NA_COLS - 1
    n_tiles = grid_rows // NA_Q_ROWS
    cases = (0, 1, n_tiles - 1)
    rel_r = np.zeros((3, NA_Q_ROWS, NA_KEY_ROWS), np.int32)
    valid = np.zeros((3, NA_Q_ROWS, NA_KEY_ROWS), bool)
    for ci, tile in enumerate(cases):
        r0 = tile * NA_Q_ROWS
        k0 = int(np.clip(r0 - NA_ROWS // 2, 0, grid_rows - NA_KEY_ROWS))
        for j in range(NA_Q_ROWS):
            start = int(np.clip(r0 + j - NA_ROWS // 2, 0, grid_rows - NA_ROWS))
            for i in range(NA_KEY_ROWS):
                kr = k0 + i
                valid[ci, j, i] = start <= kr < start + NA_ROWS
                rel_r[ci, j, i] = np.clip(kr - (r0 + j) + NA_ROWS - 1, 0, 2 * NA_ROWS - 2)
    del rel_c
    edge = w - NA_COLS
    ext = jnp.pad(rpb.astype(F32), ((0, 0), (0, 0), (edge, edge)), mode='edge')
    t = jnp.stack([ext[:, :, w - 1 - q:2 * w - 1 - q] for q in range(w)], axis=2)
    t = jnp.where(in_win[None, None], t, MASK_VALUE)
    masked = jnp.full((rpb.shape[0], w, w), MASK_VALUE, F32)
    b = jnp.stack([jnp.stack([jnp.stack([t[:, rel_r[c, j, i]] if valid[c, j, i] else masked
                                         for i in range(NA_KEY_ROWS)], axis=0)
                              for j in range(NA_Q_ROWS)], axis=0)
                   for c in range(3)], axis=0)
    b = b.transpose(0, 3, 1, 4, 2, 5)
    return b.reshape(3, rpb.shape[0], NA_Q_ROWS * w, NA_KEY_ROWS * w)


def _moe(h2_slabs, logits, x1, mod3, w1, b1p, w2, b2, perm, layer, n_lat_rows, seq):
    n = x1.shape[0]
    blk = EXPERT_BLOCK
    idx, gate, rank, counts = _route_call(logits)
    counts = counts[0, :N_EXPERTS]
    padded = (counts + blk - 1) // blk * blk
    padded_end = jnp.cumsum(padded)
    padded_start = padded_end - padded
    dest = (jnp.take(padded_start, idx.reshape(n * TOP_K)) + rank.reshape(n * TOP_K)).astype(jnp.int32)
    n_blocks = n * TOP_K // blk + N_EXPERTS
    block_row0 = jnp.arange(n_blocks, dtype=jnp.int32) * blk
    block_e = jnp.minimum(jnp.sum((padded_end[None, :] <= block_row0[:, None]).astype(jnp.int32), axis=1),
                          N_EXPERTS - 1).astype(jnp.int32)
    first = jnp.concatenate([jnp.ones((1,), jnp.int32),
                             (block_e[1:] != block_e[:-1]).astype(jnp.int32)])
    n_valid = (padded_end[-1:] // blk).astype(jnp.int32)
    fill_start = (padded_start + counts).astype(jnp.int32)
    pad_len = (padded - counts).astype(jnp.int32)
    hs = _dispatch_call(fill_start, pad_len, dest, h2_slabs, n_blocks * blk)
    ys = _expert_call(block_e, first, n_valid, hs, w1, b1p, w2, b2, perm, layer, n_blocks)
    return _combine_call(dest, ys, gate, x1, mod3, n_lat_rows, seq)


def kernel(x, c, ctx, c_ctx, w_ada, b_ada, g_norm1, g_norm2, w_in, w_out, na_q_gain, na_k_gain, na_rpb, diff_q_gain, diff_k_gain, diff_lambda_q1, diff_lambda_k1, diff_lambda_q2, diff_lambda_k2, diff_subln, pool_w, pool_scale, fft_w, router_w, router_b, moe_w1, moe_b1, moe_w2, moe_b2):
    n_batch, seq, d = x.shape
    ctx_len = ctx.shape[1]
    depth = w_ada.shape[0]
    gw = GROUP_WIDTH
    assert d == D_MODEL and seq % ROW_TILE == 0 and seq % ctx_len == 0 and ctx_len % TOKEN_TILE == 0
    assert (n_batch * ctx_len) % ROW_TILE == 0 and (seq // GRID_W) >= NA_KEY_ROWS
    n_lat = n_batch * seq
    n_ctx = n_batch * ctx_len

    xa, xb, ctx_blk0 = x.reshape(n_lat, d), ctx.reshape(n_ctx, d), 0
    mod_rows = -(-(n_batch + 1) // SUBLANES) * SUBLANES
    cvec = jnp.zeros((mod_rows, d), F32).at[:n_batch].set(c).at[n_batch].set(c_ctx)
    mod = _ada_call(cvec, w_ada, b_ada)

    cos_t, sin_t = _rope_tables(seq, ROW_TILE)
    ones = lambda w: _block_diag(jnp.ones((gw // w, w, w), BF16))
    g64, g32 = ones(HEAD_DIM), ones(DIFF_QK_DIM)
    cl_lat, sl_lat = _dft_tables(seq, BF16)
    cl_ctx, sl_ctx = _dft_tables(ctx_len, BF16)
    cc, sc = _dft_tables(gw // 4, F32)
    n_grp = fft_w.shape[1]
    cc_bd = _block_diag(jnp.broadcast_to(cc, (n_grp,) + cc.shape)).astype(BF16)
    sc_bd = _block_diag(jnp.broadcast_to(sc, (n_grp,) + sc.shape)).astype(BF16)
    src = np.arange(GLU_BLOCK)
    dst = np.where(src % 2 == 0, src // 2, LANES + src // 2)
    perm_np = np.zeros((GLU_BLOCK, GLU_BLOCK), np.float32)
    perm_np[src, dst] = 1.0
    glu_perm = jnp.asarray(perm_np, BF16)

    for l in range(depth):
        ctx_out = l < depth - 1
        lam_init = 0.8 - 0.6 * math.exp(-0.3 * l)
        mod3 = mod[l].reshape(mod_rows, 1, 6 * d)
        tile = lambda v: jnp.tile(v.astype(F32), gw // v.shape[0])
        gains = jnp.stack([tile(na_q_gain[l]) * (HEAD_DIM ** -0.5 * LOG2_E), tile(na_k_gain[l]),
                           tile(diff_q_gain[l]) * (DIFF_QK_DIM ** -0.5 * LOG2_E), tile(diff_k_gain[l])]
                          + [jnp.zeros((gw,), F32)] * 4)
        lam_vecs = jnp.stack([diff_lambda_q1[l], diff_lambda_k1[l], diff_lambda_q2[l], diff_lambda_k2[l]]
                             + [jnp.zeros_like(diff_lambda_q1[l])] * 4).astype(F32)
        subln = tile(diff_subln[l]).reshape(1, gw)
        bias_tab = _na_bias_tables(na_rpb[l] * LOG2_E, seq // GRID_W)
        pool_bd = _block_diag(pool_w[l]).astype(BF16)
        fftw_bd = _block_diag(fft_w[l]).astype(BF16)
        rw_f32 = jnp.zeros((d, LANES), F32).at[:, :N_EXPERTS].set(router_w[l])
        rw_hi = rw_f32.astype(BF16)
        rw_pad = jnp.concatenate([rw_hi, (rw_f32 - rw_hi.astype(F32)).astype(BF16)], axis=1)
        rb_pad = jnp.full((1, LANES), -jnp.inf, F32).at[0, :N_EXPERTS].set(router_b[l])
        n_e = moe_w1.shape[1]
        b1p = moe_b1[l].reshape(n_e, 2 * D_FF // GLU_BLOCK, LANES, 2).transpose(0, 1, 3, 2)
        b1p = b1p.reshape(n_e, 1, 2 * D_FF)
        b2 = moe_b2[l].reshape(n_e, 1, d)

        qkv, pool_in, fft_in = _in_call(xa, xb, ctx_blk0, n_lat + n_ctx, mod3, g_norm1[l].reshape(1, d),
                                        w_in[l].astype(BF16), gains, cos_t, sin_t, g64, g32, n_lat, seq)
        rows = n_lat + n_ctx if ctx_out else n_lat
        ya = _na_call(qkv, bias_tab, n_batch, seq, rows, ctx_len)
        yd = _df_call(qkv, lam_vecs, subln, lam_init, n_batch, seq, rows, ctx_len)
        yb = _pool_call(pool_in, pool_bd, pool_scale[l].reshape(1, gw), n_batch, seq, 0, rows)
        yf = _fft_call(fft_in, cl_lat, sl_lat, cc_bd, sc_bd, fftw_bd, n_batch, seq, 0, rows)
        if ctx_out:
            ya, yd = _ctx_attn_call(qkv, lam_vecs, subln, lam_init, ya, yd, n_batch, seq, ctx_len)
            yb = _pool_call(pool_in, pool_bd, pool_scale[l].reshape(1, gw), n_batch, ctx_len,
                            n_lat // ctx_len, rows, prev=yb)
            yf = _fft_call(fft_in, cl_ctx, sl_ctx, cc_bd, sc_bd, fftw_bd, n_batch, ctx_len,
                           n_lat // ctx_len, rows, prev=yf)
        x1, h2_slabs, logits = _out_call(ya, yd, yb, yf, w_out[l].astype(BF16), xa, xb, ctx_blk0, mod3,
                                         g_norm2[l].reshape(1, d), rw_pad, rb_pad, rows, n_lat, seq)
        x2 = _moe(h2_slabs, logits, x1, mod3, moe_w1, b1p, moe_w2, b2, glu_perm, l, n_lat, seq)
        xa, xb, ctx_blk0 = x2, x2, n_lat // ROW_TILE
    return x2[:n_lat].reshape(n_batch, seq, d)
```

```python
import functools
import math

import numpy as np
import jax
import jax.numpy as jnp
from jax import lax
from jax.experimental import pallas as pl
from jax.experimental.pallas import tpu as pltpu

F32 = jnp.float32
BF16 = jnp.bfloat16
HIGHEST = lax.Precision.HIGHEST

D_MODEL = 1024
DEPTH = 2
GRID_W = 64
HEAD_DIM = 64
GROUP_WIDTH = 256
N_HEADS = GROUP_WIDTH // HEAD_DIM
DIFF_QK_DIM = HEAD_DIM // 2
NA_ROWS = 8
NA_COLS = 16
POOL_WINDOWS = (2, 4, 8, 16)
POOL_HALO = max(POOL_WINDOWS) // 2
N_EXPERTS = 32
TOP_K = 4
D_FF = D_MODEL
SWIGLU_ALPHA = 1.702
SWIGLU_LIMIT = 7.0
ROPE_THETA = 10000.0
NORM_EPS = 1e-6
MASK_VALUE = -1e30

LANES = 128
SUBLANES = 8
ROW_TILE = 512
TOKEN_TILE = 256
EXPERT_BLOCK = 512
Q_TILE = 256
DIFF_Q_TILE = 512
LOG2_E = math.log2(math.e)
NA_Q_ROWS = Q_TILE // GRID_W
NA_KEY_ROWS = NA_ROWS + NA_Q_ROWS - 1
FFT_TILE = 512
SLABS = D_MODEL // LANES
ISSUE_UNROLL = 8
RANK_STRIDE = 1 << 17
RANK_MASK = RANK_STRIDE - 1
RANK_SHIFT = 17
VMEM_LIMIT = 48 * 1024 * 1024
EXPERT_VMEM_LIMIT = 56 * 1024 * 1024


def _cparams(n_axes, vmem_limit=VMEM_LIMIT):
    return pltpu.CompilerParams(dimension_semantics=("arbitrary",) * n_axes,
                                vmem_limit_bytes=vmem_limit)


def _dot(a, b):
    return jnp.dot(a, b, preferred_element_type=F32)


def _dot_nt(a, b):
    return lax.dot_general(a, b, (((1,), (1,)), ((), ())), preferred_element_type=F32)


def _full(shape):
    zeros = (0,) * len(shape)
    return pl.BlockSpec(shape, lambda *_: zeros)


def _ada_kernel(c_ref, w_ref, b_ref, o_ref):
    c = c_ref[...]
    act = c * jax.nn.sigmoid(c)
    o_ref[0] = jnp.dot(act, w_ref[0], precision=HIGHEST, preferred_element_type=F32) + b_ref[0]


def _ada_call(cvec, w_ada, b_ada):
    depth, d, n = w_ada.shape
    r = cvec.shape[0]
    tn = 1024
    return pl.pallas_call(
        _ada_kernel,
        grid=(depth, n // tn),
        in_specs=[_full((r, d)),
                  pl.BlockSpec((1, d, tn), lambda l, j: (l, 0, j)),
                  pl.BlockSpec((1, 1, tn), lambda l, j: (l, 0, j))],
        out_specs=pl.BlockSpec((1, r, tn), lambda l, j: (l, 0, j)),
        out_shape=jax.ShapeDtypeStruct((depth, r, n), F32),
        compiler_params=_cparams(2),
        name="ada_mod",
    )(cvec, w_ada, b_ada.reshape(depth, 1, n))


def _seg_mean_sq(p, gmat_ref, width):
    sq = p * p
    hi = sq.astype(BF16)
    lo = (sq - hi.astype(F32)).astype(BF16)
    g = gmat_ref[...]
    return (_dot(hi, g) + _dot(lo, g)) * (1.0 / width)


def _stream_specs(tm, d, n_lat_tiles, ctx_blk0):
    return [pl.BlockSpec((tm, d), lambda i: (jnp.minimum(i, n_lat_tiles - 1), 0)),
            pl.BlockSpec((tm, d), lambda i: (ctx_blk0 + jnp.maximum(i - n_lat_tiles, 0), 0))]


def _in_kernel(xa_ref, xb_ref, *rest, n_lat_tiles):
    x = jnp.where(pl.program_id(0) < n_lat_tiles, xa_ref[...], xb_ref[...])
    _in_body(x, *rest)


def _in_body(x, sh_ref, sc_ref, g_ref, w_ref, gains_ref, cos_ref, sin_ref, g64_ref, g32_ref,
             qkv_ref, pool_ref, fft_ref):
    ms = jnp.mean(x * x, axis=-1, keepdims=True)
    y = x * lax.rsqrt(ms + NORM_EPS) * g_ref[...]
    h = (y * (1.0 + sc_ref[0]) + sh_ref[0]).astype(BF16)
    gw = GROUP_WIDTH

    def proj(g):
        return _dot(h, w_ref[:, g * gw:(g + 1) * gw])

    def put(g, val):
        qkv_ref[:, g * gw:(g + 1) * gw] = val.astype(BF16)

    def normed(p, gmat_ref, width, row):
        return p * lax.rsqrt(_seg_mean_sq(p, gmat_ref, width) + NORM_EPS) * gains_ref[row:row + 1, :]

    lane = lax.broadcasted_iota(jnp.int32, (1, gw), 1)
    first_half = (lane % 16) < 8

    def rope(p):
        rot = jnp.where(first_half, pltpu.roll(p, gw - 8, 1), pltpu.roll(p, 8, 1))
        return p * cos_ref[...] + rot * sin_ref[...]

    put(0, normed(proj(0), g64_ref, HEAD_DIM, 0))
    put(1, normed(proj(1), g64_ref, HEAD_DIM, 1))
    put(2, proj(2))
    put(3, rope(normed(proj(3), g32_ref, DIFF_QK_DIM, 2)))
    put(4, rope(normed(proj(4), g32_ref, DIFF_QK_DIM, 3)))
    put(5, proj(5))
    pool_ref[...] = proj(6)
    fft_ref[...] = proj(7).astype(BF16)


def _in_proj_io(tm, rows, params, n_lat_rows, seq):
    mod3, g1, w_in_bf, gains, cos_t, sin_t, g64, g32 = params
    d = D_MODEL
    gw = GROUP_WIDTH
    n_lat_tiles = n_lat_rows // tm
    tiles_per_seq = seq // tm
    n_batch = n_lat_rows // seq

    def mod_row(i):
        return jnp.minimum(i // tiles_per_seq, n_batch)

    def tab_row(i):
        return jnp.where(i < n_lat_tiles, i % tiles_per_seq, tiles_per_seq)

    full = lambda shape: pl.BlockSpec(shape, lambda i, *_: (0,) * len(shape))
    in_specs = [pl.BlockSpec((1, 1, d), lambda i, *_: (mod_row(i), 0, 0)),
                pl.BlockSpec((1, 1, d), lambda i, *_: (mod_row(i), 0, 1)),
                full((1, d)), full(w_in_bf.shape), full(gains.shape),
                pl.BlockSpec((tm, gw), lambda i, *_: (tab_row(i), 0)),
                pl.BlockSpec((tm, gw), lambda i, *_: (tab_row(i), 0)),
                full((gw, gw)), full((gw, gw))]
    args = [mod3, mod3, g1, w_in_bf, gains, cos_t, sin_t, g64, g32]
    out_specs = [pl.BlockSpec((tm, 6 * gw), lambda i, *_: (i, 0)),
                 pl.BlockSpec((tm, gw), lambda i, *_: (i, 0)),
                 pl.BlockSpec((tm, gw), lambda i, *_: (i, 0))]
    out_shape = [jax.ShapeDtypeStruct((rows, 6 * gw), BF16),
                 jax.ShapeDtypeStruct((rows, gw), F32),
                 jax.ShapeDtypeStruct((rows, gw), BF16)]
    return in_specs, args, out_specs, out_shape


def _in_call(xa, xb, ctx_blk0, rows, params, n_lat_rows, seq):
    tm = ROW_TILE
    n_lat_tiles = n_lat_rows // tm
    in_specs, args, out_specs, out_shape = _in_proj_io(tm, rows, params, n_lat_rows, seq)
    return pl.pallas_call(
        functools.partial(_in_kernel, n_lat_tiles=n_lat_tiles),
        grid=(rows // tm,),
        in_specs=_stream_specs(tm, D_MODEL, n_lat_tiles, ctx_blk0) + in_specs,
        out_specs=out_specs,
        out_shape=out_shape,
        compiler_params=_cparams(1),
        name="in_proj",
    )(xa, xb, *args)


def _lane_mask(width, start, size):
    lane = lax.broadcasted_iota(jnp.int32, (1, width), 1)
    return (lane >= start) & (lane < start + size)


def _softmax_pv(qm, keys, biases, vals, sum_lane=None):
    scores = []
    for k, bias in zip(keys, biases):
        s = _dot_nt(qm, k)
        scores.append(s if bias is None else s + bias)
    m = scores[0].max(axis=-1, keepdims=True)
    for s in scores[1:]:
        m = jnp.maximum(m, s.max(axis=-1, keepdims=True))
    o = None
    l = None
    for s, v in zip(scores, vals):
        e = jnp.exp2(s - m)
        part = _dot(e.astype(BF16), v)
        o = part if o is None else o + part
        if sum_lane is None:
            ls = e.sum(axis=-1, keepdims=True)
            l = ls if l is None else l + ls
    if sum_lane is not None:
        lane = lax.broadcasted_iota(jnp.int32, (1, o.shape[1]), 1)
        l = jnp.sum(jnp.where(lane == sum_lane, o, 0.0), axis=-1, keepdims=True)
    return o, l


def _na_heads(q, keys, bias_fn, vals):
    gw = GROUP_WIDTH
    acc = jnp.zeros((q.shape[0], gw), F32)
    for h in range(N_HEADS):
        mask = _lane_mask(gw, h * HEAD_DIM, HEAD_DIM)
        qm = jnp.where(mask, q, jnp.zeros_like(q))
        o, l = _softmax_pv(qm, keys, bias_fn(h), vals)
        acc = acc + jnp.where(mask, o / l, 0.0)
    return acc


def _lambda(lam_ref, lam_init):
    lv = lam_ref[...]
    d1 = jnp.sum(lv[0:1, :] * lv[1:2, :], axis=-1, keepdims=True)
    d2 = jnp.sum(lv[2:3, :] * lv[3:4, :], axis=-1, keepdims=True)
    return jnp.exp(d1) - jnp.exp(d2) + lam_init


def _diff_heads(q, keys, vals, lam, subln, lam_init, acc_ref):
    gw = GROUP_WIDTH
    lane = lax.broadcasted_iota(jnp.int32, (1, gw), 1)
    nones = [None] * len(keys)
    acc_ref[...] = jnp.zeros_like(acc_ref)

    del nones

    def scores_of(h, part):
        plo = h * HEAD_DIM + part * DIFF_QK_DIM
        qm = jnp.where((lane >= plo) & (lane < plo + DIFF_QK_DIM), q, jnp.zeros_like(q))
        scores = [_dot_nt(qm, k) for k in keys]
        m = scores[0].max(axis=-1, keepdims=True)
        for s in scores[1:]:
            m = jnp.maximum(m, s.max(axis=-1, keepdims=True))
        return scores, m

    def attend(h, scores, m):
        lo = h * HEAD_DIM
        hm = (lane >= lo) & (lane < lo + HEAD_DIM)
        o = None
        for s, v in zip(scores, vals):
            part = _dot(jnp.exp2(s - m).astype(BF16), jnp.where(hm, v, jnp.ones_like(v)))
            o = part if o is None else o + part
        l = jnp.sum(jnp.where(lane == (lo + HEAD_DIM) % gw, o, 0.0), axis=-1, keepdims=True)
        return o / l

    combos = [(h, part) for h in range(N_HEADS) for part in range(2)]
    pending = [scores_of(*combos[0])]
    outs = []
    for c, (h, part) in enumerate(combos):
        if c + 1 < len(combos):
            pending.append(scores_of(*combos[c + 1]))
        outs.append(attend(h, *pending.pop(0)))
        if part == 1:
            lo = h * HEAD_DIM
            hm = (lane >= lo) & (lane < lo + HEAD_DIM)
            a = jnp.where(hm, outs[-2] - lam * outs[-1], 0.0)
            ms = jnp.sum(a * a, axis=-1, keepdims=True) * (1.0 / HEAD_DIM)
            acc_ref[...] += a * lax.rsqrt(ms + NORM_EPS)
    return acc_ref[...] * subln * (1.0 - lam_init)


def _na_kernel(q_ref, k_ref, v_ref, kc_ref, vc_ref, bias_ref, o_ref, *, grid_rows):
    j = pl.program_id(1)
    key_row0 = jnp.clip(j * NA_Q_ROWS - NA_ROWS // 2, 0, grid_rows - NA_KEY_ROWS)
    ks = pl.multiple_of(key_row0 * GRID_W, GRID_W)
    n_win = NA_KEY_ROWS * GRID_W
    kwin = k_ref[pl.ds(ks, n_win), :]
    vwin = v_ref[pl.ds(ks, n_win), :]
    acc = _na_heads(q_ref[...], [kwin, kc_ref[...]], lambda h: [bias_ref[0, h], None],
                    [vwin, vc_ref[...]])
    o_ref[...] = acc.astype(BF16)


def _na_call(qkv, bias_tab, n_batch, seq, total_rows, ctx_len):
    gw = GROUP_WIDTH
    qt = Q_TILE
    n_q = seq // qt
    grid_rows = seq // GRID_W
    ctx_blk0 = n_batch * seq // ctx_len

    def bias_case(j):
        return jnp.where(j == 0, 0, jnp.where(j == n_q - 1, 2, 1))

    return pl.pallas_call(
        functools.partial(_na_kernel, grid_rows=grid_rows),
        grid=(n_batch, n_q),
        in_specs=[pl.BlockSpec((qt, gw), lambda b, j: (b * n_q + j, 0)),
                  pl.BlockSpec((seq, gw), lambda b, j: (b, 1)),
                  pl.BlockSpec((seq, gw), lambda b, j: (b, 2)),
                  pl.BlockSpec((ctx_len, gw), lambda b, j: (ctx_blk0 + b, 1)),
                  pl.BlockSpec((ctx_len, gw), lambda b, j: (ctx_blk0 + b, 2)),
                  pl.BlockSpec((1,) + bias_tab.shape[1:], lambda b, j: (bias_case(j), 0, 0, 0))],
        out_specs=pl.BlockSpec((qt, gw), lambda b, j: (b * n_q + j, 0)),
        out_shape=jax.ShapeDtypeStruct((total_rows, gw), BF16),
        compiler_params=_cparams(2),
        name="na_attn",
    )(qkv, qkv, qkv, qkv, qkv, bias_tab)


def _df_kernel(q_ref, k_ref, v_ref, kc_ref, vc_ref, lam_ref, subln_ref, o_ref, acc_ref, *, lam_init):
    lam = _lambda(lam_ref, lam_init)
    acc = _diff_heads(q_ref[...], [kc_ref[...], k_ref[...]], [vc_ref[...], v_ref[...]], lam,
                      subln_ref[...], lam_init, acc_ref)
    o_ref[...] = acc.astype(BF16)


def _df_call(qkv, lam_vecs, subln, lam_init, n_batch, seq, total_rows, ctx_len):
    gw = GROUP_WIDTH
    qt = DIFF_Q_TILE
    n_q = seq // qt
    ctx_blk0 = n_batch * seq // ctx_len
    return pl.pallas_call(
        functools.partial(_df_kernel, lam_init=lam_init),
        grid=(n_batch, n_q),
        in_specs=[pl.BlockSpec((qt, gw), lambda b, j: (b * n_q + j, 3)),
                  pl.BlockSpec((seq, gw), lambda b, j: (b, 4)),
                  pl.BlockSpec((seq, gw), lambda b, j: (b, 5)),
                  pl.BlockSpec((ctx_len, gw), lambda b, j: (ctx_blk0 + b, 4)),
                  pl.BlockSpec((ctx_len, gw), lambda b, j: (ctx_blk0 + b, 5)),
                  _full(lam_vecs.shape),
                  _full(subln.shape)],
        out_specs=pl.BlockSpec((qt, gw), lambda b, j: (b * n_q + j, 0)),
        out_shape=jax.ShapeDtypeStruct((total_rows, gw), BF16),
        scratch_shapes=[pltpu.VMEM((qt, gw), F32)],
        compiler_params=_cparams(2),
        name="diff_attn",
    )(qkv, qkv, qkv, qkv, qkv, lam_vecs, subln)


def _ctx_attn_kernel(qkv_ref, lam_ref, subln_ref, ya_in, yd_in, ya_ref, yd_ref, acc_ref, *, lam_init):
    del ya_in, yd_in
    gw = GROUP_WIDTH
    col = lambda g: qkv_ref[:, g * gw:(g + 1) * gw]
    ya = _na_heads(col(0), [col(1)], lambda h: [None], [col(2)])
    ya_ref[...] = ya.astype(BF16)
    lam = _lambda(lam_ref, lam_init)
    yd = _diff_heads(col(3), [col(4)], [col(5)], lam, subln_ref[...], lam_init, acc_ref)
    yd_ref[...] = yd.astype(BF16)


def _ctx_attn_call(qkv, lam_vecs, subln, lam_init, ya, yd, n_batch, seq, ctx_len):
    gw = GROUP_WIDTH
    ctx_blk0 = n_batch * seq // ctx_len
    any_spec = pl.BlockSpec(memory_space=pl.ANY)
    out_spec = pl.BlockSpec((ctx_len, gw), lambda b: (ctx_blk0 + b, 0))
    return pl.pallas_call(
        functools.partial(_ctx_attn_kernel, lam_init=lam_init),
        grid=(n_batch,),
        in_specs=[pl.BlockSpec((ctx_len, 6 * gw), lambda b: (ctx_blk0 + b, 0)),
                  _full(lam_vecs.shape), _full(subln.shape), any_spec, any_spec],
        out_specs=[out_spec, out_spec],
        out_shape=[jax.ShapeDtypeStruct(ya.shape, BF16), jax.ShapeDtypeStruct(yd.shape, BF16)],
        input_output_aliases={3: 0, 4: 1},
        scratch_shapes=[pltpu.VMEM((ctx_len, gw), F32)],
        compiler_params=_cparams(1),
        name="ctx_attn",
    )(qkv, lam_vecs, subln, ya, yd)


def _pool_kernel(p_ref, w_ref, scale_ref, *rest, seq, aliased):
    o_ref, pad_ref = rest[-2], rest[-1]
    del aliased
    gw = GROUP_WIDTH
    halo = POOL_HALO
    pad_ref[0:halo, :] = jnp.zeros((halo, gw), F32)
    pad_ref[halo + seq:, :] = jnp.zeros((halo, gw), F32)
    pad_ref[halo:halo + seq, :] = p_ref[...]
    chunk = min(seq, 256)
    lane_group = lax.broadcasted_iota(jnp.int32, (1, gw), 1) // (gw // len(POOL_WINDOWS))
    for c0 in range(0, seq, chunk):
        def at(off):
            return pad_ref[halo + c0 + off:halo + c0 + off + chunk, :]
        pos = c0 + lax.broadcasted_iota(jnp.int32, (chunk, 1), 0)
        x = at(0)
        run = x
        mean = None
        lo_done, hi_done = 0, 0
        for g, win in enumerate(POOL_WINDOWS):
            half = win // 2
            for off in range(-half, -lo_done):
                run = run + at(off)
            for off in range(hi_done + 1, half):
                run = run + at(off)
            lo_done, hi_done = half, half - 1
            cnt = (jnp.minimum(pos + half, seq) - jnp.maximum(pos - half, 0)).astype(F32)
            m = run / cnt
            mean = m if mean is None else jnp.where(lane_group == g, m, mean)
        y = _dot((mean - x).astype(BF16), w_ref[...]) * scale_ref[...]
        o_ref[c0:c0 + chunk, :] = y.astype(BF16)


def _pool_call(pool_in, w_bd, scale, n_seq, seq, row_blk0, total_rows, prev=None):
    gw = GROUP_WIDTH
    in_specs = [pl.BlockSpec((seq, gw), lambda b: (row_blk0 + b, 0)), _full((gw, gw)), _full((1, gw))]
    args = [pool_in, w_bd, scale]
    aliases = {}
    if prev is not None:
        in_specs.append(pl.BlockSpec(memory_space=pl.ANY))
        args.append(prev)
        aliases = {3: 0}
    return pl.pallas_call(
        functools.partial(_pool_kernel, seq=seq, aliased=prev is not None),
        grid=(n_seq,),
        in_specs=in_specs,
        out_specs=pl.BlockSpec((seq, gw), lambda b: (row_blk0 + b, 0)),
        out_shape=jax.ShapeDtypeStruct((total_rows, gw), BF16),
        scratch_shapes=[pltpu.VMEM((seq + 2 * POOL_HALO, gw), F32)],
        input_output_aliases=aliases,
        compiler_params=_cparams(1),
        name="pool_mix",
    )(*args)


def _fft_kernel(t_ref, cl_ref, sl_ref, cc_ref, sc_ref, w_ref, *rest, norm):
    o_ref, a_ref, b_ref = rest[-3], rest[-2], rest[-1]

    @pl.when(pl.program_id(1) == 0)
    def _():
        t = t_ref[...]
        a_ref[...] = _dot(t, cc_ref[...]).astype(BF16)
        b_ref[...] = _dot(t, sc_ref[...]).astype(BF16)

    f = (_dot(cl_ref[...], a_ref[...]) - _dot(sl_ref[...], b_ref[...])) * norm
    o_ref[...] = _dot(f.astype(BF16), w_ref[...]).astype(BF16)


def _fft_call(fft_in, cl, sl, cc_bd, sc_bd, w_bd, n_seq, seq, row_blk0, total_rows, prev=None):
    gw = GROUP_WIDTH
    tk = min(FFT_TILE, seq)
    n_k = seq // tk
    in_specs = [pl.BlockSpec((seq, gw), lambda b, k: (row_blk0 + b, 0)),
                pl.BlockSpec((tk, seq), lambda b, k: (k, 0)),
                pl.BlockSpec((tk, seq), lambda b, k: (k, 0)),
                _full((gw, gw)), _full((gw, gw)), _full((gw, gw))]
    args = [fft_in, cl, sl, cc_bd, sc_bd, w_bd]
    aliases = {}
    if prev is not None:
        in_specs.append(pl.BlockSpec(memory_space=pl.ANY))
        args.append(prev)
        aliases = {6: 0}
    norm = 1.0 / math.sqrt(seq * (gw // 4))
    return pl.pallas_call(
        functools.partial(_fft_kernel, norm=norm),
        grid=(n_seq, n_k),
        in_specs=in_specs,
        out_specs=pl.BlockSpec((tk, gw), lambda b, k: ((row_blk0 + b) * n_k + k, 0)),
        out_shape=jax.ShapeDtypeStruct((total_rows, gw), BF16),
        scratch_shapes=[pltpu.VMEM((seq, gw), BF16), pltpu.VMEM((seq, gw), BF16)],
        input_output_aliases=aliases,
        compiler_params=_cparams(2),
        name="fourier_mix",
    )(*args)


def _out_kernel(ya_ref, yd_ref, yb_ref, yf_ref, w_ref, xa_ref, xb_ref, g1_ref, sh2_ref, sc2_ref, gn2_ref,
                rw_ref, rb_ref, x1_ref, h2_ref, lg_ref, *, n_lat_tiles):
    gw = GROUP_WIDTH
    x = jnp.where(pl.program_id(0) < n_lat_tiles, xa_ref[...], xb_ref[...])
    acc = _dot(ya_ref[...], w_ref[0:gw, :])
    acc = acc + _dot(yd_ref[...], w_ref[gw:2 * gw, :])
    acc = acc + _dot(yb_ref[...], w_ref[2 * gw:3 * gw, :])
    acc = acc + _dot(yf_ref[...], w_ref[3 * gw:4 * gw, :])
    x1 = x + g1_ref[0] * acc
    x1_ref[...] = x1
    ms = jnp.mean(x1 * x1, axis=-1, keepdims=True)
    h2 = x1 * lax.rsqrt(ms + NORM_EPS) * gn2_ref[...] * (1.0 + sc2_ref[0]) + sh2_ref[0]
    h_hi = h2.astype(BF16)
    h_lo = (h2 - h_hi.astype(F32)).astype(BF16)
    by_hi = _dot(h_hi, rw_ref[...])
    by_lo = _dot(h_lo, rw_ref[:, :LANES])
    lg_ref[...] = by_hi[:, :LANES] + by_hi[:, LANES:] + by_lo + rb_ref[...]
    tm = x1.shape[0]
    for s in range(SLABS):
        h2_ref[pl.ds(s, tm, stride=SLABS), :] = h2[:, s * LANES:(s + 1) * LANES]


def _out_call(ya, yd, yb, yf, w_out_bf, xa, xb, ctx_blk0, mod3, gn2, rw_pad, rb_pad, rows, n_lat_rows, seq):
    d = D_MODEL
    gw = GROUP_WIDTH
    tm = ROW_TILE
    n_lat_tiles = n_lat_rows // tm
    tiles_per_seq = seq // tm
    n_batch = n_lat_rows // seq

    def mod_row(i):
        return jnp.minimum(i // tiles_per_seq, n_batch)

    mix_spec = pl.BlockSpec((tm, gw), lambda i: (i, 0))
    mod_spec = lambda chunk: pl.BlockSpec((1, 1, d), lambda i: (mod_row(i), 0, chunk))
    return pl.pallas_call(
        functools.partial(_out_kernel, n_lat_tiles=n_lat_tiles),
        grid=(rows // tm,),
        in_specs=[mix_spec, mix_spec, mix_spec, mix_spec, _full((4 * gw, d))]
                 + _stream_specs(tm, d, n_lat_tiles, ctx_blk0) + [
                  mod_spec(2), mod_spec(3), mod_spec(4), _full((1, d)),
                  _full(rw_pad.shape), _full(rb_pad.shape)],
        out_specs=[pl.BlockSpec((tm, d), lambda i: (i, 0)),
                   pl.BlockSpec((tm * SLABS, LANES), lambda i: (i, 0)),
                   pl.BlockSpec((tm, LANES), lambda i: (i, 0))],
        out_shape=[jax.ShapeDtypeStruct((rows, d), F32),
                   jax.ShapeDtypeStruct((rows * SLABS, LANES), F32),
                   jax.ShapeDtypeStruct((rows, LANES), F32)],
        compiler_params=_cparams(1),
        name="out_proj",
    )(ya, yd, yb, yf, w_out_bf, xa, xb, mod3, mod3, mod3, gn2, rw_pad, rb_pad)


def _route_kernel(lg_ref, gate_ref, code_ref, cnt_ref, carry_ref):
    i = pl.program_id(0)

    @pl.when(i == 0)
    def _():
        carry_ref[...] = jnp.zeros_like(carry_ref)

    lg = lg_ref[...]
    tm = lg.shape[0]
    lane = lax.broadcasted_iota(jnp.int32, (tm, LANES), 1)
    vals, idxs = [], []
    onehot = jnp.zeros((tm, LANES), F32)
    for _ in range(TOP_K):
        m = lg.max(axis=-1, keepdims=True)
        idx = jnp.min(jnp.where(lg == m, lane, LANES), axis=-1, keepdims=True)
        sel = lane == idx
        onehot = onehot + sel.astype(F32)
        lg = jnp.where(sel, -jnp.inf, lg)
        vals.append(m)
        idxs.append(idx)
    exps = [jnp.exp(v - vals[0]) for v in vals]
    denom = exps[0] + exps[1] + exps[2] + exps[3]
    r_i = lax.broadcasted_iota(jnp.int32, (tm, tm), 0)
    c_i = lax.broadcasted_iota(jnp.int32, (tm, tm), 1)
    tri = (c_i < r_i).astype(BF16)
    before = _dot(tri, onehot.astype(BF16)) + carry_ref[...]
    col4 = lax.broadcasted_iota(jnp.int32, (tm, TOP_K), 1)
    gate_o = jnp.zeros((tm, TOP_K), F32)
    code_o = jnp.zeros((tm, TOP_K), F32)
    for k in range(TOP_K):
        rk = jnp.sum(jnp.where(lane == idxs[k], before, 0.0), axis=-1, keepdims=True)
        gate_o = jnp.where(col4 == k, exps[k] / denom, gate_o)
        code_o = jnp.where(col4 == k, idxs[k].astype(F32) * float(RANK_STRIDE) + rk, code_o)
    gate_ref[...] = gate_o
    code_ref[...] = code_o.astype(jnp.int32)
    carry_ref[...] = carry_ref[...] + jnp.sum(onehot, axis=0, keepdims=True)
    cnt_ref[...] = carry_ref[...].astype(jnp.int32)


def _route_call(logits):
    n = logits.shape[0]
    tm = ROW_TILE
    k_spec = pl.BlockSpec((tm, TOP_K), lambda i: (i, 0))
    return pl.pallas_call(
        _route_kernel,
        grid=(n // tm,),
        in_specs=[pl.BlockSpec((tm, LANES), lambda i: (i, 0))],
        out_specs=[k_spec, k_spec, _full((1, LANES))],
        out_shape=[jax.ShapeDtypeStruct((n, TOP_K), F32),
                   jax.ShapeDtypeStruct((n, TOP_K), jnp.int32),
                   jax.ShapeDtypeStruct((1, LANES), jnp.int32)],
        scratch_shapes=[pltpu.VMEM((1, LANES), F32)],
        compiler_params=_cparams(1),
        name="route",
    )(logits)


def _row_copy(src_ref, src_row, dst_ref, dst_row, sem):
    return pltpu.make_async_copy(
        src_ref.at[pl.ds(pl.multiple_of(src_row * SLABS, SLABS), SLABS), :],
        dst_ref.at[pl.ds(pl.multiple_of(dst_row * SLABS, SLABS), SLABS), :], sem)


def _pad_copy(zero_ref, hs_ref, start_row, n_rows, sem):
    return pltpu.make_async_copy(
        zero_ref.at[pl.ds(0, n_rows * SLABS), :],
        hs_ref.at[pl.ds(pl.multiple_of(start_row * SLABS, SLABS), n_rows * SLABS), :], sem)


def _slot_of(code, start_ref):
    return start_ref[lax.shift_right_logical(code, RANK_SHIFT)] + (code & RANK_MASK)


def _dispatch_kernel(fill_ref, pad_ref, start_ref, dest_hbm, h2_ref, hs_ref, dest_smem, zero_ref, sem, isem):
    i = pl.program_id(0)
    n_assign = TOKEN_TILE * TOP_K
    idx_copy = pltpu.make_async_copy(dest_hbm.at[pl.ds(i * n_assign, n_assign)], dest_smem, isem)
    idx_copy.start()

    @pl.when(i == 0)
    def _():
        zero_ref[...] = jnp.zeros_like(zero_ref)
        bits = [1 << b for b in reversed(range(int(math.log2(EXPERT_BLOCK))))]
        for phase in ("start", "wait"):
            def fill(e, carry, phase=phase):
                pos = fill_ref[e]
                pad = pad_ref[e]
                for bit in bits:
                    @pl.when((pad & bit) != 0)
                    def _(pos=pos, bit=bit):
                        cp = _pad_copy(zero_ref, hs_ref, pos, bit, sem)
                        cp.start() if phase == "start" else cp.wait()
                    pos = pos + (pad & bit)
                return carry

            lax.fori_loop(0, N_EXPERTS, fill, 0)

    idx_copy.wait()

    def start(t, carry):
        for k in range(TOP_K):
            slot = _slot_of(dest_smem[t * TOP_K + k], start_ref)
            _row_copy(h2_ref, t, hs_ref, slot, sem).start(priority=k % 2)
        return carry

    lax.fori_loop(0, TOKEN_TILE, start, 0, unroll=ISSUE_UNROLL)
    for _ in range(TOP_K):
        pltpu.make_async_copy(h2_ref, hs_ref.at[pl.ds(0, TOKEN_TILE * SLABS), :], sem).wait()


def _dispatch_call(fill_start, pad_len, slot_start, dest_flat, h2_slabs, n_slots):
    n = h2_slabs.shape[0] // SLABS
    tt = TOKEN_TILE
    grid_spec = pltpu.PrefetchScalarGridSpec(
        num_scalar_prefetch=3,
        grid=(n // tt,),
        in_specs=[pl.BlockSpec(memory_space=pl.ANY),
                  pl.BlockSpec((tt * SLABS, LANES), lambda i, *_: (i, 0))],
        out_specs=pl.BlockSpec(memory_space=pl.ANY),
        scratch_shapes=[pltpu.SMEM((tt * TOP_K,), jnp.int32),
                        pltpu.VMEM((EXPERT_BLOCK // 2 * SLABS, LANES), F32),
                        pltpu.SemaphoreType.DMA, pltpu.SemaphoreType.DMA],
    )
    return pl.pallas_call(
        _dispatch_kernel,
        grid_spec=grid_spec,
        out_shape=jax.ShapeDtypeStruct((n_slots * SLABS, LANES), F32),
        compiler_params=_cparams(1),
        name="dispatch",
    )(fill_start, pad_len, slot_start, dest_flat, h2_slabs)


GLU_BLOCK = 2 * LANES


def _expert_kernel(be_ref, first_ref, nv_ref, hs_ref, w1_ref, b1_ref, w2_ref, b2_ref, perm_ref,
                   ys_ref, h_ref, act_ref, w1p_ref, w2p_ref):
    del be_ref
    i = pl.program_id(0)
    n_glu = 2 * D_FF // GLU_BLOCK

    @pl.when((first_ref[i] != 0) & (i < nv_ref[0]))
    def _():
        for b in range(n_glu):
            cols = slice(b * GLU_BLOCK, (b + 1) * GLU_BLOCK)
            w1p_ref[:, cols] = _dot(w1_ref[0, :, cols].astype(BF16), perm_ref[...]).astype(BF16)
        w2p_ref[...] = w2_ref[0].astype(BF16)

    @pl.when(i < nv_ref[0])
    def _():
        blk = EXPERT_BLOCK
        for s in range(SLABS):
            h_ref[:, s * LANES:(s + 1) * LANES] = hs_ref[pl.ds(s, blk, stride=SLABS), :].astype(BF16)
        h = h_ref[...]
        for b in range(n_glu):
            cols = slice(b * GLU_BLOCK, (b + 1) * GLU_BLOCK)
            u = _dot(h, w1p_ref[:, cols]) + b1_ref[0, :, cols]
            glu = jnp.minimum(u[:, :LANES], SWIGLU_LIMIT)
            lin = jnp.clip(u[:, LANES:], -SWIGLU_LIMIT, SWIGLU_LIMIT)
            act = glu * jax.nn.sigmoid(SWIGLU_ALPHA * glu) * (lin + 1.0)
            act_ref[:, b * LANES:(b + 1) * LANES] = act.astype(BF16)
        y = _dot(act_ref[...], w2p_ref[...]) + b2_ref[0]
        for s in range(SLABS):
            ys_ref[pl.ds(s, blk, stride=SLABS), :] = y[:, s * LANES:(s + 1) * LANES]


def _expert_call(block_e, first, n_valid, hs, w1, b1p, w2, b2, perm, layer, n_blocks):
    blk = EXPERT_BLOCK
    d = D_MODEL

    def row_blk(i, be, first, nv):
        return (jnp.minimum(i, nv[0] - 1), 0)

    def expert_blk(i, be, first, nv):
        return (be[i], 0, 0)

    def weight_blk(i, be, first, nv):
        return (layer, be[i], 0, 0)

    grid_spec = pltpu.PrefetchScalarGridSpec(
        num_scalar_prefetch=3,
        grid=(n_blocks,),
        in_specs=[pl.BlockSpec((blk * SLABS, LANES), row_blk),
                  pl.BlockSpec((None, 1, d, 2 * D_FF), weight_blk),
                  pl.BlockSpec((1, 1, 2 * D_FF), expert_blk),
                  pl.BlockSpec((None, 1, D_FF, d), weight_blk),
                  pl.BlockSpec((1, 1, d), expert_blk),
                  pl.BlockSpec((GLU_BLOCK, GLU_BLOCK), lambda i, *_: (0, 0))],
        out_specs=pl.BlockSpec((blk * SLABS, LANES), row_blk),
        scratch_shapes=[pltpu.VMEM((blk, d), BF16), pltpu.VMEM((blk, D_FF), BF16),
                        pltpu.VMEM((d, 2 * D_FF), BF16), pltpu.VMEM((D_FF, d), BF16)],
    )
    return pl.pallas_call(
        _expert_kernel,
        grid_spec=grid_spec,
        out_shape=jax.ShapeDtypeStruct(hs.shape, F32),
        compiler_params=_cparams(1, EXPERT_VMEM_LIMIT),
        name="expert_ffn",
    )(block_e, first, n_valid, hs, w1, b1p, w2, b2, perm)


def _combine_kernel(start_ref, dest_hbm, ys_hbm, gate_ref, x_ref, g2_ref, *rest, n_next):
    next_in, rest = rest[:n_next], rest[n_next:]
    o_ref, next_out = rest[0], rest[1:len(rest) - 6]
    idx0, idx1, buf0, buf1, sems, isems = rest[len(rest) - 6:]
    i = pl.program_id(0)
    n_tiles = pl.num_programs(0)
    tt = TOKEN_TILE
    n_assign = tt * TOP_K
    idx_refs, buf_refs = (idx0, idx1), (buf0, buf1)

    def idx_copy(tile, s):
        return pltpu.make_async_copy(dest_hbm.at[pl.ds(tile * n_assign, n_assign)], idx_refs[s],
                                     isems.at[s])

    def issue(s):
        def start(t, carry):
            for k in range(TOP_K):
                slot = _slot_of(idx_refs[s][t * TOP_K + k], start_ref)
                _row_copy(ys_hbm, slot, buf_refs[s], k * tt + t, sems.at[s]).start(priority=k % 2)
            return carry
        lax.fori_loop(0, tt, start, 0, unroll=ISSUE_UNROLL)

    @pl.when(i == 0)
    def _():
        idx_copy(0, 0).start()
        idx_copy(0, 0).wait()
        issue(0)

        @pl.when(n_tiles > 1)
        def _():
            idx_copy(1, 1).start()

    for s in (0, 1):
        @pl.when(i % 2 == s)
        def _(s=s):
            @pl.when(i + 1 < n_tiles)
            def _():
                idx_copy(i + 1, 1 - s).wait()
                issue(1 - s)

            pltpu.make_async_copy(ys_hbm.at[pl.ds(0, n_assign * SLABS), :], buf_refs[s], sems.at[s]).wait()

            @pl.when(i + 2 < n_tiles)
            def _():
                idx_copy(i + 2, s).start()

            gate = gate_ref[...]
            g2 = g2_ref[0]
            for sl in range(SLABS):
                cols = slice(sl * LANES, (sl + 1) * LANES)
                y = jnp.zeros((tt, LANES), F32)
                for k in range(TOP_K):
                    y = y + gate[:, k:k + 1] * buf_refs[s][pl.ds(k * tt * SLABS + sl, tt, stride=SLABS), :]
                o_ref[:, cols] = x_ref[:, cols] + g2[:, cols] * y

    if n_next:
        _in_body(o_ref[...], *next_in, *next_out)


def _combine_call(slot_start, dest_flat, ys, gate, x1, mod3, n_lat_rows, seq, next_params=None):
    n, d = x1.shape
    tt = TOKEN_TILE
    tiles_per_seq = seq // tt
    n_batch = n_lat_rows // seq

    def mod_row(i):
        return jnp.minimum(i // tiles_per_seq, n_batch)

    in_specs = [pl.BlockSpec(memory_space=pl.ANY), pl.BlockSpec(memory_space=pl.ANY),
                pl.BlockSpec((tt, TOP_K), lambda i, *_: (i, 0)),
                pl.BlockSpec((tt, d), lambda i, *_: (i, 0)),
                pl.BlockSpec((1, 1, d), lambda i, *_: (mod_row(i), 0, 5))]
    args = [dest_flat, ys, gate, x1, mod3]
    out_specs = [pl.BlockSpec((tt, d), lambda i, *_: (i, 0))]
    out_shape = [jax.ShapeDtypeStruct((n, d), F32)]
    n_next = 0
    if next_params is not None:
        nx_specs, nx_args, nx_out_specs, nx_out_shape = _in_proj_io(tt, n, next_params, n_lat_rows, seq)
        in_specs, args, n_next = in_specs + nx_specs, args + nx_args, len(nx_args)
        out_specs, out_shape = out_specs + nx_out_specs, out_shape + nx_out_shape
    grid_spec = pltpu.PrefetchScalarGridSpec(
        num_scalar_prefetch=1,
        grid=(n // tt,),
        in_specs=in_specs,
        out_specs=out_specs,
        scratch_shapes=[pltpu.SMEM((tt * TOP_K,), jnp.int32), pltpu.SMEM((tt * TOP_K,), jnp.int32),
                        pltpu.VMEM((TOP_K * tt * SLABS, LANES), F32),
                        pltpu.VMEM((TOP_K * tt * SLABS, LANES), F32),
                        pltpu.SemaphoreType.DMA((2,)), pltpu.SemaphoreType.DMA((2,))],
    )
    return pl.pallas_call(
        functools.partial(_combine_kernel, n_next=n_next),
        grid_spec=grid_spec,
        out_shape=out_shape,
        compiler_params=_cparams(1),
        name="combine",
    )(slot_start, *args)


def _block_diag(blocks):
    g, a, b = blocks.shape
    eye = jnp.eye(g, dtype=blocks.dtype)
    return (eye[:, None, :, None] * blocks[:, :, None, :]).reshape(g * a, g * b)


def _rope_tables(seq, extra_rows):
    half = DIFF_QK_DIM // 2
    inv = ROPE_THETA ** (-jnp.arange(0, half, 2, dtype=F32) / half)
    pos = jnp.arange(seq)
    c = jnp.arange(GROUP_WIDTH)
    dd = c % DIFF_QK_DIM
    axis_pos = jnp.where((dd < half)[None, :], (pos // GRID_W)[:, None], (pos % GRID_W)[:, None])
    ang = axis_pos.astype(F32) * inv[dd % (half // 2)][None, :]
    sign = jnp.where((dd % half) < half // 2, -1.0, 1.0).astype(F32)
    cos_t = jnp.concatenate([jnp.cos(ang), jnp.ones((extra_rows, GROUP_WIDTH), F32)], axis=0)
    sin_t = jnp.concatenate([jnp.sin(ang) * sign[None, :], jnp.zeros((extra_rows, GROUP_WIDTH), F32)], axis=0)
    return cos_t, sin_t


def _dft_tables(n, dtype):
    k = np.arange(n, dtype=np.int64)
    ang = ((k[:, None] * k[None, :]) % n).astype(np.float64) * (2.0 * math.pi / n)
    return jnp.asarray(np.cos(ang), dtype), jnp.asarray(np.sin(ang), dtype)


def _na_bias_tables(rpb, grid_rows):
    w = GRID_W
    col = np.arange(w)
    col_start = np.clip(col - NA_COLS // 2, 0, w - NA_COLS)
    in_win = (col[None, :] >= col_start[:, None]) & (col[None, :] < col_start[:, None] + NA_COLS)
    rel_c = np.clip(col[None, :] - col[:, None], 1 - NA_COLS, NA_COLS - 1) + NA_COLS - 1
    n_tiles = grid_rows // NA_Q_ROWS
    cases = (0, 1, n_tiles - 1)
    rel_r = np.zeros((3, NA_Q_ROWS, NA_KEY_ROWS), np.int32)
    valid = np.zeros((3, NA_Q_ROWS, NA_KEY_ROWS), bool)
    for ci, tile in enumerate(cases):
        r0 = tile * NA_Q_ROWS
        k0 = int(np.clip(r0 - NA_ROWS // 2, 0, grid_rows - NA_KEY_ROWS))
        for j in range(NA_Q_ROWS):
            start = int(np.clip(r0 + j - NA_ROWS // 2, 0, grid_rows - NA_ROWS))
            for i in range(NA_KEY_ROWS):
                kr = k0 + i
                valid[ci, j, i] = start <= kr < start + NA_ROWS
                rel_r[ci, j, i] = np.clip(kr - (r0 + j) + NA_ROWS - 1, 0, 2 * NA_ROWS - 2)
    del rel_c
    edge = w - NA_COLS
    ext = jnp.pad(rpb.astype(F32), ((0, 0), (0, 0), (edge, edge)), mode='edge')
    t = jnp.stack([ext[:, :, w - 1 - q:2 * w - 1 - q] for q in range(w)], axis=2)
    t = jnp.where(in_win[None, None], t, MASK_VALUE)
    masked = jnp.full((rpb.shape[0], w, w), MASK_VALUE, F32)
    return jnp.stack([jnp.concatenate([jnp.concatenate(
        [t[:, rel_r[c, j, i]] if valid[c, j, i] else masked for i in range(NA_KEY_ROWS)], axis=-1)
        for j in range(NA_Q_ROWS)], axis=-2) for c in range(3)], axis=0)


def _moe(h2_slabs, logits, x1, mod3, w1, b1p, w2, b2, perm, layer, n_lat_rows, seq, next_params):
    n = x1.shape[0]
    blk = EXPERT_BLOCK
    assert n * TOP_K <= RANK_STRIDE
    gate, code, counts = _route_call(logits)
    counts = counts[0, :N_EXPERTS]
    padded = (counts + blk - 1) // blk * blk
    padded_end = jnp.cumsum(padded)
    padded_start = padded_end - padded
    codes = code.reshape(n * TOP_K)
    slot_start = padded_start.astype(jnp.int32)
    n_blocks = n * TOP_K // blk + N_EXPERTS
    block_row0 = jnp.arange(n_blocks, dtype=jnp.int32) * blk
    block_e = jnp.minimum(jnp.sum((padded_end[None, :] <= block_row0[:, None]).astype(jnp.int32), axis=1),
                          N_EXPERTS - 1).astype(jnp.int32)
    first = jnp.concatenate([jnp.ones((1,), jnp.int32),
                             (block_e[1:] != block_e[:-1]).astype(jnp.int32)])
    n_valid = (padded_end[-1:] // blk).astype(jnp.int32)
    fill_start = (padded_start + counts).astype(jnp.int32)
    pad_len = (padded - counts).astype(jnp.int32)
    hs = _dispatch_call(fill_start, pad_len, slot_start, codes, h2_slabs, n_blocks * blk)
    ys = _expert_call(block_e, first, n_valid, hs, w1, b1p, w2, b2, perm, layer, n_blocks)
    return _combine_call(slot_start, codes, ys, gate, x1, mod3, n_lat_rows, seq, next_params)


def kernel(x, c, ctx, c_ctx, w_ada, b_ada, g_norm1, g_norm2, w_in, w_out, na_q_gain, na_k_gain, na_rpb, diff_q_gain, diff_k_gain, diff_lambda_q1, diff_lambda_k1, diff_lambda_q2, diff_lambda_k2, diff_subln, pool_w, pool_scale, fft_w, router_w, router_b, moe_w1, moe_b1, moe_w2, moe_b2):
    n_batch, seq, d = x.shape
    ctx_len = ctx.shape[1]
    depth = w_ada.shape[0]
    gw = GROUP_WIDTH
    assert d == D_MODEL and seq % ROW_TILE == 0 and seq % ctx_len == 0 and ctx_len % TOKEN_TILE == 0
    assert (n_batch * ctx_len) % ROW_TILE == 0 and (seq // GRID_W) >= NA_KEY_ROWS
    n_lat = n_batch * seq
    n_ctx = n_batch * ctx_len

    xa, xb, ctx_blk0 = x.reshape(n_lat, d), ctx.reshape(n_ctx, d), 0
    mod_rows = -(-(n_batch + 1) // SUBLANES) * SUBLANES
    cvec = jnp.zeros((mod_rows, d), F32).at[:n_batch].set(c).at[n_batch].set(c_ctx)
    mod = _ada_call(cvec, w_ada, b_ada)

    cos_t, sin_t = _rope_tables(seq, ROW_TILE)
    ones = lambda w: _block_diag(jnp.ones((gw // w, w, w), BF16))
    g64, g32 = ones(HEAD_DIM), ones(DIFF_QK_DIM)
    cl_lat, sl_lat = _dft_tables(seq, BF16)
    cl_ctx, sl_ctx = _dft_tables(ctx_len, BF16)
    cc, sc = _dft_tables(gw // 4, F32)
    n_grp = fft_w.shape[1]
    cc_bd = _block_diag(jnp.broadcast_to(cc, (n_grp,) + cc.shape)).astype(BF16)
    sc_bd = _block_diag(jnp.broadcast_to(sc, (n_grp,) + sc.shape)).astype(BF16)
    src = np.arange(GLU_BLOCK)
    dst = np.where(src % 2 == 0, src // 2, LANES + src // 2)
    perm_np = np.zeros((GLU_BLOCK, GLU_BLOCK), np.float32)
    perm_np[src, dst] = 1.0
    glu_perm = jnp.asarray(perm_np, BF16)

    tile = lambda v: jnp.tile(v.astype(F32), gw // v.shape[0])

    def in_proj_params(l):
        gains = jnp.stack([tile(na_q_gain[l]) * (HEAD_DIM ** -0.5 * LOG2_E), tile(na_k_gain[l]),
                           tile(diff_q_gain[l]) * (DIFF_QK_DIM ** -0.5 * LOG2_E), tile(diff_k_gain[l])]
                          + [jnp.zeros((gw,), F32)] * 4)
        return (mod[l].reshape(mod_rows, 1, 6 * d), g_norm1[l].reshape(1, d), w_in[l].astype(BF16),
                gains, cos_t, sin_t, g64, g32)

    qkv, pool_in, fft_in = _in_call(xa, xb, ctx_blk0, n_lat + n_ctx, in_proj_params(0), n_lat, seq)
    for l in range(depth):
        ctx_out = l < depth - 1
        lam_init = 0.8 - 0.6 * math.exp(-0.3 * l)
        mod3 = mod[l].reshape(mod_rows, 1, 6 * d)
        lam_vecs = jnp.stack([diff_lambda_q1[l], diff_lambda_k1[l], diff_lambda_q2[l], diff_lambda_k2[l]]
                             + [jnp.zeros_like(diff_lambda_q1[l])] * 4).astype(F32)
        subln = tile(diff_subln[l]).reshape(1, gw)
        bias_tab = _na_bias_tables(na_rpb[l] * LOG2_E, seq // GRID_W)
        pool_bd = _block_diag(pool_w[l]).astype(BF16)
        fftw_bd = _block_diag(fft_w[l]).astype(BF16)
        rw_f32 = jnp.zeros((d, LANES), F32).at[:, :N_EXPERTS].set(router_w[l])
        rw_hi = rw_f32.astype(BF16)
        rw_pad = jnp.concatenate([rw_hi, (rw_f32 - rw_hi.astype(F32)).astype(BF16)], axis=1)
        rb_pad = jnp.full((1, LANES), -jnp.inf, F32).at[0, :N_EXPERTS].set(router_b[l])
        n_e = moe_w1.shape[1]
        b1p = moe_b1[l].reshape(n_e, 2 * D_FF // GLU_BLOCK, LANES, 2).transpose(0, 1, 3, 2)
        b1p = b1p.reshape(n_e, 1, 2 * D_FF)
        b2 = moe_b2[l].reshape(n_e, 1, d)

        rows = n_lat + n_ctx if ctx_out else n_lat
        ya = _na_call(qkv, bias_tab, n_batch, seq, rows, ctx_len)
        yd = _df_call(qkv, lam_vecs, subln, lam_init, n_batch, seq, rows, ctx_len)
        yb = _pool_call(pool_in, pool_bd, pool_scale[l].reshape(1, gw), n_batch, seq, 0, rows)
        yf = _fft_call(fft_in, cl_lat, sl_lat, cc_bd, sc_bd, fftw_bd, n_batch, seq, 0, rows)
        if ctx_out:
            ya, yd = _ctx_attn_call(qkv, lam_vecs, subln, lam_init, ya, yd, n_batch, seq, ctx_len)
            yb = _pool_call(pool_in, pool_bd, pool_scale[l].reshape(1, gw), n_batch, ctx_len,
                            n_lat // ctx_len, rows, prev=yb)
            yf = _fft_call(fft_in, cl_ctx, sl_ctx, cc_bd, sc_bd, fftw_bd, n_batch, ctx_len,
                           n_lat // ctx_len, rows, prev=yf)
        x1, h2_slabs, logits = _out_call(ya, yd, yb, yf, w_out[l].astype(BF16), xa, xb, ctx_blk0, mod3,
                                         g_norm2[l].reshape(1, d), rw_pad, rb_pad, rows, n_lat, seq)
        next_params = in_proj_params(l + 1) if ctx_out else None
        outs = _moe(h2_slabs, logits, x1, mod3, moe_w1, b1p, moe_w2, b2, glu_perm, l, n_lat, seq,
                    next_params)
        x2 = outs[0]
        if ctx_out:
            qkv, pool_in, fft_in = outs[1:]
        xa, xb, ctx_blk0 = x2, x2, n_lat // ROW_TILE
    return x2[:n_lat].reshape(n_batch, seq, d)
```

```python
import functools
import math

import numpy as np
import jax
import jax.numpy as jnp
from jax import lax
from jax.experimental import pallas as pl
from jax.experimental.pallas import tpu as pltpu

F32 = jnp.float32
BF16 = jnp.bfloat16
HIGHEST = lax.Precision.HIGHEST

D_MODEL = 1024
DEPTH = 2
GRID_W = 64
HEAD_DIM = 64
GROUP_WIDTH = 256
N_HEADS = GROUP_WIDTH // HEAD_DIM
DIFF_QK_DIM = HEAD_DIM // 2
NA_ROWS = 8
NA_COLS = 16
POOL_WINDOWS = (2, 4, 8, 16)
POOL_HALO = max(POOL_WINDOWS) // 2
N_EXPERTS = 32
TOP_K = 4
D_FF = D_MODEL
SWIGLU_ALPHA = 1.702
SWIGLU_LIMIT = 7.0
ROPE_THETA = 10000.0
NORM_EPS = 1e-6
MASK_VALUE = -1e30

LANES = 128
SUBLANES = 8
ROW_TILE = 512
TOKEN_TILE = 256
EXPERT_BLOCK = 512
Q_TILE = 256
DIFF_Q_TILE = 512
LOG2_E = math.log2(math.e)
NA_Q_ROWS = Q_TILE // GRID_W
NA_KEY_ROWS = NA_ROWS + NA_Q_ROWS - 1
FFT_TILE = 512
SLABS = D_MODEL // LANES
ISSUE_UNROLL = 8
RANK_STRIDE = 1 << 17
RANK_MASK = RANK_STRIDE - 1
RANK_SHIFT = 17
VMEM_LIMIT = 48 * 1024 * 1024
EXPERT_VMEM_LIMIT = 56 * 1024 * 1024


def _cparams(n_axes, vmem_limit=VMEM_LIMIT):
    return pltpu.CompilerParams(dimension_semantics=("arbitrary",) * n_axes,
                                vmem_limit_bytes=vmem_limit)


def _dot(a, b):
    return jnp.dot(a, b, preferred_element_type=F32)


def _dot_nt(a, b):
    return lax.dot_general(a, b, (((1,), (1,)), ((), ())), preferred_element_type=F32)


def _full(shape):
    zeros = (0,) * len(shape)
    return pl.BlockSpec(shape, lambda *_: zeros)


def _ada_kernel(c_ref, w_ref, b_ref, o_ref):
    c = c_ref[...]
    act = c * jax.nn.sigmoid(c)
    o_ref[0] = jnp.dot(act, w_ref[0], precision=HIGHEST, preferred_element_type=F32) + b_ref[0]


def _ada_call(cvec, w_ada, b_ada):
    depth, d, n = w_ada.shape
    r = cvec.shape[0]
    tn = 1024
    return pl.pallas_call(
        _ada_kernel,
        grid=(depth, n // tn),
        in_specs=[_full((r, d)),
                  pl.BlockSpec((1, d, tn), lambda l, j: (l, 0, j)),
                  pl.BlockSpec((1, 1, tn), lambda l, j: (l, 0, j))],
        out_specs=pl.BlockSpec((1, r, tn), lambda l, j: (l, 0, j)),
        out_shape=jax.ShapeDtypeStruct((depth, r, n), F32),
        compiler_params=_cparams(2),
        name="ada_mod",
    )(cvec, w_ada, b_ada.reshape(depth, 1, n))


def _seg_mean_sq(p, gmat_ref, width):
    sq = p * p
    hi = sq.astype(BF16)
    lo = (sq - hi.astype(F32)).astype(BF16)
    g = gmat_ref[...]
    return (_dot(hi, g) + _dot(lo, g)) * (1.0 / width)


def _stream_specs(tm, d, n_lat_tiles, ctx_blk0):
    return [pl.BlockSpec((tm, d), lambda i: (jnp.minimum(i, n_lat_tiles - 1), 0)),
            pl.BlockSpec((tm, d), lambda i: (ctx_blk0 + jnp.maximum(i - n_lat_tiles, 0), 0))]


def _in_kernel(xa_ref, xb_ref, *rest, n_lat_tiles):
    x = jnp.where(pl.program_id(0) < n_lat_tiles, xa_ref[...], xb_ref[...])
    _in_body(x, *rest)


def _in_body(x, sh_ref, sc_ref, g_ref, w_ref, gains_ref, cos_ref, sin_ref, g64_ref, g32_ref,
             qkv_ref, pool_ref, fft_ref):
    ms = jnp.mean(x * x, axis=-1, keepdims=True)
    y = x * lax.rsqrt(ms + NORM_EPS) * g_ref[...]
    h = (y * (1.0 + sc_ref[0]) + sh_ref[0]).astype(BF16)
    gw = GROUP_WIDTH

    def proj(g):
        return _dot(h, w_ref[:, g * gw:(g + 1) * gw])

    def put(g, val):
        qkv_ref[:, g * gw:(g + 1) * gw] = val.astype(BF16)

    def normed(p, gmat_ref, width, row):
        return p * lax.rsqrt(_seg_mean_sq(p, gmat_ref, width) + NORM_EPS) * gains_ref[row:row + 1, :]

    lane = lax.broadcasted_iota(jnp.int32, (1, gw), 1)
    first_half = (lane % 16) < 8

    def rope(p):
        rot = jnp.where(first_half, pltpu.roll(p, gw - 8, 1), pltpu.roll(p, 8, 1))
        return p * cos_ref[...] + rot * sin_ref[...]

    put(0, normed(proj(0), g64_ref, HEAD_DIM, 0))
    put(1, normed(proj(1), g64_ref, HEAD_DIM, 1))
    put(2, proj(2))
    put(3, rope(normed(proj(3), g32_ref, DIFF_QK_DIM, 2)))
    put(4, rope(normed(proj(4), g32_ref, DIFF_QK_DIM, 3)))
    put(5, proj(5))
    pool_ref[...] = proj(6)
    fft_ref[...] = proj(7).astype(BF16)


def _in_proj_io(tm, rows, params, n_lat_rows, seq):
    mod3, g1, w_in_bf, gains, cos_t, sin_t, g64, g32 = params
    d = D_MODEL
    gw = GROUP_WIDTH
    n_lat_tiles = n_lat_rows // tm
    tiles_per_seq = seq // tm
    n_batch = n_lat_rows // seq

    def mod_row(i):
        return jnp.minimum(i // tiles_per_seq, n_batch)

    def tab_row(i):
        return jnp.where(i < n_lat_tiles, i % tiles_per_seq, tiles_per_seq)

    full = lambda shape: pl.BlockSpec(shape, lambda i, *_: (0,) * len(shape))
    in_specs = [pl.BlockSpec((1, 1, d), lambda i, *_: (mod_row(i), 0, 0)),
                pl.BlockSpec((1, 1, d), lambda i, *_: (mod_row(i), 0, 1)),
                full((1, d)), full(w_in_bf.shape), full(gains.shape),
                pl.BlockSpec((tm, gw), lambda i, *_: (tab_row(i), 0)),
                pl.BlockSpec((tm, gw), lambda i, *_: (tab_row(i), 0)),
                full((gw, gw)), full((gw, gw))]
    args = [mod3, mod3, g1, w_in_bf, gains, cos_t, sin_t, g64, g32]
    out_specs = [pl.BlockSpec((tm, 6 * gw), lambda i, *_: (i, 0)),
                 pl.BlockSpec((tm, gw), lambda i, *_: (i, 0)),
                 pl.BlockSpec((tm, gw), lambda i, *_: (i, 0))]
    out_shape = [jax.ShapeDtypeStruct((rows, 6 * gw), BF16),
                 jax.ShapeDtypeStruct((rows, gw), F32),
                 jax.ShapeDtypeStruct((rows, gw), BF16)]
    return in_specs, args, out_specs, out_shape


def _in_call(xa, xb, ctx_blk0, rows, params, n_lat_rows, seq):
    tm = ROW_TILE
    n_lat_tiles = n_lat_rows // tm
    in_specs, args, out_specs, out_shape = _in_proj_io(tm, rows, params, n_lat_rows, seq)
    return pl.pallas_call(
        functools.partial(_in_kernel, n_lat_tiles=n_lat_tiles),
        grid=(rows // tm,),
        in_specs=_stream_specs(tm, D_MODEL, n_lat_tiles, ctx_blk0) + in_specs,
        out_specs=out_specs,
        out_shape=out_shape,
        compiler_params=_cparams(1),
        name="in_proj",
    )(xa, xb, *args)


def _lane_mask(width, start, size):
    lane = lax.broadcasted_iota(jnp.int32, (1, width), 1)
    return (lane >= start) & (lane < start + size)


def _softmax_pv(qm, keys, biases, vals, sum_lane=None):
    scores = []
    for k, bias in zip(keys, biases):
        s = _dot_nt(qm, k)
        scores.append(s if bias is None else s + bias)
    m = scores[0].max(axis=-1, keepdims=True)
    for s in scores[1:]:
        m = jnp.maximum(m, s.max(axis=-1, keepdims=True))
    o = None
    l = None
    for s, v in zip(scores, vals):
        e = jnp.exp2(s - m)
        part = _dot(e.astype(BF16), v)
        o = part if o is None else o + part
        if sum_lane is None:
            ls = e.sum(axis=-1, keepdims=True)
            l = ls if l is None else l + ls
    if sum_lane is not None:
        lane = lax.broadcasted_iota(jnp.int32, (1, o.shape[1]), 1)
        l = jnp.sum(jnp.where(lane == sum_lane, o, 0.0), axis=-1, keepdims=True)
    return o, l


def _na_heads(q, keys, bias_fn, vals):
    gw = GROUP_WIDTH
    acc = jnp.zeros((q.shape[0], gw), F32)
    for h in range(N_HEADS):
        mask = _lane_mask(gw, h * HEAD_DIM, HEAD_DIM)
        qm = jnp.where(mask, q, jnp.zeros_like(q))
        o, l = _softmax_pv(qm, keys, bias_fn(h), vals)
        acc = acc + jnp.where(mask, o / l, 0.0)
    return acc


def _lambda(lam_ref, lam_init):
    lv = lam_ref[...]
    d1 = jnp.sum(lv[0:1, :] * lv[1:2, :], axis=-1, keepdims=True)
    d2 = jnp.sum(lv[2:3, :] * lv[3:4, :], axis=-1, keepdims=True)
    return jnp.exp(d1) - jnp.exp(d2) + lam_init


def _diff_heads(q, keys, vals, lam, subln, lam_init, acc_ref):
    gw = GROUP_WIDTH
    lane = lax.broadcasted_iota(jnp.int32, (1, gw), 1)
    nones = [None] * len(keys)
    acc_ref[...] = jnp.zeros_like(acc_ref)

    del nones

    def scores_of(h, part):
        plo = h * HEAD_DIM + part * DIFF_QK_DIM
        qm = jnp.where((lane >= plo) & (lane < plo + DIFF_QK_DIM), q, jnp.zeros_like(q))
        scores = [_dot_nt(qm, k) for k in keys]
        m = scores[0].max(axis=-1, keepdims=True)
        for s in scores[1:]:
            m = jnp.maximum(m, s.max(axis=-1, keepdims=True))
        return scores, m

    def attend(h, scores, m):
        lo = h * HEAD_DIM
        hm = (lane >= lo) & (lane < lo + HEAD_DIM)
        o = None
        for s, v in zip(scores, vals):
            part = _dot(jnp.exp2(s - m).astype(BF16), jnp.where(hm, v, jnp.ones_like(v)))
            o = part if o is None else o + part
        l = jnp.sum(jnp.where(lane == (lo + HEAD_DIM) % gw, o, 0.0), axis=-1, keepdims=True)
        return o / l

    combos = [(h, part) for h in range(N_HEADS) for part in range(2)]
    pending = [scores_of(*combos[0])]
    outs = []
    for c, (h, part) in enumerate(combos):
        if c + 1 < len(combos):
            pending.append(scores_of(*combos[c + 1]))
        outs.append(attend(h, *pending.pop(0)))
        if part == 1:
            lo = h * HEAD_DIM
            hm = (lane >= lo) & (lane < lo + HEAD_DIM)
            a = jnp.where(hm, outs[-2] - lam * outs[-1], 0.0)
            ms = jnp.sum(a * a, axis=-1, keepdims=True) * (1.0 / HEAD_DIM)
            acc_ref[...] += a * lax.rsqrt(ms + NORM_EPS)
    return acc_ref[...] * subln * (1.0 - lam_init)


def _na_kernel(q_ref, k_ref, v_ref, kc_ref, vc_ref, bias_ref, o_ref, *, grid_rows):
    j = pl.program_id(1)
    key_row0 = jnp.clip(j * NA_Q_ROWS - NA_ROWS // 2, 0, grid_rows - NA_KEY_ROWS)
    ks = pl.multiple_of(key_row0 * GRID_W, GRID_W)
    n_win = NA_KEY_ROWS * GRID_W
    kwin = k_ref[pl.ds(ks, n_win), :]
    vwin = v_ref[pl.ds(ks, n_win), :]
    acc = _na_heads(q_ref[...], [kwin, kc_ref[...]], lambda h: [bias_ref[0, h], None],
                    [vwin, vc_ref[...]])
    o_ref[...] = acc.astype(BF16)


def _na_call(qkv, bias_tab, n_batch, seq, total_rows, ctx_len):
    gw = GROUP_WIDTH
    qt = Q_TILE
    n_q = seq // qt
    grid_rows = seq // GRID_W
    ctx_blk0 = n_batch * seq // ctx_len

    def bias_case(j):
        return jnp.where(j == 0, 0, jnp.where(j == n_q - 1, 2, 1))

    return pl.pallas_call(
        functools.partial(_na_kernel, grid_rows=grid_rows),
        grid=(n_batch, n_q),
        in_specs=[pl.BlockSpec((qt, gw), lambda b, j: (b * n_q + j, 0)),
                  pl.BlockSpec((seq, gw), lambda b, j: (b, 1)),
                  pl.BlockSpec((seq, gw), lambda b, j: (b, 2)),
                  pl.BlockSpec((ctx_len, gw), lambda b, j: (ctx_blk0 + b, 1)),
                  pl.BlockSpec((ctx_len, gw), lambda b, j: (ctx_blk0 + b, 2)),
                  pl.BlockSpec((1,) + bias_tab.shape[1:], lambda b, j: (bias_case(j), 0, 0, 0))],
        out_specs=pl.BlockSpec((qt, gw), lambda b, j: (b * n_q + j, 0)),
        out_shape=jax.ShapeDtypeStruct((total_rows, gw), BF16),
        compiler_params=_cparams(2),
        name="na_attn",
    )(qkv, qkv, qkv, qkv, qkv, bias_tab)


def _df_kernel(q_ref, k_ref, v_ref, kc_ref, vc_ref, lam_ref, subln_ref, o_ref, acc_ref, *, lam_init):
    lam = _lambda(lam_ref, lam_init)
    acc = _diff_heads(q_ref[...], [kc_ref[...], k_ref[...]], [vc_ref[...], v_ref[...]], lam,
                      subln_ref[...], lam_init, acc_ref)
    o_ref[...] = acc.astype(BF16)


def _df_call(qkv, lam_vecs, subln, lam_init, n_batch, seq, total_rows, ctx_len):
    gw = GROUP_WIDTH
    qt = DIFF_Q_TILE
    n_q = seq // qt
    ctx_blk0 = n_batch * seq // ctx_len
    return pl.pallas_call(
        functools.partial(_df_kernel, lam_init=lam_init),
        grid=(n_batch, n_q),
        in_specs=[pl.BlockSpec((qt, gw), lambda b, j: (b * n_q + j, 3)),
                  pl.BlockSpec((seq, gw), lambda b, j: (b, 4)),
                  pl.BlockSpec((seq, gw), lambda b, j: (b, 5)),
                  pl.BlockSpec((ctx_len, gw), lambda b, j: (ctx_blk0 + b, 4)),
                  pl.BlockSpec((ctx_len, gw), lambda b, j: (ctx_blk0 + b, 5)),
                  _full(lam_vecs.shape),
                  _full(subln.shape)],
        out_specs=pl.BlockSpec((qt, gw), lambda b, j: (b * n_q + j, 0)),
        out_shape=jax.ShapeDtypeStruct((total_rows, gw), BF16),
        scratch_shapes=[pltpu.VMEM((qt, gw), F32)],
        compiler_params=_cparams(2),
        name="diff_attn",
    )(qkv, qkv, qkv, qkv, qkv, lam_vecs, subln)


def _ctx_attn_kernel(qkv_ref, lam_ref, subln_ref, ya_in, yd_in, ya_ref, yd_ref, acc_ref, *, lam_init):
    del ya_in, yd_in
    gw = GROUP_WIDTH
    col = lambda g: qkv_ref[:, g * gw:(g + 1) * gw]
    ya = _na_heads(col(0), [col(1)], lambda h: [None], [col(2)])
    ya_ref[...] = ya.astype(BF16)
    lam = _lambda(lam_ref, lam_init)
    yd = _diff_heads(col(3), [col(4)], [col(5)], lam, subln_ref[...], lam_init, acc_ref)
    yd_ref[...] = yd.astype(BF16)


def _ctx_attn_call(qkv, lam_vecs, subln, lam_init, ya, yd, n_batch, seq, ctx_len):
    gw = GROUP_WIDTH
    ctx_blk0 = n_batch * seq // ctx_len
    any_spec = pl.BlockSpec(memory_space=pl.ANY)
    out_spec = pl.BlockSpec((ctx_len, gw), lambda b: (ctx_blk0 + b, 0))
    return pl.pallas_call(
        functools.partial(_ctx_attn_kernel, lam_init=lam_init),
        grid=(n_batch,),
        in_specs=[pl.BlockSpec((ctx_len, 6 * gw), lambda b: (ctx_blk0 + b, 0)),
                  _full(lam_vecs.shape), _full(subln.shape), any_spec, any_spec],
        out_specs=[out_spec, out_spec],
        out_shape=[jax.ShapeDtypeStruct(ya.shape, BF16), jax.ShapeDtypeStruct(yd.shape, BF16)],
        input_output_aliases={3: 0, 4: 1},
        scratch_shapes=[pltpu.VMEM((ctx_len, gw), F32)],
        compiler_params=_cparams(1),
        name="ctx_attn",
    )(qkv, lam_vecs, subln, ya, yd)


def _pool_kernel(p_ref, w_ref, scale_ref, *rest, seq, aliased):
    o_ref, pad_ref = rest[-2], rest[-1]
    del aliased
    gw = GROUP_WIDTH
    halo = POOL_HALO
    pad_ref[0:halo, :] = jnp.zeros((halo, gw), F32)
    pad_ref[halo + seq:, :] = jnp.zeros((halo, gw), F32)
    pad_ref[halo:halo + seq, :] = p_ref[...]
    chunk = min(seq, 256)
    lane_group = lax.broadcasted_iota(jnp.int32, (1, gw), 1) // (gw // len(POOL_WINDOWS))
    for c0 in range(0, seq, chunk):
        def at(off):
            return pad_ref[halo + c0 + off:halo + c0 + off + chunk, :]
        pos = c0 + lax.broadcasted_iota(jnp.int32, (chunk, 1), 0)
        x = at(0)
        run = x
        mean = None
        lo_done, hi_done = 0, 0
        for g, win in enumerate(POOL_WINDOWS):
            half = win // 2
            for off in range(-half, -lo_done):
                run = run + at(off)
            for off in range(hi_done + 1, half):
                run = run + at(off)
            lo_done, hi_done = half, half - 1
            cnt = (jnp.minimum(pos + half, seq) - jnp.maximum(pos - half, 0)).astype(F32)
            m = run / cnt
            mean = m if mean is None else jnp.where(lane_group == g, m, mean)
        y = _dot((mean - x).astype(BF16), w_ref[...]) * scale_ref[...]
        o_ref[c0:c0 + chunk, :] = y.astype(BF16)


def _pool_call(pool_in, w_bd, scale, n_seq, seq, row_blk0, total_rows, prev=None):
    gw = GROUP_WIDTH
    in_specs = [pl.BlockSpec((seq, gw), lambda b: (row_blk0 + b, 0)), _full((gw, gw)), _full((1, gw))]
    args = [pool_in, w_bd, scale]
    aliases = {}
    if prev is not None:
        in_specs.append(pl.BlockSpec(memory_space=pl.ANY))
        args.append(prev)
        aliases = {3: 0}
    return pl.pallas_call(
        functools.partial(_pool_kernel, seq=seq, aliased=prev is not None),
        grid=(n_seq,),
        in_specs=in_specs,
        out_specs=pl.BlockSpec((seq, gw), lambda b: (row_blk0 + b, 0)),
        out_shape=jax.ShapeDtypeStruct((total_rows, gw), BF16),
        scratch_shapes=[pltpu.VMEM((seq + 2 * POOL_HALO, gw), F32)],
        input_output_aliases=aliases,
        compiler_params=_cparams(1),
        name="pool_mix",
    )(*args)


def _fft_kernel(t_ref, cl_ref, sl_ref, cc_ref, sc_ref, w_ref, *rest, norm):
    o_ref, a_ref, b_ref = rest[-3], rest[-2], rest[-1]

    @pl.when(pl.program_id(1) == 0)
    def _():
        t = t_ref[...]
        a_ref[...] = _dot(t, cc_ref[...]).astype(BF16)
        b_ref[...] = _dot(t, sc_ref[...]).astype(BF16)

    f = (_dot(cl_ref[...], a_ref[...]) - _dot(sl_ref[...], b_ref[...])) * norm
    o_ref[...] = _dot(f.astype(BF16), w_ref[...]).astype(BF16)


def _fft_call(fft_in, cl, sl, cc_bd, sc_bd, w_bd, n_seq, seq, row_blk0, total_rows, prev=None):
    gw = GROUP_WIDTH
    tk = min(FFT_TILE, seq)
    n_k = seq // tk
    in_specs = [pl.BlockSpec((seq, gw), lambda b, k: (row_blk0 + b, 0)),
                pl.BlockSpec((tk, seq), lambda b, k: (k, 0)),
                pl.BlockSpec((tk, seq), lambda b, k: (k, 0)),
                _full((gw, gw)), _full((gw, gw)), _full((gw, gw))]
    args = [fft_in, cl, sl, cc_bd, sc_bd, w_bd]
    aliases = {}
    if prev is not None:
        in_specs.append(pl.BlockSpec(memory_space=pl.ANY))
        args.append(prev)
        aliases = {6: 0}
    norm = 1.0 / math.sqrt(seq * (gw // 4))
    return pl.pallas_call(
        functools.partial(_fft_kernel, norm=norm),
        grid=(n_seq, n_k),
        in_specs=in_specs,
        out_specs=pl.BlockSpec((tk, gw), lambda b, k: ((row_blk0 + b) * n_k + k, 0)),
        out_shape=jax.ShapeDtypeStruct((total_rows, gw), BF16),
        scratch_shapes=[pltpu.VMEM((seq, gw), BF16), pltpu.VMEM((seq, gw), BF16)],
        input_output_aliases=aliases,
        compiler_params=_cparams(2),
        name="fourier_mix",
    )(*args)


def _out_kernel(ya_ref, yd_ref, yb_ref, yf_ref, w_ref, xa_ref, xb_ref, g1_ref, sh2_ref, sc2_ref, gn2_ref,
                rw_ref, rb_ref, x1_ref, h2_ref, lg_ref, *, n_lat_tiles):
    gw = GROUP_WIDTH
    x = jnp.where(pl.program_id(0) < n_lat_tiles, xa_ref[...], xb_ref[...])
    acc = _dot(ya_ref[...], w_ref[0:gw, :])
    acc = acc + _dot(yd_ref[...], w_ref[gw:2 * gw, :])
    acc = acc + _dot(yb_ref[...], w_ref[2 * gw:3 * gw, :])
    acc = acc + _dot(yf_ref[...], w_ref[3 * gw:4 * gw, :])
    x1 = x + g1_ref[0] * acc
    x1_ref[...] = x1
    ms = jnp.mean(x1 * x1, axis=-1, keepdims=True)
    h2 = x1 * lax.rsqrt(ms + NORM_EPS) * gn2_ref[...] * (1.0 + sc2_ref[0]) + sh2_ref[0]
    h_hi = h2.astype(BF16)
    h_lo = (h2 - h_hi.astype(F32)).astype(BF16)
    by_hi = _dot(h_hi, rw_ref[...])
    by_lo = _dot(h_lo, rw_ref[:, :LANES])
    lg_ref[...] = by_hi[:, :LANES] + by_hi[:, LANES:] + by_lo + rb_ref[...]
    tm = x1.shape[0]
    for s in range(SLABS):
        h2_ref[pl.ds(s, tm, stride=SLABS), :] = h2[:, s * LANES:(s + 1) * LANES]


def _out_call(ya, yd, yb, yf, w_out_bf, xa, xb, ctx_blk0, mod3, gn2, rw_pad, rb_pad, rows, n_lat_rows, seq):
    d = D_MODEL
    gw = GROUP_WIDTH
    tm = ROW_TILE
    n_lat_tiles = n_lat_rows // tm
    tiles_per_seq = seq // tm
    n_batch = n_lat_rows // seq

    def mod_row(i):
        return jnp.minimum(i // tiles_per_seq, n_batch)

    mix_spec = pl.BlockSpec((tm, gw), lambda i: (i, 0))
    mod_spec = lambda chunk: pl.BlockSpec((1, 1, d), lambda i: (mod_row(i), 0, chunk))
    return pl.pallas_call(
        functools.partial(_out_kernel, n_lat_tiles=n_lat_tiles),
        grid=(rows // tm,),
        in_specs=[mix_spec, mix_spec, mix_spec, mix_spec, _full((4 * gw, d))]
                 + _stream_specs(tm, d, n_lat_tiles, ctx_blk0) + [
                  mod_spec(2), mod_spec(3), mod_spec(4), _full((1, d)),
                  _full(rw_pad.shape), _full(rb_pad.shape)],
        out_specs=[pl.BlockSpec((tm, d), lambda i: (i, 0)),
                   pl.BlockSpec((tm * SLABS, LANES), lambda i: (i, 0)),
                   pl.BlockSpec((tm, LANES), lambda i: (i, 0))],
        out_shape=[jax.ShapeDtypeStruct((rows, d), F32),
                   jax.ShapeDtypeStruct((rows * SLABS, LANES), F32),
                   jax.ShapeDtypeStruct((rows, LANES), F32)],
        compiler_params=_cparams(1),
        name="out_proj",
    )(ya, yd, yb, yf, w_out_bf, xa, xb, mod3, mod3, mod3, gn2, rw_pad, rb_pad)


def _route_kernel(lg_ref, gate_ref, code_ref, cnt_ref, carry_ref):
    i = pl.program_id(0)

    @pl.when(i == 0)
    def _():
        carry_ref[...] = jnp.zeros_like(carry_ref)

    lg = lg_ref[...]
    tm = lg.shape[0]
    lane = lax.broadcasted_iota(jnp.int32, (tm, LANES), 1)
    vals, idxs = [], []
    onehot = jnp.zeros((tm, LANES), F32)
    for _ in range(TOP_K):
        m = lg.max(axis=-1, keepdims=True)
        idx = jnp.min(jnp.where(lg == m, lane, LANES), axis=-1, keepdims=True)
        sel = lane == idx
        onehot = onehot + sel.astype(F32)
        lg = jnp.where(sel, -jnp.inf, lg)
        vals.append(m)
        idxs.append(idx)
    exps = [jnp.exp(v - vals[0]) for v in vals]
    denom = exps[0] + exps[1] + exps[2] + exps[3]
    r_i = lax.broadcasted_iota(jnp.int32, (tm, tm), 0)
    c_i = lax.broadcasted_iota(jnp.int32, (tm, tm), 1)
    tri = (c_i < r_i).astype(BF16)
    before = _dot(tri, onehot.astype(BF16)) + carry_ref[...]
    col4 = lax.broadcasted_iota(jnp.int32, (tm, TOP_K), 1)
    gate_o = jnp.zeros((tm, TOP_K), F32)
    code_o = jnp.zeros((tm, TOP_K), F32)
    for k in range(TOP_K):
        rk = jnp.sum(jnp.where(lane == idxs[k], before, 0.0), axis=-1, keepdims=True)
        gate_o = jnp.where(col4 == k, exps[k] / denom, gate_o)
        code_o = jnp.where(col4 == k, idxs[k].astype(F32) * float(RANK_STRIDE) + rk, code_o)
    gate_ref[...] = gate_o
    code_ref[...] = code_o.astype(jnp.int32)
    carry_ref[...] = carry_ref[...] + jnp.sum(onehot, axis=0, keepdims=True)
    cnt_ref[...] = carry_ref[...].astype(jnp.int32)


def _route_call(logits):
    n = logits.shape[0]
    tm = ROW_TILE
    k_spec = pl.BlockSpec((tm, TOP_K), lambda i: (i, 0))
    return pl.pallas_call(
        _route_kernel,
        grid=(n // tm,),
        in_specs=[pl.BlockSpec((tm, LANES), lambda i: (i, 0))],
        out_specs=[k_spec, k_spec, _full((1, LANES))],
        out_shape=[jax.ShapeDtypeStruct((n, TOP_K), F32),
                   jax.ShapeDtypeStruct((n, TOP_K), jnp.int32),
                   jax.ShapeDtypeStruct((1, LANES), jnp.int32)],
        scratch_shapes=[pltpu.VMEM((1, LANES), F32)],
        compiler_params=_cparams(1),
        name="route",
    )(logits)


def _row_copy(src_ref, src_row, dst_ref, dst_row, sem):
    return pltpu.make_async_copy(
        src_ref.at[pl.ds(pl.multiple_of(src_row * SLABS, SLABS), SLABS), :],
        dst_ref.at[pl.ds(pl.multiple_of(dst_row * SLABS, SLABS), SLABS), :], sem)


def _pad_copy(zero_ref, hs_ref, start_row, n_rows, sem):
    return pltpu.make_async_copy(
        zero_ref.at[pl.ds(0, n_rows * SLABS), :],
        hs_ref.at[pl.ds(pl.multiple_of(start_row * SLABS, SLABS), n_rows * SLABS), :], sem)


def _dispatch_kernel(fill_ref, pad_ref, dest_hbm, h2_ref, hs_ref, idx0, idx1, zero_ref, sem, isems):
    i = pl.program_id(0)
    n_tiles = pl.num_programs(0)
    n_assign = TOKEN_TILE * TOP_K
    idx_refs = (idx0, idx1)

    def idx_copy(tile, s):
        return pltpu.make_async_copy(dest_hbm.at[pl.ds(tile * n_assign, n_assign)], idx_refs[s],
                                     isems.at[s])

    @pl.when(i == 0)
    def _():
        idx_copy(0, 0).start()
        zero_ref[...] = jnp.zeros_like(zero_ref)
        bits = [1 << b for b in reversed(range(int(math.log2(EXPERT_BLOCK))))]
        for phase in ("start", "wait"):
            def fill(e, carry, phase=phase):
                pos = fill_ref[e]
                pad = pad_ref[e]
                for bit in bits:
                    @pl.when((pad & bit) != 0)
                    def _(pos=pos, bit=bit):
                        cp = _pad_copy(zero_ref, hs_ref, pos, bit, sem)
                        cp.start() if phase == "start" else cp.wait()
                    pos = pos + (pad & bit)
                return carry

            lax.fori_loop(0, N_EXPERTS, fill, 0)

    for s in (0, 1):
        @pl.when(i % 2 == s)
        def _(s=s):
            idx_copy(i, s).wait()

            @pl.when(i + 1 < n_tiles)
            def _():
                idx_copy(i + 1, 1 - s).start()

            def start(t, carry):
                for k in range(TOP_K):
                    _row_copy(h2_ref, t, hs_ref, idx_refs[s][t * TOP_K + k], sem).start(priority=k % 2)
                return carry

            lax.fori_loop(0, TOKEN_TILE, start, 0, unroll=ISSUE_UNROLL)

    for _ in range(TOP_K):
        pltpu.make_async_copy(h2_ref, hs_ref.at[pl.ds(0, TOKEN_TILE * SLABS), :], sem).wait()


def _dispatch_call(fill_start, pad_len, dest_flat, h2_slabs, n_slots):
    n = h2_slabs.shape[0] // SLABS
    tt = TOKEN_TILE
    grid_spec = pltpu.PrefetchScalarGridSpec(
        num_scalar_prefetch=2,
        grid=(n // tt,),
        in_specs=[pl.BlockSpec(memory_space=pl.ANY),
                  pl.BlockSpec((tt * SLABS, LANES), lambda i, *_: (i, 0))],
        out_specs=pl.BlockSpec(memory_space=pl.ANY),
        scratch_shapes=[pltpu.SMEM((tt * TOP_K,), jnp.int32), pltpu.SMEM((tt * TOP_K,), jnp.int32),
                        pltpu.VMEM((EXPERT_BLOCK // 2 * SLABS, LANES), F32),
                        pltpu.SemaphoreType.DMA, pltpu.SemaphoreType.DMA((2,))],
    )
    return pl.pallas_call(
        _dispatch_kernel,
        grid_spec=grid_spec,
        out_shape=jax.ShapeDtypeStruct((n_slots * SLABS, LANES), F32),
        compiler_params=_cparams(1),
        name="dispatch",
    )(fill_start, pad_len, dest_flat, h2_slabs)


GLU_BLOCK = 2 * LANES


def _expert_kernel(be_ref, first_ref, nv_ref, hs_ref, w1_ref, b1_ref, w2_ref, b2_ref, perm_ref,
                   ys_ref, h_ref, act_ref, w1p_ref, w2p_ref):
    del be_ref
    i = pl.program_id(0)
    n_glu = 2 * D_FF // GLU_BLOCK

    @pl.when((first_ref[i] != 0) & (i < nv_ref[0]))
    def _():
        for b in range(n_glu):
            cols = slice(b * GLU_BLOCK, (b + 1) * GLU_BLOCK)
            w1p_ref[:, cols] = _dot(w1_ref[0, :, cols].astype(BF16), perm_ref[...]).astype(BF16)
        w2p_ref[...] = w2_ref[0].astype(BF16)

    @pl.when(i < nv_ref[0])
    def _():
        blk = EXPERT_BLOCK
        for s in range(SLABS):
            h_ref[:, s * LANES:(s + 1) * LANES] = hs_ref[pl.ds(s, blk, stride=SLABS), :].astype(BF16)
        h = h_ref[...]
        for b in range(n_glu):
            cols = slice(b * GLU_BLOCK, (b + 1) * GLU_BLOCK)
            u = _dot(h, w1p_ref[:, cols]) + b1_ref[0, :, cols]
            glu = jnp.minimum(u[:, :LANES], SWIGLU_LIMIT)
            lin = jnp.clip(u[:, LANES:], -SWIGLU_LIMIT, SWIGLU_LIMIT)
            act = glu * jax.nn.sigmoid(SWIGLU_ALPHA * glu) * (lin + 1.0)
            act_ref[:, b * LANES:(b + 1) * LANES] = act.astype(BF16)
        y = _dot(act_ref[...], w2p_ref[...]) + b2_ref[0]
        for s in range(SLABS):
            ys_ref[pl.ds(s, blk, stride=SLABS), :] = y[:, s * LANES:(s + 1) * LANES]


def _expert_call(block_e, first, n_valid, hs, w1, b1p, w2, b2, perm, layer, n_blocks):
    blk = EXPERT_BLOCK
    d = D_MODEL

    def row_blk(i, be, first, nv):
        return (jnp.minimum(i, nv[0] - 1), 0)

    def expert_blk(i, be, first, nv):
        return (be[i], 0, 0)

    def weight_blk(i, be, first, nv):
        return (layer, be[i], 0, 0)

    grid_spec = pltpu.PrefetchScalarGridSpec(
        num_scalar_prefetch=3,
        grid=(n_blocks,),
        in_specs=[pl.BlockSpec((blk * SLABS, LANES), row_blk),
                  pl.BlockSpec((None, 1, d, 2 * D_FF), weight_blk),
                  pl.BlockSpec((1, 1, 2 * D_FF), expert_blk),
                  pl.BlockSpec((None, 1, D_FF, d), weight_blk),
                  pl.BlockSpec((1, 1, d), expert_blk),
                  pl.BlockSpec((GLU_BLOCK, GLU_BLOCK), lambda i, *_: (0, 0))],
        out_specs=pl.BlockSpec((blk * SLABS, LANES), row_blk),
        scratch_shapes=[pltpu.VMEM((blk, d), BF16), pltpu.VMEM((blk, D_FF), BF16),
                        pltpu.VMEM((d, 2 * D_FF), BF16), pltpu.VMEM((D_FF, d), BF16)],
    )
    return pl.pallas_call(
        _expert_kernel,
        grid_spec=grid_spec,
        out_shape=jax.ShapeDtypeStruct(hs.shape, F32),
        compiler_params=_cparams(1, EXPERT_VMEM_LIMIT),
        name="expert_ffn",
    )(block_e, first, n_valid, hs, w1, b1p, w2, b2, perm)


def _combine_kernel(dest_hbm, ys_hbm, gate_ref, x_ref, g2_ref, o_ref,
                    idx0, idx1, buf0, buf1, sems, isems):
    i = pl.program_id(0)
    n_tiles = pl.num_programs(0)
    tt = TOKEN_TILE
    n_assign = tt * TOP_K
    idx_refs, buf_refs = (idx0, idx1), (buf0, buf1)

    def idx_copy(tile, s):
        return pltpu.make_async_copy(dest_hbm.at[pl.ds(tile * n_assign, n_assign)], idx_refs[s],
                                     isems.at[s])

    def issue(s):
        def start(t, carry):
            for k in range(TOP_K):
                _row_copy(ys_hbm, idx_refs[s][t * TOP_K + k], buf_refs[s], k * tt + t,
                          sems.at[s]).start(priority=k % 2)
            return carry
        lax.fori_loop(0, tt, start, 0, unroll=ISSUE_UNROLL)

    @pl.when(i == 0)
    def _():
        idx_copy(0, 0).start()
        idx_copy(0, 0).wait()
        issue(0)

        @pl.when(n_tiles > 1)
        def _():
            idx_copy(1, 1).start()

    for s in (0, 1):
        @pl.when(i % 2 == s)
        def _(s=s):
            @pl.when(i + 1 < n_tiles)
            def _():
                idx_copy(i + 1, 1 - s).wait()
                issue(1 - s)

            pltpu.make_async_copy(ys_hbm.at[pl.ds(0, n_assign * SLABS), :], buf_refs[s], sems.at[s]).wait()

            @pl.when(i + 2 < n_tiles)
            def _():
                idx_copy(i + 2, s).start()

            gate = gate_ref[...]
            g2 = g2_ref[0]
            for sl in range(SLABS):
                cols = slice(sl * LANES, (sl + 1) * LANES)
                y = jnp.zeros((tt, LANES), F32)
                for k in range(TOP_K):
                    y = y + gate[:, k:k + 1] * buf_refs[s][pl.ds(k * tt * SLABS + sl, tt, stride=SLABS), :]
                o_ref[:, cols] = x_ref[:, cols] + g2[:, cols] * y


def _combine_call(dest_flat, ys, gate, x1, mod3, n_lat_rows, seq):
    n, d = x1.shape
    tt = TOKEN_TILE
    tiles_per_seq = seq // tt
    n_batch = n_lat_rows // seq

    def mod_row(i):
        return jnp.minimum(i // tiles_per_seq, n_batch)

    return pl.pallas_call(
        _combine_kernel,
        grid=(n // tt,),
        in_specs=[pl.BlockSpec(memory_space=pl.ANY), pl.BlockSpec(memory_space=pl.ANY),
                  pl.BlockSpec((tt, TOP_K), lambda i: (i, 0)),
                  pl.BlockSpec((tt, d), lambda i: (i, 0)),
                  pl.BlockSpec((1, 1, d), lambda i: (mod_row(i), 0, 5))],
        out_specs=pl.BlockSpec((tt, d), lambda i: (i, 0)),
        out_shape=jax.ShapeDtypeStruct((n, d), F32),
        scratch_shapes=[pltpu.SMEM((tt * TOP_K,), jnp.int32), pltpu.SMEM((tt * TOP_K,), jnp.int32),
                        pltpu.VMEM((TOP_K * tt * SLABS, LANES), F32),
                        pltpu.VMEM((TOP_K * tt * SLABS, LANES), F32),
                        pltpu.SemaphoreType.DMA((2,)), pltpu.SemaphoreType.DMA((2,))],
        compiler_params=_cparams(1),
        name="combine",
    )(dest_flat, ys, gate, x1, mod3)


def _block_diag(blocks):
    g, a, b = blocks.shape
    eye = jnp.eye(g, dtype=blocks.dtype)
    return (eye[:, None, :, None] * blocks[:, :, None, :]).reshape(g * a, g * b)


def _rope_tables(seq, extra_rows):
    half = DIFF_QK_DIM // 2
    inv = ROPE_THETA ** (-jnp.arange(0, half, 2, dtype=F32) / half)
    pos = jnp.arange(seq)
    c = jnp.arange(GROUP_WIDTH)
    dd = c % DIFF_QK_DIM
    axis_pos = jnp.where((dd < half)[None, :], (pos // GRID_W)[:, None], (pos % GRID_W)[:, None])
    ang = axis_pos.astype(F32) * inv[dd % (half // 2)][None, :]
    sign = jnp.where((dd % half) < half // 2, -1.0, 1.0).astype(F32)
    cos_t = jnp.concatenate([jnp.cos(ang), jnp.ones((extra_rows, GROUP_WIDTH), F32)], axis=0)
    sin_t = jnp.concatenate([jnp.sin(ang) * sign[None, :], jnp.zeros((extra_rows, GROUP_WIDTH), F32)], axis=0)
    return cos_t, sin_t


def _dft_tables(n, dtype):
    k = np.arange(n, dtype=np.int64)
    ang = ((k[:, None] * k[None, :]) % n).astype(np.float64) * (2.0 * math.pi / n)
    return jnp.asarray(np.cos(ang), dtype), jnp.asarray(np.sin(ang), dtype)


def _na_bias_tables(rpb, grid_rows):
    w = GRID_W
    col = np.arange(w)
    col_start = np.clip(col - NA_COLS // 2, 0, w - NA_COLS)
    in_win = (col[None, :] >= col_start[:, None]) & (col[None, :] < col_start[:, None] + NA_COLS)
    rel_c = np.clip(col[None, :] - col[:, None], 1 - NA_COLS, NA_COLS - 1) + NA_COLS - 1
    n_tiles = grid_rows // NA_Q_ROWS
    cases = (0, 1, n_tiles - 1)
    rel_r = np.zeros((3, NA_Q_ROWS, NA_KEY_ROWS), np.int32)
    valid = np.zeros((3, NA_Q_ROWS, NA_KEY_ROWS), bool)
    for ci, tile in enumerate(cases):
        r0 = tile * NA_Q_ROWS
        k0 = int(np.clip(r0 - NA_ROWS // 2, 0, grid_rows - NA_KEY_ROWS))
        for j in range(NA_Q_ROWS):
            start = int(np.clip(r0 + j - NA_ROWS // 2, 0, grid_rows - NA_ROWS))
            for i in range(NA_KEY_ROWS):
                kr = k0 + i
                valid[ci, j, i] = start <= kr < start + NA_ROWS
                rel_r[ci, j, i] = np.clip(kr - (r0 + j) + NA_ROWS - 1, 0, 2 * NA_ROWS - 2)
    del rel_c
    edge = w - NA_COLS
    ext = jnp.pad(rpb.astype(F32), ((0, 0), (0, 0), (edge, edge)), mode='edge')
    t = jnp.stack([ext[:, :, w - 1 - q:2 * w - 1 - q] for q in range(w)], axis=2)
    t = jnp.where(in_win[None, None], t, MASK_VALUE)
    masked = jnp.full((rpb.shape[0], w, w), MASK_VALUE, F32)
    return jnp.stack([jnp.concatenate([jnp.concatenate(
        [t[:, rel_r[c, j, i]] if valid[c, j, i] else masked for i in range(NA_KEY_ROWS)], axis=-1)
        for j in range(NA_Q_ROWS)], axis=-2) for c in range(3)], axis=0)


def _moe(h2_slabs, logits, x1, mod3, w1, b1p, w2, b2, perm, layer, n_lat_rows, seq):
    n = x1.shape[0]
    blk = EXPERT_BLOCK
    assert n * TOP_K <= RANK_STRIDE
    gate, code, counts = _route_call(logits)
    counts = counts[0, :N_EXPERTS]
    padded = (counts + blk - 1) // blk * blk
    padded_end = jnp.cumsum(padded)
    padded_start = padded_end - padded
    codes = code.reshape(1, n * TOP_K)
    experts = lax.shift_right_logical(codes, RANK_SHIFT)
    first_slot = jnp.sum(jnp.where(experts == jnp.arange(N_EXPERTS, dtype=jnp.int32)[:, None],
                                   padded_start.astype(jnp.int32)[:, None], 0), axis=0)
    dest = first_slot + (codes[0] & RANK_MASK)
    n_blocks = n * TOP_K // blk + N_EXPERTS
    block_row0 = jnp.arange(n_blocks, dtype=jnp.int32) * blk
    block_e = jnp.minimum(jnp.sum((padded_end[None, :] <= block_row0[:, None]).astype(jnp.int32), axis=1),
                          N_EXPERTS - 1).astype(jnp.int32)
    first = jnp.concatenate([jnp.ones((1,), jnp.int32),
                             (block_e[1:] != block_e[:-1]).astype(jnp.int32)])
    n_valid = (padded_end[-1:] // blk).astype(jnp.int32)
    fill_start = (padded_start + counts).astype(jnp.int32)
    pad_len = (padded - counts).astype(jnp.int32)
    hs = _dispatch_call(fill_start, pad_len, dest, h2_slabs, n_blocks * blk)
    ys = _expert_call(block_e, first, n_valid, hs, w1, b1p, w2, b2, perm, layer, n_blocks)
    return _combine_call(dest, ys, gate, x1, mod3, n_lat_rows, seq)


def kernel(x, c, ctx, c_ctx, w_ada, b_ada, g_norm1, g_norm2, w_in, w_out, na_q_gain, na_k_gain, na_rpb, diff_q_gain, diff_k_gain, diff_lambda_q1, diff_lambda_k1, diff_lambda_q2, diff_lambda_k2, diff_subln, pool_w, pool_scale, fft_w, router_w, router_b, moe_w1, moe_b1, moe_w2, moe_b2):
    n_batch, seq, d = x.shape
    ctx_len = ctx.shape[1]
    depth = w_ada.shape[0]
    gw = GROUP_WIDTH
    assert d == D_MODEL and seq % ROW_TILE == 0 and seq % ctx_len == 0 and ctx_len % TOKEN_TILE == 0
    assert (n_batch * ctx_len) % ROW_TILE == 0 and (seq // GRID_W) >= NA_KEY_ROWS
    n_lat = n_batch * seq
    n_ctx = n_batch * ctx_len

    xa, xb, ctx_blk0 = x.reshape(n_lat, d), ctx.reshape(n_ctx, d), 0
    mod_rows = -(-(n_batch + 1) // SUBLANES) * SUBLANES
    cvec = jnp.zeros((mod_rows, d), F32).at[:n_batch].set(c).at[n_batch].set(c_ctx)
    mod = _ada_call(cvec, w_ada, b_ada)

    cos_t, sin_t = _rope_tables(seq, ROW_TILE)
    ones = lambda w: _block_diag(jnp.ones((gw // w, w, w), BF16))
    g64, g32 = ones(HEAD_DIM), ones(DIFF_QK_DIM)
    cl_lat, sl_lat = _dft_tables(seq, BF16)
    cl_ctx, sl_ctx = _dft_tables(ctx_len, BF16)
    cc, sc = _dft_tables(gw // 4, F32)
    n_grp = fft_w.shape[1]
    cc_bd = _block_diag(jnp.broadcast_to(cc, (n_grp,) + cc.shape)).astype(BF16)
    sc_bd = _block_diag(jnp.broadcast_to(sc, (n_grp,) + sc.shape)).astype(BF16)
    src = np.arange(GLU_BLOCK)
    dst = np.where(src % 2 == 0, src // 2, LANES + src // 2)
    perm_np = np.zeros((GLU_BLOCK, GLU_BLOCK), np.float32)
    perm_np[src, dst] = 1.0
    glu_perm = jnp.asarray(perm_np, BF16)

    tile = lambda v: jnp.tile(v.astype(F32), gw // v.shape[0])

    def in_proj_params(l):
        gains = jnp.stack([tile(na_q_gain[l]) * (HEAD_DIM ** -0.5 * LOG2_E), tile(na_k_gain[l]),
                           tile(diff_q_gain[l]) * (DIFF_QK_DIM ** -0.5 * LOG2_E), tile(diff_k_gain[l])]
                          + [jnp.zeros((gw,), F32)] * 4)
        return (mod[l].reshape(mod_rows, 1, 6 * d), g_norm1[l].reshape(1, d), w_in[l].astype(BF16),
                gains, cos_t, sin_t, g64, g32)

    for l in range(depth):
        ctx_out = l < depth - 1
        lam_init = 0.8 - 0.6 * math.exp(-0.3 * l)
        mod3 = mod[l].reshape(mod_rows, 1, 6 * d)
        qkv, pool_in, fft_in = _in_call(xa, xb, ctx_blk0, n_lat + n_ctx, in_proj_params(l), n_lat, seq)
        lam_vecs = jnp.stack([diff_lambda_q1[l], diff_lambda_k1[l], diff_lambda_q2[l], diff_lambda_k2[l]]
                             + [jnp.zeros_like(diff_lambda_q1[l])] * 4).astype(F32)
        subln = tile(diff_subln[l]).reshape(1, gw)
        bias_tab = _na_bias_tables(na_rpb[l] * LOG2_E, seq // GRID_W)
        pool_bd = _block_diag(pool_w[l]).astype(BF16)
        fftw_bd = _block_diag(fft_w[l]).astype(BF16)
        rw_f32 = jnp.zeros((d, LANES), F32).at[:, :N_EXPERTS].set(router_w[l])
        rw_hi = rw_f32.astype(BF16)
        rw_pad = jnp.concatenate([rw_hi, (rw_f32 - rw_hi.astype(F32)).astype(BF16)], axis=1)
        rb_pad = jnp.full((1, LANES), -jnp.inf, F32).at[0, :N_EXPERTS].set(router_b[l])
        n_e = moe_w1.shape[1]
        b1p = moe_b1[l].reshape(n_e, 2 * D_FF // GLU_BLOCK, LANES, 2).transpose(0, 1, 3, 2)
        b1p = b1p.reshape(n_e, 1, 2 * D_FF)
        b2 = moe_b2[l].reshape(n_e, 1, d)

        rows = n_lat + n_ctx if ctx_out else n_lat
        ya = _na_call(qkv, bias_tab, n_batch, seq, rows, ctx_len)
        yd = _df_call(qkv, lam_vecs, subln, lam_init, n_batch, seq, rows, ctx_len)
        yb = _pool_call(pool_in, pool_bd, pool_scale[l].reshape(1, gw), n_batch, seq, 0, rows)
        yf = _fft_call(fft_in, cl_lat, sl_lat, cc_bd, sc_bd, fftw_bd, n_batch, seq, 0, rows)
        if ctx_out:
            ya, yd = _ctx_attn_call(qkv, lam_vecs, subln, lam_init, ya, yd, n_batch, seq, ctx_len)
            yb = _pool_call(pool_in, pool_bd, pool_scale[l].reshape(1, gw), n_batch, ctx_len,
                            n_lat // ctx_len, rows, prev=yb)
            yf = _fft_call(fft_in, cl_ctx, sl_ctx, cc_bd, sc_bd, fftw_bd, n_batch, ctx_len,
                           n_lat // ctx_len, rows, prev=yf)
        x1, h2_slabs, logits = _out_call(ya, yd, yb, yf, w_out[l].astype(BF16), xa, xb, ctx_blk0, mod3,
                                         g_norm2[l].reshape(1, d), rw_pad, rb_pad, rows, n_lat, seq)
        x2 = _moe(h2_slabs, logits, x1, mod3, moe_w1, b1p, moe_w2, b2, glu_perm, l, n_lat, seq)
        xa, xb, ctx_blk0 = x2, x2, n_lat // ROW_TILE
    return x2[:n_lat].reshape(n_batch, seq, d)
```

```python
import functools
import math

import numpy as np
import jax
import jax.numpy as jnp
from jax import lax
from jax.experimental import pallas as pl
from jax.experimental.pallas import tpu as pltpu

F32 = jnp.float32
BF16 = jnp.bfloat16
HIGHEST = lax.Precision.HIGHEST

D_MODEL = 1024
DEPTH = 2
GRID_W = 64
HEAD_DIM = 64
GROUP_WIDTH = 256
N_HEADS = GROUP_WIDTH // HEAD_DIM
DIFF_QK_DIM = HEAD_DIM // 2
NA_ROWS = 8
NA_COLS = 16
POOL_WINDOWS = (2, 4, 8, 16)
POOL_HALO = max(POOL_WINDOWS) // 2
N_EXPERTS = 32
TOP_K = 4
D_FF = D_MODEL
SWIGLU_ALPHA = 1.702
SWIGLU_LIMIT = 7.0
ROPE_THETA = 10000.0
NORM_EPS = 1e-6
MASK_VALUE = -1e30

LANES = 128
SUBLANES = 8
ROW_TILE = 512
TOKEN_TILE = 512
EXPERT_BLOCK = 512
Q_TILE = 256
DIFF_Q_TILE = 512
LOG2_E = math.log2(math.e)
NA_Q_ROWS = Q_TILE // GRID_W
NA_KEY_ROWS = NA_ROWS + NA_Q_ROWS - 1
FFT_TILE = 512
SLABS = D_MODEL // LANES
ISSUE_UNROLL = 8
RANK_STRIDE = 1 << 17
RANK_MASK = RANK_STRIDE - 1
RANK_SHIFT = 17
VMEM_LIMIT = 48 * 1024 * 1024
EXPERT_VMEM_LIMIT = 56 * 1024 * 1024


def _cparams(n_axes, vmem_limit=VMEM_LIMIT):
    return pltpu.CompilerParams(dimension_semantics=("arbitrary",) * n_axes,
                                vmem_limit_bytes=vmem_limit)


def _dot(a, b):
    return jnp.dot(a, b, preferred_element_type=F32)


def _dot_nt(a, b):
    return lax.dot_general(a, b, (((1,), (1,)), ((), ())), preferred_element_type=F32)


def _full(shape):
    zeros = (0,) * len(shape)
    return pl.BlockSpec(shape, lambda *_: zeros)


def _ada_kernel(c_ref, w_ref, b_ref, o_ref):
    c = c_ref[...]
    act = c * jax.nn.sigmoid(c)
    o_ref[0] = jnp.dot(act, w_ref[0], precision=HIGHEST, preferred_element_type=F32) + b_ref[0]


def _ada_call(cvec, w_ada, b_ada):
    depth, d, n = w_ada.shape
    r = cvec.shape[0]
    tn = 1024
    return pl.pallas_call(
        _ada_kernel,
        grid=(depth, n // tn),
        in_specs=[_full((r, d)),
                  pl.BlockSpec((1, d, tn), lambda l, j: (l, 0, j)),
                  pl.BlockSpec((1, 1, tn), lambda l, j: (l, 0, j))],
        out_specs=pl.BlockSpec((1, r, tn), lambda l, j: (l, 0, j)),
        out_shape=jax.ShapeDtypeStruct((depth, r, n), F32),
        compiler_params=_cparams(2),
        name="ada_mod",
    )(cvec, w_ada, b_ada.reshape(depth, 1, n))


def _seg_mean_sq(p, gmat_ref, width):
    sq = p * p
    hi = sq.astype(BF16)
    lo = (sq - hi.astype(F32)).astype(BF16)
    g = gmat_ref[...]
    return (_dot(hi, g) + _dot(lo, g)) * (1.0 / width)


def _stream_specs(tm, d, n_lat_tiles, ctx_blk0):
    return [pl.BlockSpec((tm, d), lambda i: (jnp.minimum(i, n_lat_tiles - 1), 0)),
            pl.BlockSpec((tm, d), lambda i: (ctx_blk0 + jnp.maximum(i - n_lat_tiles, 0), 0))]


def _in_kernel(xa_ref, xb_ref, *rest, n_lat_tiles):
    x = jnp.where(pl.program_id(0) < n_lat_tiles, xa_ref[...], xb_ref[...])
    _in_body(x, *rest)


def _in_body(x, sh_ref, sc_ref, g_ref, w_ref, gains_ref, cos_ref, sin_ref, g64_ref, g32_ref,
             qkv_ref, pool_ref, fft_ref):
    ms = jnp.mean(x * x, axis=-1, keepdims=True)
    y = x * lax.rsqrt(ms + NORM_EPS) * g_ref[...]
    h = (y * (1.0 + sc_ref[0]) + sh_ref[0]).astype(BF16)
    gw = GROUP_WIDTH

    def proj(g):
        return _dot(h, w_ref[:, g * gw:(g + 1) * gw])

    def put(g, val):
        qkv_ref[:, g * gw:(g + 1) * gw] = val.astype(BF16)

    def normed(p, gmat_ref, width, row):
        return p * lax.rsqrt(_seg_mean_sq(p, gmat_ref, width) + NORM_EPS) * gains_ref[row:row + 1, :]

    lane = lax.broadcasted_iota(jnp.int32, (1, gw), 1)
    first_half = (lane % 16) < 8

    def rope(p):
        rot = jnp.where(first_half, pltpu.roll(p, gw - 8, 1), pltpu.roll(p, 8, 1))
        return p * cos_ref[...] + rot * sin_ref[...]

    put(0, normed(proj(0), g64_ref, HEAD_DIM, 0))
    put(1, normed(proj(1), g64_ref, HEAD_DIM, 1))
    put(2, proj(2))
    put(3, rope(normed(proj(3), g32_ref, DIFF_QK_DIM, 2)))
    put(4, rope(normed(proj(4), g32_ref, DIFF_QK_DIM, 3)))
    put(5, proj(5))
    pool_ref[...] = proj(6)
    fft_ref[...] = proj(7).astype(BF16)


def _in_proj_io(tm, rows, params, n_lat_rows, seq):
    mod3, g1, w_in_bf, gains, cos_t, sin_t, g64, g32 = params
    d = D_MODEL
    gw = GROUP_WIDTH
    n_lat_tiles = n_lat_rows // tm
    tiles_per_seq = seq // tm
    n_batch = n_lat_rows // seq

    def mod_row(i):
        return jnp.minimum(i // tiles_per_seq, n_batch)

    def tab_row(i):
        return jnp.where(i < n_lat_tiles, i % tiles_per_seq, tiles_per_seq)

    full = lambda shape: pl.BlockSpec(shape, lambda i, *_: (0,) * len(shape))
    in_specs = [pl.BlockSpec((1, 1, d), lambda i, *_: (mod_row(i), 0, 0)),
                pl.BlockSpec((1, 1, d), lambda i, *_: (mod_row(i), 0, 1)),
                full((1, d)), full(w_in_bf.shape), full(gains.shape),
                pl.BlockSpec((tm, gw), lambda i, *_: (tab_row(i), 0)),
                pl.BlockSpec((tm, gw), lambda i, *_: (tab_row(i), 0)),
                full((gw, gw)), full((gw, gw))]
    args = [mod3, mod3, g1, w_in_bf, gains, cos_t, sin_t, g64, g32]
    out_specs = [pl.BlockSpec((tm, 6 * gw), lambda i, *_: (i, 0)),
                 pl.BlockSpec((tm, gw), lambda i, *_: (i, 0)),
                 pl.BlockSpec((tm, gw), lambda i, *_: (i, 0))]
    out_shape = [jax.ShapeDtypeStruct((rows, 6 * gw), BF16),
                 jax.ShapeDtypeStruct((rows, gw), F32),
                 jax.ShapeDtypeStruct((rows, gw), BF16)]
    return in_specs, args, out_specs, out_shape


def _in_call(xa, xb, ctx_blk0, rows, params, n_lat_rows, seq):
    tm = ROW_TILE
    n_lat_tiles = n_lat_rows // tm
    in_specs, args, out_specs, out_shape = _in_proj_io(tm, rows, params, n_lat_rows, seq)
    return pl.pallas_call(
        functools.partial(_in_kernel, n_lat_tiles=n_lat_tiles),
        grid=(rows // tm,),
        in_specs=_stream_specs(tm, D_MODEL, n_lat_tiles, ctx_blk0) + in_specs,
        out_specs=out_specs,
        out_shape=out_shape,
        compiler_params=_cparams(1),
        name="in_proj",
    )(xa, xb, *args)


def _lane_mask(width, start, size):
    lane = lax.broadcasted_iota(jnp.int32, (1, width), 1)
    return (lane >= start) & (lane < start + size)


def _softmax_pv(qm, keys, biases, vals, sum_lane=None):
    scores = []
    for k, bias in zip(keys, biases):
        s = _dot_nt(qm, k)
        scores.append(s if bias is None else s + bias)
    m = scores[0].max(axis=-1, keepdims=True)
    for s in scores[1:]:
        m = jnp.maximum(m, s.max(axis=-1, keepdims=True))
    o = None
    l = None
    for s, v in zip(scores, vals):
        e = jnp.exp2(s - m)
        part = _dot(e.astype(BF16), v)
        o = part if o is None else o + part
        if sum_lane is None:
            ls = e.sum(axis=-1, keepdims=True)
            l = ls if l is None else l + ls
    if sum_lane is not None:
        lane = lax.broadcasted_iota(jnp.int32, (1, o.shape[1]), 1)
        l = jnp.sum(jnp.where(lane == sum_lane, o, 0.0), axis=-1, keepdims=True)
    return o, l


def _na_heads(q, keys, bias_fn, vals):
    gw = GROUP_WIDTH
    acc = jnp.zeros((q.shape[0], gw), F32)
    for h in range(N_HEADS):
        mask = _lane_mask(gw, h * HEAD_DIM, HEAD_DIM)
        qm = jnp.where(mask, q, jnp.zeros_like(q))
        o, l = _softmax_pv(qm, keys, bias_fn(h), vals)
        acc = acc + jnp.where(mask, o / l, 0.0)
    return acc


def _lambda(lam_ref, lam_init):
    lv = lam_ref[...]
    d1 = jnp.sum(lv[0:1, :] * lv[1:2, :], axis=-1, keepdims=True)
    d2 = jnp.sum(lv[2:3, :] * lv[3:4, :], axis=-1, keepdims=True)
    return jnp.exp(d1) - jnp.exp(d2) + lam_init


def _diff_heads(q, keys, vals, lam, subln, lam_init, acc_ref):
    gw = GROUP_WIDTH
    lane = lax.broadcasted_iota(jnp.int32, (1, gw), 1)
    acc_ref[...] = jnp.zeros_like(acc_ref)

    def scores_of(h, part):
        plo = h * HEAD_DIM + part * DIFF_QK_DIM
        qm = jnp.where((lane >= plo) & (lane < plo + DIFF_QK_DIM), q, jnp.zeros_like(q))
        scores = [_dot_nt(qm, k) for k in keys]
        m = scores[0].max(axis=-1, keepdims=True)
        for s in scores[1:]:
            m = jnp.maximum(m, s.max(axis=-1, keepdims=True))
        return scores, m

    def attend(h, scores, m):
        lo = h * HEAD_DIM
        hm = (lane >= lo) & (lane < lo + HEAD_DIM)
        o = None
        for s, v in zip(scores, vals):
            part = _dot(jnp.exp2(s - m).astype(BF16), jnp.where(hm, v, jnp.ones_like(v)))
            o = part if o is None else o + part
        l = jnp.sum(jnp.where(lane == (lo + HEAD_DIM) % gw, o, 0.0), axis=-1, keepdims=True)
        return o / l

    combos = [(h, part) for h in range(N_HEADS) for part in range(2)]
    pending = [scores_of(*combos[0])]
    outs = []
    for c, (h, part) in enumerate(combos):
        if c + 1 < len(combos):
            pending.append(scores_of(*combos[c + 1]))
        outs.append(attend(h, *pending.pop(0)))
        if part == 1:
            lo = h * HEAD_DIM
            hm = (lane >= lo) & (lane < lo + HEAD_DIM)
            a = jnp.where(hm, outs[-2] - lam * outs[-1], 0.0)
            ms = jnp.sum(a * a, axis=-1, keepdims=True) * (1.0 / HEAD_DIM)
            acc_ref[...] += a * lax.rsqrt(ms + NORM_EPS)
    return acc_ref[...] * subln * (1.0 - lam_init)


def _na_kernel(q_ref, k_ref, v_ref, kc_ref, vc_ref, bias_ref, o_ref, *, grid_rows):
    j = pl.program_id(1)
    key_row0 = jnp.clip(j * NA_Q_ROWS - NA_ROWS // 2, 0, grid_rows - NA_KEY_ROWS)
    ks = pl.multiple_of(key_row0 * GRID_W, GRID_W)
    n_win = NA_KEY_ROWS * GRID_W
    kwin = k_ref[pl.ds(ks, n_win), :]
    vwin = v_ref[pl.ds(ks, n_win), :]
    acc = _na_heads(q_ref[...], [kwin, kc_ref[...]], lambda h: [bias_ref[0, h], None],
                    [vwin, vc_ref[...]])
    o_ref[...] = acc.astype(BF16)


def _na_call(qkv, bias_tab, n_batch, seq, total_rows, ctx_len):
    gw = GROUP_WIDTH
    qt = Q_TILE
    n_q = seq // qt
    grid_rows = seq // GRID_W
    ctx_blk0 = n_batch * seq // ctx_len

    def bias_case(j):
        return jnp.where(j == 0, 0, jnp.where(j == n_q - 1, 2, 1))

    return pl.pallas_call(
        functools.partial(_na_kernel, grid_rows=grid_rows),
        grid=(n_batch, n_q),
        in_specs=[pl.BlockSpec((qt, gw), lambda b, j: (b * n_q + j, 0)),
                  pl.BlockSpec((seq, gw), lambda b, j: (b, 1)),
                  pl.BlockSpec((seq, gw), lambda b, j: (b, 2)),
                  pl.BlockSpec((ctx_len, gw), lambda b, j: (ctx_blk0 + b, 1)),
                  pl.BlockSpec((ctx_len, gw), lambda b, j: (ctx_blk0 + b, 2)),
                  pl.BlockSpec((1,) + bias_tab.shape[1:], lambda b, j: (bias_case(j), 0, 0, 0))],
        out_specs=pl.BlockSpec((qt, gw), lambda b, j: (b * n_q + j, 0)),
        out_shape=jax.ShapeDtypeStruct((total_rows, gw), BF16),
        compiler_params=_cparams(2),
        name="na_attn",
    )(qkv, qkv, qkv, qkv, qkv, bias_tab)


def _df_kernel(q_ref, k_ref, v_ref, kc_ref, vc_ref, lam_ref, subln_ref, o_ref, acc_ref, *, lam_init):
    lam = _lambda(lam_ref, lam_init)
    acc = _diff_heads(q_ref[...], [kc_ref[...], k_ref[...]], [vc_ref[...], v_ref[...]], lam,
                      subln_ref[...], lam_init, acc_ref)
    o_ref[...] = acc.astype(BF16)


def _df_call(qkv, lam_vecs, subln, lam_init, n_batch, seq, total_rows, ctx_len):
    gw = GROUP_WIDTH
    qt = DIFF_Q_TILE
    n_q = seq // qt
    ctx_blk0 = n_batch * seq // ctx_len
    return pl.pallas_call(
        functools.partial(_df_kernel, lam_init=lam_init),
        grid=(n_batch, n_q),
        in_specs=[pl.BlockSpec((qt, gw), lambda b, j: (b * n_q + j, 3)),
                  pl.BlockSpec((seq, gw), lambda b, j: (b, 4)),
                  pl.BlockSpec((seq, gw), lambda b, j: (b, 5)),
                  pl.BlockSpec((ctx_len, gw), lambda b, j: (ctx_blk0 + b, 4)),
                  pl.BlockSpec((ctx_len, gw), lambda b, j: (ctx_blk0 + b, 5)),
                  _full(lam_vecs.shape),
                  _full(subln.shape)],
        out_specs=pl.BlockSpec((qt, gw), lambda b, j: (b * n_q + j, 0)),
        out_shape=jax.ShapeDtypeStruct((total_rows, gw), BF16),
        scratch_shapes=[pltpu.VMEM((qt, gw), F32)],
        compiler_params=_cparams(2),
        name="diff_attn",
    )(qkv, qkv, qkv, qkv, qkv, lam_vecs, subln)


def _ctx_attn_kernel(qkv_ref, lam_ref, subln_ref, ya_in, yd_in, ya_ref, yd_ref, acc_ref, *, lam_init):
    del ya_in, yd_in
    gw = GROUP_WIDTH
    col = lambda g: qkv_ref[:, g * gw:(g + 1) * gw]
    ya = _na_heads(col(0), [col(1)], lambda h: [None], [col(2)])
    ya_ref[...] = ya.astype(BF16)
    lam = _lambda(lam_ref, lam_init)
    yd = _diff_heads(col(3), [col(4)], [col(5)], lam, subln_ref[...], lam_init, acc_ref)
    yd_ref[...] = yd.astype(BF16)


def _ctx_attn_call(qkv, lam_vecs, subln, lam_init, ya, yd, n_batch, seq, ctx_len):
    gw = GROUP_WIDTH
    ctx_blk0 = n_batch * seq // ctx_len
    any_spec = pl.BlockSpec(memory_space=pl.ANY)
    out_spec = pl.BlockSpec((ctx_len, gw), lambda b: (ctx_blk0 + b, 0))
    return pl.pallas_call(
        functools.partial(_ctx_attn_kernel, lam_init=lam_init),
        grid=(n_batch,),
        in_specs=[pl.BlockSpec((ctx_len, 6 * gw), lambda b: (ctx_blk0 + b, 0)),
                  _full(lam_vecs.shape), _full(subln.shape), any_spec, any_spec],
        out_specs=[out_spec, out_spec],
        out_shape=[jax.ShapeDtypeStruct(ya.shape, BF16), jax.ShapeDtypeStruct(yd.shape, BF16)],
        input_output_aliases={3: 0, 4: 1},
        scratch_shapes=[pltpu.VMEM((ctx_len, gw), F32)],
        compiler_params=_cparams(1),
        name="ctx_attn",
    )(qkv, lam_vecs, subln, ya, yd)


def _pool_kernel(p_ref, w_ref, scale_ref, *rest, seq, aliased):
    o_ref, pad_ref = rest[-2], rest[-1]
    del aliased
    gw = GROUP_WIDTH
    halo = POOL_HALO
    pad_ref[0:halo, :] = jnp.zeros((halo, gw), F32)
    pad_ref[halo + seq:, :] = jnp.zeros((halo, gw), F32)
    pad_ref[halo:halo + seq, :] = p_ref[...]
    chunk = min(seq, 256)
    lane_group = lax.broadcasted_iota(jnp.int32, (1, gw), 1) // (gw // len(POOL_WINDOWS))
    for c0 in range(0, seq, chunk):
        def at(off):
            return pad_ref[halo + c0 + off:halo + c0 + off + chunk, :]
        pos = c0 + lax.broadcasted_iota(jnp.int32, (chunk, 1), 0)
        x = at(0)
        run = x
        mean = None
        lo_done, hi_done = 0, 0
        for g, win in enumerate(POOL_WINDOWS):
            half = win // 2
            for off in range(-half, -lo_done):
                run = run + at(off)
            for off in range(hi_done + 1, half):
                run = run + at(off)
            lo_done, hi_done = half, half - 1
            cnt = (jnp.minimum(pos + half, seq) - jnp.maximum(pos - half, 0)).astype(F32)
            m = run / cnt
            mean = m if mean is None else jnp.where(lane_group == g, m, mean)
        y = _dot((mean - x).astype(BF16), w_ref[...]) * scale_ref[...]
        o_ref[c0:c0 + chunk, :] = y.astype(BF16)


def _pool_call(pool_in, w_bd, scale, n_seq, seq, row_blk0, total_rows, prev=None):
    gw = GROUP_WIDTH
    in_specs = [pl.BlockSpec((seq, gw), lambda b: (row_blk0 + b, 0)), _full((gw, gw)), _full((1, gw))]
    args = [pool_in, w_bd, scale]
    aliases = {}
    if prev is not None:
        in_specs.append(pl.BlockSpec(memory_space=pl.ANY))
        args.append(prev)
        aliases = {3: 0}
    return pl.pallas_call(
        functools.partial(_pool_kernel, seq=seq, aliased=prev is not None),
        grid=(n_seq,),
        in_specs=in_specs,
        out_specs=pl.BlockSpec((seq, gw), lambda b: (row_blk0 + b, 0)),
        out_shape=jax.ShapeDtypeStruct((total_rows, gw), BF16),
        scratch_shapes=[pltpu.VMEM((seq + 2 * POOL_HALO, gw), F32)],
        input_output_aliases=aliases,
        compiler_params=_cparams(1),
        name="pool_mix",
    )(*args)


def _fft_kernel(t_ref, cl_ref, sl_ref, cc_ref, sc_ref, w_ref, *rest, norm):
    o_ref, a_ref, b_ref = rest[-3], rest[-2], rest[-1]

    @pl.when(pl.program_id(1) == 0)
    def _():
        t = t_ref[...]
        a_ref[...] = _dot(t, cc_ref[...]).astype(BF16)
        b_ref[...] = _dot(t, sc_ref[...]).astype(BF16)

    f = (_dot(cl_ref[...], a_ref[...]) - _dot(sl_ref[...], b_ref[...])) * norm
    o_ref[...] = _dot(f.astype(BF16), w_ref[...]).astype(BF16)


def _fft_call(fft_in, cl, sl, cc_bd, sc_bd, w_bd, n_seq, seq, row_blk0, total_rows, prev=None):
    gw = GROUP_WIDTH
    tk = min(FFT_TILE, seq)
    n_k = seq // tk
    in_specs = [pl.BlockSpec((seq, gw), lambda b, k: (row_blk0 + b, 0)),
                pl.BlockSpec((tk, seq), lambda b, k: (k, 0)),
                pl.BlockSpec((tk, seq), lambda b, k: (k, 0)),
                _full((gw, gw)), _full((gw, gw)), _full((gw, gw))]
    args = [fft_in, cl, sl, cc_bd, sc_bd, w_bd]
    aliases = {}
    if prev is not None:
        in_specs.append(pl.BlockSpec(memory_space=pl.ANY))
        args.append(prev)
        aliases = {6: 0}
    norm = 1.0 / math.sqrt(seq * (gw // 4))
    return pl.pallas_call(
        functools.partial(_fft_kernel, norm=norm),
        grid=(n_seq, n_k),
        in_specs=in_specs,
        out_specs=pl.BlockSpec((tk, gw), lambda b, k: ((row_blk0 + b) * n_k + k, 0)),
        out_shape=jax.ShapeDtypeStruct((total_rows, gw), BF16),
        scratch_shapes=[pltpu.VMEM((seq, gw), BF16), pltpu.VMEM((seq, gw), BF16)],
        input_output_aliases=aliases,
        compiler_params=_cparams(2),
        name="fourier_mix",
    )(*args)


def _out_kernel(ya_ref, yd_ref, yb_ref, yf_ref, w_ref, xa_ref, xb_ref, g1_ref, sh2_ref, sc2_ref, gn2_ref,
                rw_ref, rb_ref, x1_ref, h2_ref, lg_ref, *, n_lat_tiles):
    gw = GROUP_WIDTH
    x = jnp.where(pl.program_id(0) < n_lat_tiles, xa_ref[...], xb_ref[...])
    acc = _dot(ya_ref[...], w_ref[0:gw, :])
    acc = acc + _dot(yd_ref[...], w_ref[gw:2 * gw, :])
    acc = acc + _dot(yb_ref[...], w_ref[2 * gw:3 * gw, :])
    acc = acc + _dot(yf_ref[...], w_ref[3 * gw:4 * gw, :])
    x1 = x + g1_ref[0] * acc
    x1_ref[...] = x1
    ms = jnp.mean(x1 * x1, axis=-1, keepdims=True)
    h2 = x1 * lax.rsqrt(ms + NORM_EPS) * gn2_ref[...] * (1.0 + sc2_ref[0]) + sh2_ref[0]
    h_hi = h2.astype(BF16)
    h_lo = (h2 - h_hi.astype(F32)).astype(BF16)
    by_hi = _dot(h_hi, rw_ref[...])
    by_lo = _dot(h_lo, rw_ref[:, :LANES])
    lg_ref[...] = by_hi[:, :LANES] + by_hi[:, LANES:] + by_lo + rb_ref[...]
    tm = x1.shape[0]
    for s in range(SLABS):
        h2_ref[pl.ds(s, tm, stride=SLABS), :] = h2[:, s * LANES:(s + 1) * LANES]


def _out_call(ya, yd, yb, yf, w_out_bf, xa, xb, ctx_blk0, mod3, gn2, rw_pad, rb_pad, rows, n_lat_rows, seq):
    d = D_MODEL
    gw = GROUP_WIDTH
    tm = ROW_TILE
    n_lat_tiles = n_lat_rows // tm
    tiles_per_seq = seq // tm
    n_batch = n_lat_rows // seq

    def mod_row(i):
        return jnp.minimum(i // tiles_per_seq, n_batch)

    mix_spec = pl.BlockSpec((tm, gw), lambda i: (i, 0))
    mod_spec = lambda chunk: pl.BlockSpec((1, 1, d), lambda i: (mod_row(i), 0, chunk))
    return pl.pallas_call(
        functools.partial(_out_kernel, n_lat_tiles=n_lat_tiles),
        grid=(rows // tm,),
        in_specs=[mix_spec, mix_spec, mix_spec, mix_spec, _full((4 * gw, d))]
                 + _stream_specs(tm, d, n_lat_tiles, ctx_blk0) + [
                  mod_spec(2), mod_spec(3), mod_spec(4), _full((1, d)),
                  _full(rw_pad.shape), _full(rb_pad.shape)],
        out_specs=[pl.BlockSpec((tm, d), lambda i: (i, 0)),
                   pl.BlockSpec((tm * SLABS, LANES), lambda i: (i, 0)),
                   pl.BlockSpec((tm, LANES), lambda i: (i, 0))],
        out_shape=[jax.ShapeDtypeStruct((rows, d), F32),
                   jax.ShapeDtypeStruct((rows * SLABS, LANES), F32),
                   jax.ShapeDtypeStruct((rows, LANES), F32)],
        compiler_params=_cparams(1),
        name="out_proj",
    )(ya, yd, yb, yf, w_out_bf, xa, xb, mod3, mod3, mod3, gn2, rw_pad, rb_pad)


def _route_kernel(lg_ref, gate_ref, code_ref, cnt_ref, carry_ref):
    i = pl.program_id(0)

    @pl.when(i == 0)
    def _():
        carry_ref[...] = jnp.zeros_like(carry_ref)

    lg = lg_ref[...]
    tm = lg.shape[0]
    lane = lax.broadcasted_iota(jnp.int32, (tm, LANES), 1)
    vals, idxs = [], []
    onehot = jnp.zeros((tm, LANES), F32)
    for _ in range(TOP_K):
        m = lg.max(axis=-1, keepdims=True)
        idx = jnp.min(jnp.where(lg == m, lane, LANES), axis=-1, keepdims=True)
        sel = lane == idx
        onehot = onehot + sel.astype(F32)
        lg = jnp.where(sel, -jnp.inf, lg)
        vals.append(m)
        idxs.append(idx)
    exps = [jnp.exp(v - vals[0]) for v in vals]
    denom = exps[0] + exps[1] + exps[2] + exps[3]
    r_i = lax.broadcasted_iota(jnp.int32, (tm, tm), 0)
    c_i = lax.broadcasted_iota(jnp.int32, (tm, tm), 1)
    tri = (c_i < r_i).astype(BF16)
    before = _dot(tri, onehot.astype(BF16)) + carry_ref[...]
    col4 = lax.broadcasted_iota(jnp.int32, (tm, TOP_K), 1)
    gate_o = jnp.zeros((tm, TOP_K), F32)
    code_o = jnp.zeros((tm, TOP_K), F32)
    for k in range(TOP_K):
        rk = jnp.sum(jnp.where(lane == idxs[k], before, 0.0), axis=-1, keepdims=True)
        gate_o = jnp.where(col4 == k, exps[k] / denom, gate_o)
        code_o = jnp.where(col4 == k, idxs[k].astype(F32) * float(RANK_STRIDE) + rk, code_o)
    gate_ref[...] = gate_o
    code_ref[...] = code_o.astype(jnp.int32)
    carry_ref[...] = carry_ref[...] + jnp.sum(onehot, axis=0, keepdims=True)
    cnt_ref[...] = carry_ref[...].astype(jnp.int32)


def _route_call(logits):
    n = logits.shape[0]
    tm = ROW_TILE
    k_spec = pl.BlockSpec((tm, TOP_K), lambda i: (i, 0))
    return pl.pallas_call(
        _route_kernel,
        grid=(n // tm,),
        in_specs=[pl.BlockSpec((tm, LANES), lambda i: (i, 0))],
        out_specs=[k_spec, k_spec, _full((1, LANES))],
        out_shape=[jax.ShapeDtypeStruct((n, TOP_K), F32),
                   jax.ShapeDtypeStruct((n, TOP_K), jnp.int32),
                   jax.ShapeDtypeStruct((1, LANES), jnp.int32)],
        scratch_shapes=[pltpu.VMEM((1, LANES), F32)],
        compiler_params=_cparams(1),
        name="route",
    )(logits)


def _row_copy(src_ref, src_row, dst_ref, dst_row, sem):
    return pltpu.make_async_copy(
        src_ref.at[pl.ds(pl.multiple_of(src_row * SLABS, SLABS), SLABS), :],
        dst_ref.at[pl.ds(pl.multiple_of(dst_row * SLABS, SLABS), SLABS), :], sem)


def _pad_copy(zero_ref, hs_ref, start_row, n_rows, sem):
    return pltpu.make_async_copy(
        zero_ref.at[pl.ds(0, n_rows * SLABS), :],
        hs_ref.at[pl.ds(pl.multiple_of(start_row * SLABS, SLABS), n_rows * SLABS), :], sem)


def _dispatch_kernel(fill_ref, pad_ref, dest_hbm, h2_hbm, hs_ref,
                     idx0, idx1, idx2, rows0, rows1, rows2, zero_ref, zsem, isems, lsems, ssems):
    i = pl.program_id(0)
    n_tiles = pl.num_programs(0)
    tt = TOKEN_TILE
    n_assign = tt * TOP_K
    idx_refs, row_refs = (idx0, idx1, idx2), (rows0, rows1, rows2)

    def idx_copy(tile, s):
        return pltpu.make_async_copy(dest_hbm.at[pl.ds(tile * n_assign, n_assign)], idx_refs[s],
                                     isems.at[s])

    def row_load(tile, s):
        return pltpu.make_async_copy(h2_hbm.at[pl.ds(tile * (tt * SLABS), tt * SLABS), :], row_refs[s],
                                     lsems.at[s])

    def wait_sent(s):
        for _ in range(TOP_K):
            pltpu.make_async_copy(row_refs[s], hs_ref.at[pl.ds(0, tt * SLABS), :], ssems.at[s]).wait()

    @pl.when(i == 0)
    def _():
        idx_copy(0, 0).start()
        row_load(0, 0).start()
        zero_ref[...] = jnp.zeros_like(zero_ref)
        bits = [1 << b for b in reversed(range(int(math.log2(EXPERT_BLOCK))))]
        for phase in ("start", "wait"):
            def fill(e, carry, phase=phase):
                pos = fill_ref[e]
                pad = pad_ref[e]
                for bit in bits:
                    @pl.when((pad & bit) != 0)
                    def _(pos=pos, bit=bit):
                        cp = _pad_copy(zero_ref, hs_ref, pos, bit, zsem)
                        cp.start() if phase == "start" else cp.wait()
                    pos = pos + (pad & bit)
                return carry

            lax.fori_loop(0, N_EXPERTS, fill, 0)

    for s in range(3):
        @pl.when(i % 3 == s)
        def _(s=s):
            nxt, prev = (s + 1) % 3, (s + 2) % 3

            @pl.when(i >= 2)
            def _():
                wait_sent(nxt)

            @pl.when(i + 1 < n_tiles)
            def _():
                idx_copy(i + 1, nxt).start()
                row_load(i + 1, nxt).start()

            idx_copy(i, s).wait()
            row_load(i, s).wait()

            def start(t, carry):
                for k in range(TOP_K):
                    _row_copy(row_refs[s], t, hs_ref, idx_refs[s][t * TOP_K + k],
                              ssems.at[s]).start(priority=k % 2)
                return carry

            lax.fori_loop(0, tt, start, 0, unroll=ISSUE_UNROLL)

            @pl.when(i == n_tiles - 1)
            def _():
                @pl.when(i >= 1)
                def _():
                    wait_sent(prev)
                wait_sent(s)


def _dispatch_call(fill_start, pad_len, dest_flat, h2_slabs, n_slots):
    n = h2_slabs.shape[0] // SLABS
    tt = TOKEN_TILE
    any_spec = pl.BlockSpec(memory_space=pl.ANY)
    grid_spec = pltpu.PrefetchScalarGridSpec(
        num_scalar_prefetch=2,
        grid=(n // tt,),
        in_specs=[any_spec, any_spec],
        out_specs=any_spec,
        scratch_shapes=[pltpu.SMEM((tt * TOP_K,), jnp.int32)] * 3
                       + [pltpu.VMEM((tt * SLABS, LANES), F32)] * 3
                       + [pltpu.VMEM((EXPERT_BLOCK // 2 * SLABS, LANES), F32),
                          pltpu.SemaphoreType.DMA, pltpu.SemaphoreType.DMA((3,)),
                          pltpu.SemaphoreType.DMA((3,)), pltpu.SemaphoreType.DMA((3,))],
    )
    return pl.pallas_call(
        _dispatch_kernel,
        grid_spec=grid_spec,
        out_shape=jax.ShapeDtypeStruct((n_slots * SLABS, LANES), F32),
        compiler_params=_cparams(1),
        name="dispatch",
    )(fill_start, pad_len, dest_flat, h2_slabs)


GLU_BLOCK = 2 * LANES


def _expert_kernel(be_ref, first_ref, nv_ref, hs_ref, w1_ref, b1_ref, w2_ref, b2_ref, perm_ref,
                   ys_ref, h_ref, act_ref, w1p_ref, w2p_ref):
    del be_ref
    i = pl.program_id(0)
    n_glu = 2 * D_FF // GLU_BLOCK

    @pl.when((first_ref[i] != 0) & (i < nv_ref[0]))
    def _():
        for b in range(n_glu):
            cols = slice(b * GLU_BLOCK, (b + 1) * GLU_BLOCK)
            w1p_ref[:, cols] = _dot(w1_ref[0, :, cols].astype(BF16), perm_ref[...]).astype(BF16)
        w2p_ref[...] = w2_ref[0].astype(BF16)

    @pl.when(i < nv_ref[0])
    def _():
        blk = EXPERT_BLOCK
        for s in range(SLABS):
            h_ref[:, s * LANES:(s + 1) * LANES] = hs_ref[pl.ds(s, blk, stride=SLABS), :].astype(BF16)
        h = h_ref[...]
        for b in range(n_glu):
            cols = slice(b * GLU_BLOCK, (b + 1) * GLU_BLOCK)
            u = _dot(h, w1p_ref[:, cols]) + b1_ref[0, :, cols]
            glu = jnp.minimum(u[:, :LANES], SWIGLU_LIMIT)
            lin = jnp.clip(u[:, LANES:], -SWIGLU_LIMIT, SWIGLU_LIMIT)
            act = glu * jax.nn.sigmoid(SWIGLU_ALPHA * glu) * (lin + 1.0)
            act_ref[:, b * LANES:(b + 1) * LANES] = act.astype(BF16)
        y = _dot(act_ref[...], w2p_ref[...]) + b2_ref[0]
        for s in range(SLABS):
            ys_ref[pl.ds(s, blk, stride=SLABS), :] = y[:, s * LANES:(s + 1) * LANES]


def _expert_call(block_e, first, n_valid, hs, w1, b1p, w2, b2, perm, layer, n_blocks):
    blk = EXPERT_BLOCK
    d = D_MODEL

    def row_blk(i, be, first, nv):
        return (jnp.minimum(i, nv[0] - 1), 0)

    def expert_blk(i, be, first, nv):
        return (be[i], 0, 0)

    def weight_blk(i, be, first, nv):
        return (layer, be[i], 0, 0)

    grid_spec = pltpu.PrefetchScalarGridSpec(
        num_scalar_prefetch=3,
        grid=(n_blocks,),
        in_specs=[pl.BlockSpec((blk * SLABS, LANES), row_blk),
                  pl.BlockSpec((None, 1, d, 2 * D_FF), weight_blk),
                  pl.BlockSpec((1, 1, 2 * D_FF), expert_blk),
                  pl.BlockSpec((None, 1, D_FF, d), weight_blk),
                  pl.BlockSpec((1, 1, d), expert_blk),
                  pl.BlockSpec((GLU_BLOCK, GLU_BLOCK), lambda i, *_: (0, 0))],
        out_specs=pl.BlockSpec((blk * SLABS, LANES), row_blk),
        scratch_shapes=[pltpu.VMEM((blk, d), BF16), pltpu.VMEM((blk, D_FF), BF16),
                        pltpu.VMEM((d, 2 * D_FF), BF16), pltpu.VMEM((D_FF, d), BF16)],
    )
    return pl.pallas_call(
        _expert_kernel,
        grid_spec=grid_spec,
        out_shape=jax.ShapeDtypeStruct(hs.shape, F32),
        compiler_params=_cparams(1, EXPERT_VMEM_LIMIT),
        name="expert_ffn",
    )(block_e, first, n_valid, hs, w1, b1p, w2, b2, perm)


def _combine_kernel(dest_hbm, ys_hbm, gate_ref, x_ref, g2_ref, o_ref,
                    idx0, idx1, buf0, buf1, sems, isems):
    i = pl.program_id(0)
    n_tiles = pl.num_programs(0)
    tt = TOKEN_TILE
    n_assign = tt * TOP_K
    idx_refs, buf_refs = (idx0, idx1), (buf0, buf1)

    def idx_copy(tile, s):
        return pltpu.make_async_copy(dest_hbm.at[pl.ds(tile * n_assign, n_assign)], idx_refs[s],
                                     isems.at[s])

    def issue(s):
        def start(t, carry):
            for k in range(TOP_K):
                _row_copy(ys_hbm, idx_refs[s][t * TOP_K + k], buf_refs[s], k * tt + t,
                          sems.at[s]).start(priority=k % 2)
            return carry
        lax.fori_loop(0, tt, start, 0, unroll=ISSUE_UNROLL)

    @pl.when(i == 0)
    def _():
        idx_copy(0, 0).start()
        idx_copy(0, 0).wait()
        issue(0)

        @pl.when(n_tiles > 1)
        def _():
            idx_copy(1, 1).start()

    for s in (0, 1):
        @pl.when(i % 2 == s)
        def _(s=s):
            @pl.when(i + 1 < n_tiles)
            def _():
                idx_copy(i + 1, 1 - s).wait()
                issue(1 - s)

            pltpu.make_async_copy(ys_hbm.at[pl.ds(0, n_assign * SLABS), :], buf_refs[s], sems.at[s]).wait()

            @pl.when(i + 2 < n_tiles)
            def _():
                idx_copy(i + 2, s).start()

            gate = gate_ref[...]
            g2 = g2_ref[0]
            for sl in range(SLABS):
                cols = slice(sl * LANES, (sl + 1) * LANES)
                y = jnp.zeros((tt, LANES), F32)
                for k in range(TOP_K):
                    y = y + gate[:, k:k + 1] * buf_refs[s][pl.ds(k * tt * SLABS + sl, tt, stride=SLABS), :]
                o_ref[:, cols] = x_ref[:, cols] + g2[:, cols] * y


def _combine_call(dest_flat, ys, gate, x1, mod3, n_lat_rows, seq):
    n, d = x1.shape
    tt = TOKEN_TILE
    tiles_per_seq = seq // tt
    n_batch = n_lat_rows // seq

    def mod_row(i):
        return jnp.minimum(i // tiles_per_seq, n_batch)

    return pl.pallas_call(
        _combine_kernel,
        grid=(n // tt,),
        in_specs=[pl.BlockSpec(memory_space=pl.ANY), pl.BlockSpec(memory_space=pl.ANY),
                  pl.BlockSpec((tt, TOP_K), lambda i: (i, 0)),
                  pl.BlockSpec((tt, d), lambda i: (i, 0)),
                  pl.BlockSpec((1, 1, d), lambda i: (mod_row(i), 0, 5))],
        out_specs=pl.BlockSpec((tt, d), lambda i: (i, 0)),
        out_shape=jax.ShapeDtypeStruct((n, d), F32),
        scratch_shapes=[pltpu.SMEM((tt * TOP_K,), jnp.int32), pltpu.SMEM((tt * TOP_K,), jnp.int32),
                        pltpu.VMEM((TOP_K * tt * SLABS, LANES), F32),
                        pltpu.VMEM((TOP_K * tt * SLABS, LANES), F32),
                        pltpu.SemaphoreType.DMA((2,)), pltpu.SemaphoreType.DMA((2,))],
        compiler_params=_cparams(1),
        name="combine",
    )(dest_flat, ys, gate, x1, mod3)


def _block_diag(blocks):
    g, a, b = blocks.shape
    eye = jnp.eye(g, dtype=blocks.dtype)
    return (eye[:, None, :, None] * blocks[:, :, None, :]).reshape(g * a, g * b)


def _rope_tables(seq, extra_rows):
    half = DIFF_QK_DIM // 2
    inv = ROPE_THETA ** (-jnp.arange(0, half, 2, dtype=F32) / half)
    pos = jnp.arange(seq)
    c = jnp.arange(GROUP_WIDTH)
    dd = c % DIFF_QK_DIM
    axis_pos = jnp.where((dd < half)[None, :], (pos // GRID_W)[:, None], (pos % GRID_W)[:, None])
    ang = axis_pos.astype(F32) * inv[dd % (half // 2)][None, :]
    sign = jnp.where((dd % half) < half // 2, -1.0, 1.0).astype(F32)
    cos_t = jnp.concatenate([jnp.cos(ang), jnp.ones((extra_rows, GROUP_WIDTH), F32)], axis=0)
    sin_t = jnp.concatenate([jnp.sin(ang) * sign[None, :], jnp.zeros((extra_rows, GROUP_WIDTH), F32)], axis=0)
    return cos_t, sin_t


def _dft_tables(n, dtype):
    k = np.arange(n, dtype=np.int64)
    ang = ((k[:, None] * k[None, :]) % n).astype(np.float64) * (2.0 * math.pi / n)
    return jnp.asarray(np.cos(ang), dtype), jnp.asarray(np.sin(ang), dtype)


def _na_bias_tables(rpb, grid_rows):
    w = GRID_W
    col = np.arange(w)
    col_start = np.clip(col - NA_COLS // 2, 0, w - NA_COLS)
    in_win = (col[None, :] >= col_start[:, None]) & (col[None, :] < col_start[:, None] + NA_COLS)
    rel_c = np.clip(col[None, :] - col[:, None], 1 - NA_COLS, NA_COLS - 1) + NA_COLS - 1
    n_tiles = grid_rows // NA_Q_ROWS
    cases = (0, 1, n_tiles - 1)
    rel_r = np.zeros((3, NA_Q_ROWS, NA_KEY_ROWS), np.int32)
    valid = np.zeros((3, NA_Q_ROWS, NA_KEY_ROWS), bool)
    for ci, tile in enumerate(cases):
        r0 = tile * NA_Q_ROWS
        k0 = int(np.clip(r0 - NA_ROWS // 2, 0, grid_rows - NA_KEY_ROWS))
        for j in range(NA_Q_ROWS):
            start = int(np.clip(r0 + j - NA_ROWS // 2, 0, grid_rows - NA_ROWS))
            for i in range(NA_KEY_ROWS):
                kr = k0 + i
                valid[ci, j, i] = start <= kr < start + NA_ROWS
                rel_r[ci, j, i] = np.clip(kr - (r0 + j) + NA_ROWS - 1, 0, 2 * NA_ROWS - 2)
    del rel_c
    edge = w - NA_COLS
    ext = jnp.pad(rpb.astype(F32), ((0, 0), (0, 0), (edge, edge)), mode='edge')
    t = jnp.stack([ext[:, :, w - 1 - q:2 * w - 1 - q] for q in range(w)], axis=2)
    t = jnp.where(in_win[None, None], t, MASK_VALUE)
    masked = jnp.full((rpb.shape[0], w, w), MASK_VALUE, F32)
    return jnp.stack([jnp.concatenate([jnp.concatenate(
        [t[:, rel_r[c, j, i]] if valid[c, j, i] else masked for i in range(NA_KEY_ROWS)], axis=-1)
        for j in range(NA_Q_ROWS)], axis=-2) for c in range(3)], axis=0)


def _moe(h2_slabs, logits, x1, mod3, w1, b1p, w2, b2, perm, layer, n_lat_rows, seq):
    n = x1.shape[0]
    blk = EXPERT_BLOCK
    assert n * TOP_K <= RANK_STRIDE
    gate, code, counts = _route_call(logits)
    counts = counts[0, :N_EXPERTS]
    padded = (counts + blk - 1) // blk * blk
    padded_end = jnp.cumsum(padded)
    padded_start = padded_end - padded
    codes = code.reshape(1, n * TOP_K)
    experts = lax.shift_right_logical(codes, RANK_SHIFT)
    first_slot = jnp.sum(jnp.where(experts == jnp.arange(N_EXPERTS, dtype=jnp.int32)[:, None],
                                   padded_start.astype(jnp.int32)[:, None], 0), axis=0)
    dest = first_slot + (codes[0] & RANK_MASK)
    n_blocks = n * TOP_K // blk + N_EXPERTS
    block_row0 = jnp.arange(n_blocks, dtype=jnp.int32) * blk
    block_e = jnp.minimum(jnp.sum((padded_end[None, :] <= block_row0[:, None]).astype(jnp.int32), axis=1),
                          N_EXPERTS - 1).astype(jnp.int32)
    first = jnp.concatenate([jnp.ones((1,), jnp.int32),
                             (block_e[1:] != block_e[:-1]).astype(jnp.int32)])
    n_valid = (padded_end[-1:] // blk).astype(jnp.int32)
    fill_start = (padded_start + counts).astype(jnp.int32)
    pad_len = (padded - counts).astype(jnp.int32)
    hs = _dispatch_call(fill_start, pad_len, dest, h2_slabs, n_blocks * blk)
    ys = _expert_call(block_e, first, n_valid, hs, w1, b1p, w2, b2, perm, layer, n_blocks)
    return _combine_call(dest, ys, gate, x1, mod3, n_lat_rows, seq)


def kernel(x, c, ctx, c_ctx, w_ada, b_ada, g_norm1, g_norm2, w_in, w_out, na_q_gain, na_k_gain, na_rpb, diff_q_gain, diff_k_gain, diff_lambda_q1, diff_lambda_k1, diff_lambda_q2, diff_lambda_k2, diff_subln, pool_w, pool_scale, fft_w, router_w, router_b, moe_w1, moe_b1, moe_w2, moe_b2):
    n_batch, seq, d = x.shape
    ctx_len = ctx.shape[1]
    depth = w_ada.shape[0]
    gw = GROUP_WIDTH
    assert d == D_MODEL and seq % ROW_TILE == 0 and seq % ctx_len == 0 and seq % TOKEN_TILE == 0
    assert (n_batch * ctx_len) % ROW_TILE == 0 and (seq // GRID_W) >= NA_KEY_ROWS
    n_lat = n_batch * seq
    n_ctx = n_batch * ctx_len

    xa, xb, ctx_blk0 = x.reshape(n_lat, d), ctx.reshape(n_ctx, d), 0
    mod_rows = -(-(n_batch + 1) // SUBLANES) * SUBLANES
    cvec = jnp.zeros((mod_rows, d), F32).at[:n_batch].set(c).at[n_batch].set(c_ctx)
    mod = _ada_call(cvec, w_ada, b_ada)

    cos_t, sin_t = _rope_tables(seq, ROW_TILE)
    ones = lambda w: _block_diag(jnp.ones((gw // w, w, w), BF16))
    g64, g32 = ones(HEAD_DIM), ones(DIFF_QK_DIM)
    cl_lat, sl_lat = _dft_tables(seq, BF16)
    cl_ctx, sl_ctx = _dft_tables(ctx_len, BF16)
    cc, sc = _dft_tables(gw // 4, F32)
    n_grp = fft_w.shape[1]
    cc_bd = _block_diag(jnp.broadcast_to(cc, (n_grp,) + cc.shape)).astype(BF16)
    sc_bd = _block_diag(jnp.broadcast_to(sc, (n_grp,) + sc.shape)).astype(BF16)
    src = np.arange(GLU_BLOCK)
    dst = np.where(src % 2 == 0, src // 2, LANES + src // 2)
    perm_np = np.zeros((GLU_BLOCK, GLU_BLOCK), np.float32)
    perm_np[src, dst] = 1.0
    glu_perm = jnp.asarray(perm_np, BF16)

    tile = lambda v: jnp.tile(v.astype(F32), gw // v.shape[0])

    def in_proj_params(l):
        gains = jnp.stack([tile(na_q_gain[l]) * (HEAD_DIM ** -0.5 * LOG2_E), tile(na_k_gain[l]),
                           tile(diff_q_gain[l]) * (DIFF_QK_DIM ** -0.5 * LOG2_E), tile(diff_k_gain[l])]
                          + [jnp.zeros((gw,), F32)] * 4)
        return (mod[l].reshape(mod_rows, 1, 6 * d), g_norm1[l].reshape(1, d), w_in[l].astype(BF16),
                gains, cos_t, sin_t, g64, g32)

    for l in range(depth):
        ctx_out = l < depth - 1
        lam_init = 0.8 - 0.6 * math.exp(-0.3 * l)
        mod3 = mod[l].reshape(mod_rows, 1, 6 * d)
        qkv, pool_in, fft_in = _in_call(xa, xb, ctx_blk0, n_lat + n_ctx, in_proj_params(l), n_lat, seq)
        lam_vecs = jnp.stack([diff_lambda_q1[l], diff_lambda_k1[l], diff_lambda_q2[l], diff_lambda_k2[l]]
                             + [jnp.zeros_like(diff_lambda_q1[l])] * 4).astype(F32)
        subln = tile(diff_subln[l]).reshape(1, gw)
        bias_tab = _na_bias_tables(na_rpb[l] * LOG2_E, seq // GRID_W)
        pool_bd = _block_diag(pool_w[l]).astype(BF16)
        fftw_bd = _block_diag(fft_w[l]).astype(BF16)
        rw_f32 = jnp.zeros((d, LANES), F32).at[:, :N_EXPERTS].set(router_w[l])
        rw_hi = rw_f32.astype(BF16)
        rw_pad = jnp.concatenate([rw_hi, (rw_f32 - rw_hi.astype(F32)).astype(BF16)], axis=1)
        rb_pad = jnp.full((1, LANES), -jnp.inf, F32).at[0, :N_EXPERTS].set(router_b[l])
        n_e = moe_w1.shape[1]
        b1p = moe_b1[l].reshape(n_e, 2 * D_FF // GLU_BLOCK, LANES, 2).transpose(0, 1, 3, 2)
        b1p = b1p.reshape(n_e, 1, 2 * D_FF)
        b2 = moe_b2[l].reshape(n_e, 1, d)

        rows = n_lat + n_ctx if ctx_out else n_lat
        ya = _na_call(qkv, bias_tab, n_batch, seq, rows, ctx_len)
        yd = _df_call(qkv, lam_vecs, subln, lam_init, n_batch, seq, rows, ctx_len)
        yb = _pool_call(pool_in, pool_bd, pool_scale[l].reshape(1, gw), n_batch, seq, 0, rows)
        yf = _fft_call(fft_in, cl_lat, sl_lat, cc_bd, sc_bd, fftw_bd, n_batch, seq, 0, rows)
        if ctx_out:
            ya, yd = _ctx_attn_call(qkv, lam_vecs, subln, lam_init, ya, yd, n_batch, seq, ctx_len)
            yb = _pool_call(pool_in, pool_bd, pool_scale[l].reshape(1, gw), n_batch, ctx_len,
                            n_lat // ctx_len, rows, prev=yb)
            yf = _fft_call(fft_in, cl_ctx, sl_ctx, cc_bd, sc_bd, fftw_bd, n_batch, ctx_len,
                           n_lat // ctx_len, rows, prev=yf)
        x1, h2_slabs, logits = _out_call(ya, yd, yb, yf, w_out[l].astype(BF16), xa, xb, ctx_blk0, mod3,
                                         g_norm2[l].reshape(1, d), rw_pad, rb_pad, rows, n_lat, seq)
        x2 = _moe(h2_slabs, logits, x1, mod3, moe_w1, b1p, moe_w2, b2, glu_perm, l, n_lat, seq)
        xa, xb, ctx_blk0 = x2, x2, n_lat // ROW_TILE
    return x2[:n_lat].reshape(n_batch, seq, d)
```

```python
import functools
import math

import numpy as np
import jax
import jax.numpy as jnp
from jax import lax
from jax.experimental import pallas as pl
from jax.experimental.pallas import tpu as pltpu

F32 = jnp.float32
BF16 = jnp.bfloat16
HIGHEST = lax.Precision.HIGHEST

D_MODEL = 1024
DEPTH = 2
GRID_W = 64
HEAD_DIM = 64
GROUP_WIDTH = 256
N_HEADS = GROUP_WIDTH // HEAD_DIM
DIFF_QK_DIM = HEAD_DIM // 2
NA_ROWS = 8
NA_COLS = 16
POOL_WINDOWS = (2, 4, 8, 16)
POOL_HALO = max(POOL_WINDOWS) // 2
N_EXPERTS = 32
TOP_K = 4
D_FF = D_MODEL
SWIGLU_ALPHA = 1.702
SWIGLU_LIMIT = 7.0
ROPE_THETA = 10000.0
NORM_EPS = 1e-6
MASK_VALUE = -1e30

LANES = 128
SUBLANES = 8
ROW_TILE = 512
TOKEN_TILE = 512
COMBINE_TILE = 256
EXPERT_BLOCK = 512
Q_TILE = 256
DIFF_Q_TILE = 512
LOG2_E = math.log2(math.e)
NA_Q_ROWS = Q_TILE // GRID_W
NA_KEY_ROWS = NA_ROWS + NA_Q_ROWS - 1
FFT_TILE = 512
SLABS = D_MODEL // LANES
ISSUE_UNROLL = 8
RANK_STRIDE = 1 << 17
RANK_MASK = RANK_STRIDE - 1
RANK_SHIFT = 17
VMEM_LIMIT = 48 * 1024 * 1024
EXPERT_VMEM_LIMIT = 56 * 1024 * 1024


def _cparams(n_axes, vmem_limit=VMEM_LIMIT):
    return pltpu.CompilerParams(dimension_semantics=("arbitrary",) * n_axes,
                                vmem_limit_bytes=vmem_limit)


def _dot(a, b):
    return jnp.dot(a, b, preferred_element_type=F32)


def _dot_nt(a, b):
    return lax.dot_general(a, b, (((1,), (1,)), ((), ())), preferred_element_type=F32)


def _full(shape):
    zeros = (0,) * len(shape)
    return pl.BlockSpec(shape, lambda *_: zeros)


def _ada_kernel(c_ref, w_ref, b_ref, o_ref):
    c = c_ref[...]
    act = c * jax.nn.sigmoid(c)
    o_ref[0] = jnp.dot(act, w_ref[0], precision=HIGHEST, preferred_element_type=F32) + b_ref[0]


def _ada_call(cvec, w_ada, b_ada):
    depth, d, n = w_ada.shape
    r = cvec.shape[0]
    tn = 1024
    return pl.pallas_call(
        _ada_kernel,
        grid=(depth, n // tn),
        in_specs=[_full((r, d)),
                  pl.BlockSpec((1, d, tn), lambda l, j: (l, 0, j)),
                  pl.BlockSpec((1, 1, tn), lambda l, j: (l, 0, j))],
        out_specs=pl.BlockSpec((1, r, tn), lambda l, j: (l, 0, j)),
        out_shape=jax.ShapeDtypeStruct((depth, r, n), F32),
        compiler_params=_cparams(2),
        name="ada_mod",
    )(cvec, w_ada, b_ada.reshape(depth, 1, n))


def _seg_mean_sq(p, gmat_ref, width):
    sq = p * p
    hi = sq.astype(BF16)
    lo = (sq - hi.astype(F32)).astype(BF16)
    g = gmat_ref[...]
    return (_dot(hi, g) + _dot(lo, g)) * (1.0 / width)


def _stream_specs(tm, d, n_lat_tiles, ctx_blk0):
    return [pl.BlockSpec((tm, d), lambda i: (jnp.minimum(i, n_lat_tiles - 1), 0)),
            pl.BlockSpec((tm, d), lambda i: (ctx_blk0 + jnp.maximum(i - n_lat_tiles, 0), 0))]


def _in_kernel(xa_ref, xb_ref, *rest, n_lat_tiles):
    x = jnp.where(pl.program_id(0) < n_lat_tiles, xa_ref[...], xb_ref[...])
    _in_body(x, *rest)


def _in_body(x, sh_ref, sc_ref, g_ref, w_ref, gains_ref, cos_ref, sin_ref, g64_ref, g32_ref,
             qkv_ref, pool_ref, fft_ref):
    ms = jnp.mean(x * x, axis=-1, keepdims=True)
    y = x * lax.rsqrt(ms + NORM_EPS) * g_ref[...]
    h = (y * (1.0 + sc_ref[0]) + sh_ref[0]).astype(BF16)
    gw = GROUP_WIDTH

    def proj(g):
        return _dot(h, w_ref[:, g * gw:(g + 1) * gw])

    def put(g, val):
        qkv_ref[:, g * gw:(g + 1) * gw] = val.astype(BF16)

    def normed(p, gmat_ref, width, row):
        return p * lax.rsqrt(_seg_mean_sq(p, gmat_ref, width) + NORM_EPS) * gains_ref[row:row + 1, :]

    lane = lax.broadcasted_iota(jnp.int32, (1, gw), 1)
    first_half = (lane % 16) < 8

    def rope(p):
        rot = jnp.where(first_half, pltpu.roll(p, gw - 8, 1), pltpu.roll(p, 8, 1))
        return p * cos_ref[...] + rot * sin_ref[...]

    put(0, normed(proj(0), g64_ref, HEAD_DIM, 0))
    put(1, normed(proj(1), g64_ref, HEAD_DIM, 1))
    put(2, proj(2))
    put(3, rope(normed(proj(3), g32_ref, DIFF_QK_DIM, 2)))
    put(4, rope(normed(proj(4), g32_ref, DIFF_QK_DIM, 3)))
    put(5, proj(5))
    pool_ref[...] = proj(6)
    fft_ref[...] = proj(7).astype(BF16)


def _in_proj_io(tm, rows, params, n_lat_rows, seq):
    mod3, g1, w_in_bf, gains, cos_t, sin_t, g64, g32 = params
    d = D_MODEL
    gw = GROUP_WIDTH
    n_lat_tiles = n_lat_rows // tm
    tiles_per_seq = seq // tm
    n_batch = n_lat_rows // seq

    def mod_row(i):
        return jnp.minimum(i // tiles_per_seq, n_batch)

    def tab_row(i):
        return jnp.where(i < n_lat_tiles, i % tiles_per_seq, tiles_per_seq)

    full = lambda shape: pl.BlockSpec(shape, lambda i, *_: (0,) * len(shape))
    in_specs = [pl.BlockSpec((1, 1, d), lambda i, *_: (mod_row(i), 0, 0)),
                pl.BlockSpec((1, 1, d), lambda i, *_: (mod_row(i), 0, 1)),
                full((1, d)), full(w_in_bf.shape), full(gains.shape),
                pl.BlockSpec((tm, gw), lambda i, *_: (tab_row(i), 0)),
                pl.BlockSpec((tm, gw), lambda i, *_: (tab_row(i), 0)),
                full((gw, gw)), full((gw, gw))]
    args = [mod3, mod3, g1, w_in_bf, gains, cos_t, sin_t, g64, g32]
    out_specs = [pl.BlockSpec((tm, 6 * gw), lambda i, *_: (i, 0)),
                 pl.BlockSpec((tm, gw), lambda i, *_: (i, 0)),
                 pl.BlockSpec((tm, gw), lambda i, *_: (i, 0))]
    out_shape = [jax.ShapeDtypeStruct((rows, 6 * gw), BF16),
                 jax.ShapeDtypeStruct((rows, gw), F32),
                 jax.ShapeDtypeStruct((rows, gw), BF16)]
    return in_specs, args, out_specs, out_shape


def _in_call(xa, xb, ctx_blk0, rows, params, n_lat_rows, seq):
    tm = ROW_TILE
    n_lat_tiles = n_lat_rows // tm
    in_specs, args, out_specs, out_shape = _in_proj_io(tm, rows, params, n_lat_rows, seq)
    return pl.pallas_call(
        functools.partial(_in_kernel, n_lat_tiles=n_lat_tiles),
        grid=(rows // tm,),
        in_specs=_stream_specs(tm, D_MODEL, n_lat_tiles, ctx_blk0) + in_specs,
        out_specs=out_specs,
        out_shape=out_shape,
        compiler_params=_cparams(1),
        name="in_proj",
    )(xa, xb, *args)


def _lane_mask(width, start, size):
    lane = lax.broadcasted_iota(jnp.int32, (1, width), 1)
    return (lane >= start) & (lane < start + size)


def _softmax_pv(qm, keys, biases, vals, sum_lane=None):
    scores = []
    for k, bias in zip(keys, biases):
        s = _dot_nt(qm, k)
        scores.append(s if bias is None else s + bias)
    m = scores[0].max(axis=-1, keepdims=True)
    for s in scores[1:]:
        m = jnp.maximum(m, s.max(axis=-1, keepdims=True))
    o = None
    l = None
    for s, v in zip(scores, vals):
        e = jnp.exp2(s - m)
        part = _dot(e.astype(BF16), v)
        o = part if o is None else o + part
        if sum_lane is None:
            ls = e.sum(axis=-1, keepdims=True)
            l = ls if l is None else l + ls
    if sum_lane is not None:
        lane = lax.broadcasted_iota(jnp.int32, (1, o.shape[1]), 1)
        l = jnp.sum(jnp.where(lane == sum_lane, o, 0.0), axis=-1, keepdims=True)
    return o, l


def _na_heads(q, keys, bias_fn, vals):
    gw = GROUP_WIDTH
    acc = jnp.zeros((q.shape[0], gw), F32)
    for h in range(N_HEADS):
        mask = _lane_mask(gw, h * HEAD_DIM, HEAD_DIM)
        qm = jnp.where(mask, q, jnp.zeros_like(q))
        o, l = _softmax_pv(qm, keys, bias_fn(h), vals)
        acc = acc + jnp.where(mask, o / l, 0.0)
    return acc


def _lambda(lam_ref, lam_init):
    lv = lam_ref[...]
    d1 = jnp.sum(lv[0:1, :] * lv[1:2, :], axis=-1, keepdims=True)
    d2 = jnp.sum(lv[2:3, :] * lv[3:4, :], axis=-1, keepdims=True)
    return jnp.exp(d1) - jnp.exp(d2) + lam_init


def _diff_heads(q, keys, vals, lam, subln, lam_init, acc_ref):
    gw = GROUP_WIDTH
    lane = lax.broadcasted_iota(jnp.int32, (1, gw), 1)
    acc_ref[...] = jnp.zeros_like(acc_ref)

    def scores_of(h, part):
        plo = h * HEAD_DIM + part * DIFF_QK_DIM
        qm = jnp.where((lane >= plo) & (lane < plo + DIFF_QK_DIM), q, jnp.zeros_like(q))
        scores = [_dot_nt(qm, k) for k in keys]
        m = scores[0].max(axis=-1, keepdims=True)
        for s in scores[1:]:
            m = jnp.maximum(m, s.max(axis=-1, keepdims=True))
        return scores, m

    def attend(h, scores, m):
        lo = h * HEAD_DIM
        hm = (lane >= lo) & (lane < lo + HEAD_DIM)
        o = None
        for s, v in zip(scores, vals):
            part = _dot(jnp.exp2(s - m).astype(BF16), jnp.where(hm, v, jnp.ones_like(v)))
            o = part if o is None else o + part
        l = jnp.sum(jnp.where(lane == (lo + HEAD_DIM) % gw, o, 0.0), axis=-1, keepdims=True)
        return o / l

    combos = [(h, part) for h in range(N_HEADS) for part in range(2)]
    pending = [scores_of(*combos[0])]
    outs = []
    for c, (h, part) in enumerate(combos):
        if c + 1 < len(combos):
            pending.append(scores_of(*combos[c + 1]))
        outs.append(attend(h, *pending.pop(0)))
        if part == 1:
            lo = h * HEAD_DIM
            hm = (lane >= lo) & (lane < lo + HEAD_DIM)
            a = jnp.where(hm, outs[-2] - lam * outs[-1], 0.0)
            ms = jnp.sum(a * a, axis=-1, keepdims=True) * (1.0 / HEAD_DIM)
            acc_ref[...] += a * lax.rsqrt(ms + NORM_EPS)
    return acc_ref[...] * subln * (1.0 - lam_init)


def _na_kernel(q_ref, k_ref, v_ref, kc_ref, vc_ref, bias_ref, o_ref, *, grid_rows):
    j = pl.program_id(1)
    key_row0 = jnp.clip(j * NA_Q_ROWS - NA_ROWS // 2, 0, grid_rows - NA_KEY_ROWS)
    ks = pl.multiple_of(key_row0 * GRID_W, GRID_W)
    n_win = NA_KEY_ROWS * GRID_W
    kwin = k_ref[pl.ds(ks, n_win), :]
    vwin = v_ref[pl.ds(ks, n_win), :]
    acc = _na_heads(q_ref[...], [kwin, kc_ref[...]], lambda h: [bias_ref[0, h], None],
                    [vwin, vc_ref[...]])
    o_ref[...] = acc.astype(BF16)


def _na_call(qkv, bias_tab, n_batch, seq, total_rows, ctx_len):
    gw = GROUP_WIDTH
    qt = Q_TILE
    n_q = seq // qt
    grid_rows = seq // GRID_W
    ctx_blk0 = n_batch * seq // ctx_len

    def bias_case(j):
        return jnp.where(j == 0, 0, jnp.where(j == n_q - 1, 2, 1))

    return pl.pallas_call(
        functools.partial(_na_kernel, grid_rows=grid_rows),
        grid=(n_batch, n_q),
        in_specs=[pl.BlockSpec((qt, gw), lambda b, j: (b * n_q + j, 0)),
                  pl.BlockSpec((seq, gw), lambda b, j: (b, 1)),
                  pl.BlockSpec((seq, gw), lambda b, j: (b, 2)),
                  pl.BlockSpec((ctx_len, gw), lambda b, j: (ctx_blk0 + b, 1)),
                  pl.BlockSpec((ctx_len, gw), lambda b, j: (ctx_blk0 + b, 2)),
                  pl.BlockSpec((1,) + bias_tab.shape[1:], lambda b, j: (bias_case(j), 0, 0, 0))],
        out_specs=pl.BlockSpec((qt, gw), lambda b, j: (b * n_q + j, 0)),
        out_shape=jax.ShapeDtypeStruct((total_rows, gw), BF16),
        compiler_params=_cparams(2),
        name="na_attn",
    )(qkv, qkv, qkv, qkv, qkv, bias_tab)


def _df_kernel(q_ref, k_ref, v_ref, kc_ref, vc_ref, lam_ref, subln_ref, o_ref, acc_ref, *, lam_init):
    lam = _lambda(lam_ref, lam_init)
    acc = _diff_heads(q_ref[...], [kc_ref[...], k_ref[...]], [vc_ref[...], v_ref[...]], lam,
                      subln_ref[...], lam_init, acc_ref)
    o_ref[...] = acc.astype(BF16)


def _df_call(qkv, lam_vecs, subln, lam_init, n_batch, seq, total_rows, ctx_len):
    gw = GROUP_WIDTH
    qt = DIFF_Q_TILE
    n_q = seq // qt
    ctx_blk0 = n_batch * seq // ctx_len
    return pl.pallas_call(
        functools.partial(_df_kernel, lam_init=lam_init),
        grid=(n_batch, n_q),
        in_specs=[pl.BlockSpec((qt, gw), lambda b, j: (b * n_q + j, 3)),
                  pl.BlockSpec((seq, gw), lambda b, j: (b, 4)),
                  pl.BlockSpec((seq, gw), lambda b, j: (b, 5)),
                  pl.BlockSpec((ctx_len, gw), lambda b, j: (ctx_blk0 + b, 4)),
                  pl.BlockSpec((ctx_len, gw), lambda b, j: (ctx_blk0 + b, 5)),
                  _full(lam_vecs.shape),
                  _full(subln.shape)],
        out_specs=pl.BlockSpec((qt, gw), lambda b, j: (b * n_q + j, 0)),
        out_shape=jax.ShapeDtypeStruct((total_rows, gw), BF16),
        scratch_shapes=[pltpu.VMEM((qt, gw), F32)],
        compiler_params=_cparams(2),
        name="diff_attn",
    )(qkv, qkv, qkv, qkv, qkv, lam_vecs, subln)


def _ctx_attn_kernel(qkv_ref, lam_ref, subln_ref, ya_in, yd_in, ya_ref, yd_ref, acc_ref, *, lam_init):
    del ya_in, yd_in
    gw = GROUP_WIDTH
    col = lambda g: qkv_ref[:, g * gw:(g + 1) * gw]
    ya = _na_heads(col(0), [col(1)], lambda h: [None], [col(2)])
    ya_ref[...] = ya.astype(BF16)
    lam = _lambda(lam_ref, lam_init)
    yd = _diff_heads(col(3), [col(4)], [col(5)], lam, subln_ref[...], lam_init, acc_ref)
    yd_ref[...] = yd.astype(BF16)


def _ctx_attn_call(qkv, lam_vecs, subln, lam_init, ya, yd, n_batch, seq, ctx_len):
    gw = GROUP_WIDTH
    ctx_blk0 = n_batch * seq // ctx_len
    any_spec = pl.BlockSpec(memory_space=pl.ANY)
    out_spec = pl.BlockSpec((ctx_len, gw), lambda b: (ctx_blk0 + b, 0))
    return pl.pallas_call(
        functools.partial(_ctx_attn_kernel, lam_init=lam_init),
        grid=(n_batch,),
        in_specs=[pl.BlockSpec((ctx_len, 6 * gw), lambda b: (ctx_blk0 + b, 0)),
                  _full(lam_vecs.shape), _full(subln.shape), any_spec, any_spec],
        out_specs=[out_spec, out_spec],
        out_shape=[jax.ShapeDtypeStruct(ya.shape, BF16), jax.ShapeDtypeStruct(yd.shape, BF16)],
        input_output_aliases={3: 0, 4: 1},
        scratch_shapes=[pltpu.VMEM((ctx_len, gw), F32)],
        compiler_params=_cparams(1),
        name="ctx_attn",
    )(qkv, lam_vecs, subln, ya, yd)


def _pool_kernel(p_ref, w_ref, scale_ref, *rest, seq, aliased):
    o_ref, pad_ref = rest[-2], rest[-1]
    del aliased
    gw = GROUP_WIDTH
    halo = POOL_HALO
    pad_ref[0:halo, :] = jnp.zeros((halo, gw), F32)
    pad_ref[halo + seq:, :] = jnp.zeros((halo, gw), F32)
    pad_ref[halo:halo + seq, :] = p_ref[...]
    chunk = min(seq, 256)
    lane_group = lax.broadcasted_iota(jnp.int32, (1, gw), 1) // (gw // len(POOL_WINDOWS))
    for c0 in range(0, seq, chunk):
        def at(off):
            return pad_ref[halo + c0 + off:halo + c0 + off + chunk, :]
        pos = c0 + lax.broadcasted_iota(jnp.int32, (chunk, 1), 0)
        x = at(0)
        run = x
        mean = None
        lo_done, hi_done = 0, 0
        for g, win in enumerate(POOL_WINDOWS):
            half = win // 2
            for off in range(-half, -lo_done):
                run = run + at(off)
            for off in range(hi_done + 1, half):
                run = run + at(off)
            lo_done, hi_done = half, half - 1
            cnt = (jnp.minimum(pos + half, seq) - jnp.maximum(pos - half, 0)).astype(F32)
            m = run / cnt
            mean = m if mean is None else jnp.where(lane_group == g, m, mean)
        y = _dot((mean - x).astype(BF16), w_ref[...]) * scale_ref[...]
        o_ref[c0:c0 + chunk, :] = y.astype(BF16)


def _pool_call(pool_in, w_bd, scale, n_seq, seq, row_blk0, total_rows, prev=None):
    gw = GROUP_WIDTH
    in_specs = [pl.BlockSpec((seq, gw), lambda b: (row_blk0 + b, 0)), _full((gw, gw)), _full((1, gw))]
    args = [pool_in, w_bd, scale]
    aliases = {}
    if prev is not None:
        in_specs.append(pl.BlockSpec(memory_space=pl.ANY))
        args.append(prev)
        aliases = {3: 0}
    return pl.pallas_call(
        functools.partial(_pool_kernel, seq=seq, aliased=prev is not None),
        grid=(n_seq,),
        in_specs=in_specs,
        out_specs=pl.BlockSpec((seq, gw), lambda b: (row_blk0 + b, 0)),
        out_shape=jax.ShapeDtypeStruct((total_rows, gw), BF16),
        scratch_shapes=[pltpu.VMEM((seq + 2 * POOL_HALO, gw), F32)],
        input_output_aliases=aliases,
        compiler_params=_cparams(1),
        name="pool_mix",
    )(*args)


def _fft_kernel(t_ref, cl_ref, sl_ref, cc_ref, sc_ref, w_ref, *rest, norm):
    o_ref, a_ref, b_ref = rest[-3], rest[-2], rest[-1]

    @pl.when(pl.program_id(1) == 0)
    def _():
        t = t_ref[...]
        a_ref[...] = _dot(t, cc_ref[...]).astype(BF16)
        b_ref[...] = _dot(t, sc_ref[...]).astype(BF16)

    f = (_dot(cl_ref[...], a_ref[...]) - _dot(sl_ref[...], b_ref[...])) * norm
    o_ref[...] = _dot(f.astype(BF16), w_ref[...]).astype(BF16)


def _fft_call(fft_in, cl, sl, cc_bd, sc_bd, w_bd, n_seq, seq, row_blk0, total_rows, prev=None):
    gw = GROUP_WIDTH
    tk = min(FFT_TILE, seq)
    n_k = seq // tk
    in_specs = [pl.BlockSpec((seq, gw), lambda b, k: (row_blk0 + b, 0)),
                pl.BlockSpec((tk, seq), lambda b, k: (k, 0)),
                pl.BlockSpec((tk, seq), lambda b, k: (k, 0)),
                _full((gw, gw)), _full((gw, gw)), _full((gw, gw))]
    args = [fft_in, cl, sl, cc_bd, sc_bd, w_bd]
    aliases = {}
    if prev is not None:
        in_specs.append(pl.BlockSpec(memory_space=pl.ANY))
        args.append(prev)
        aliases = {6: 0}
    norm = 1.0 / math.sqrt(seq * (gw // 4))
    return pl.pallas_call(
        functools.partial(_fft_kernel, norm=norm),
        grid=(n_seq, n_k),
        in_specs=in_specs,
        out_specs=pl.BlockSpec((tk, gw), lambda b, k: ((row_blk0 + b) * n_k + k, 0)),
        out_shape=jax.ShapeDtypeStruct((total_rows, gw), BF16),
        scratch_shapes=[pltpu.VMEM((seq, gw), BF16), pltpu.VMEM((seq, gw), BF16)],
        input_output_aliases=aliases,
        compiler_params=_cparams(2),
        name="fourier_mix",
    )(*args)


def _out_kernel(ya_ref, yd_ref, yb_ref, yf_ref, w_ref, xa_ref, xb_ref, g1_ref, sh2_ref, sc2_ref, gn2_ref,
                rw_ref, rb_ref, x1_ref, h2_ref, lg_ref, *, n_lat_tiles):
    gw = GROUP_WIDTH
    x = jnp.where(pl.program_id(0) < n_lat_tiles, xa_ref[...], xb_ref[...])
    acc = _dot(ya_ref[...], w_ref[0:gw, :])
    acc = acc + _dot(yd_ref[...], w_ref[gw:2 * gw, :])
    acc = acc + _dot(yb_ref[...], w_ref[2 * gw:3 * gw, :])
    acc = acc + _dot(yf_ref[...], w_ref[3 * gw:4 * gw, :])
    x1 = x + g1_ref[0] * acc
    x1_ref[...] = x1
    ms = jnp.mean(x1 * x1, axis=-1, keepdims=True)
    h2 = x1 * lax.rsqrt(ms + NORM_EPS) * gn2_ref[...] * (1.0 + sc2_ref[0]) + sh2_ref[0]
    h_hi = h2.astype(BF16)
    h_lo = (h2 - h_hi.astype(F32)).astype(BF16)
    by_hi = _dot(h_hi, rw_ref[...])
    by_lo = _dot(h_lo, rw_ref[:, :LANES])
    lg_ref[...] = by_hi[:, :LANES] + by_hi[:, LANES:] + by_lo + rb_ref[...]
    tm = x1.shape[0]
    for s in range(SLABS):
        h2_ref[pl.ds(s, tm, stride=SLABS), :] = h2[:, s * LANES:(s + 1) * LANES]


def _out_call(ya, yd, yb, yf, w_out_bf, xa, xb, ctx_blk0, mod3, gn2, rw_pad, rb_pad, rows, n_lat_rows, seq):
    d = D_MODEL
    gw = GROUP_WIDTH
    tm = ROW_TILE
    n_lat_tiles = n_lat_rows // tm
    tiles_per_seq = seq // tm
    n_batch = n_lat_rows // seq

    def mod_row(i):
        return jnp.minimum(i // tiles_per_seq, n_batch)

    mix_spec = pl.BlockSpec((tm, gw), lambda i: (i, 0))
    mod_spec = lambda chunk: pl.BlockSpec((1, 1, d), lambda i: (mod_row(i), 0, chunk))
    return pl.pallas_call(
        functools.partial(_out_kernel, n_lat_tiles=n_lat_tiles),
        grid=(rows // tm,),
        in_specs=[mix_spec, mix_spec, mix_spec, mix_spec, _full((4 * gw, d))]
                 + _stream_specs(tm, d, n_lat_tiles, ctx_blk0) + [
                  mod_spec(2), mod_spec(3), mod_spec(4), _full((1, d)),
                  _full(rw_pad.shape), _full(rb_pad.shape)],
        out_specs=[pl.BlockSpec((tm, d), lambda i: (i, 0)),
                   pl.BlockSpec((tm * SLABS, LANES), lambda i: (i, 0)),
                   pl.BlockSpec((tm, LANES), lambda i: (i, 0))],
        out_shape=[jax.ShapeDtypeStruct((rows, d), F32),
                   jax.ShapeDtypeStruct((rows * SLABS, LANES), F32),
                   jax.ShapeDtypeStruct((rows, LANES), F32)],
        compiler_params=_cparams(1),
        name="out_proj",
    )(ya, yd, yb, yf, w_out_bf, xa, xb, mod3, mod3, mod3, gn2, rw_pad, rb_pad)


def _route_kernel(lg_ref, gate_ref, code_ref, cnt_ref, carry_ref):
    i = pl.program_id(0)

    @pl.when(i == 0)
    def _():
        carry_ref[...] = jnp.zeros_like(carry_ref)

    lg = lg_ref[...]
    tm = lg.shape[0]
    lane = lax.broadcasted_iota(jnp.int32, (tm, LANES), 1)
    vals, idxs = [], []
    onehot = jnp.zeros((tm, LANES), F32)
    for _ in range(TOP_K):
        m = lg.max(axis=-1, keepdims=True)
        idx = jnp.min(jnp.where(lg == m, lane, LANES), axis=-1, keepdims=True)
        sel = lane == idx
        onehot = onehot + sel.astype(F32)
        lg = jnp.where(sel, -jnp.inf, lg)
        vals.append(m)
        idxs.append(idx)
    exps = [jnp.exp(v - vals[0]) for v in vals]
    denom = exps[0] + exps[1] + exps[2] + exps[3]
    r_i = lax.broadcasted_iota(jnp.int32, (tm, tm), 0)
    c_i = lax.broadcasted_iota(jnp.int32, (tm, tm), 1)
    tri = (c_i < r_i).astype(BF16)
    before = _dot(tri, onehot.astype(BF16)) + carry_ref[...]
    col4 = lax.broadcasted_iota(jnp.int32, (tm, TOP_K), 1)
    gate_o = jnp.zeros((tm, TOP_K), F32)
    code_o = jnp.zeros((tm, TOP_K), F32)
    for k in range(TOP_K):
        rk = jnp.sum(jnp.where(lane == idxs[k], before, 0.0), axis=-1, keepdims=True)
        gate_o = jnp.where(col4 == k, exps[k] / denom, gate_o)
        code_o = jnp.where(col4 == k, idxs[k].astype(F32) * float(RANK_STRIDE) + rk, code_o)
    gate_ref[...] = gate_o
    code_ref[...] = code_o.astype(jnp.int32)
    carry_ref[...] = carry_ref[...] + jnp.sum(onehot, axis=0, keepdims=True)
    cnt_ref[...] = carry_ref[...].astype(jnp.int32)


def _route_call(logits):
    n = logits.shape[0]
    tm = ROW_TILE
    k_spec = pl.BlockSpec((tm, TOP_K), lambda i: (i, 0))
    return pl.pallas_call(
        _route_kernel,
        grid=(n // tm,),
        in_specs=[pl.BlockSpec((tm, LANES), lambda i: (i, 0))],
        out_specs=[k_spec, k_spec, _full((1, LANES))],
        out_shape=[jax.ShapeDtypeStruct((n, TOP_K), F32),
                   jax.ShapeDtypeStruct((n, TOP_K), jnp.int32),
                   jax.ShapeDtypeStruct((1, LANES), jnp.int32)],
        scratch_shapes=[pltpu.VMEM((1, LANES), F32)],
        compiler_params=_cparams(1),
        name="route",
    )(logits)


def _row_copy(src_ref, src_row, dst_ref, dst_row, sem):
    return pltpu.make_async_copy(
        src_ref.at[pl.ds(pl.multiple_of(src_row * SLABS, SLABS), SLABS), :],
        dst_ref.at[pl.ds(pl.multiple_of(dst_row * SLABS, SLABS), SLABS), :], sem)


def _pad_copy(zero_ref, hs_ref, start_row, n_rows, sem):
    return pltpu.make_async_copy(
        zero_ref.at[pl.ds(0, n_rows * SLABS), :],
        hs_ref.at[pl.ds(pl.multiple_of(start_row * SLABS, SLABS), n_rows * SLABS), :], sem)


def _dispatch_kernel(fill_ref, pad_ref, dest_hbm, h2_hbm, hs_ref,
                     idx0, idx1, idx2, rows0, rows1, rows2, zero_ref, zsem, isems, lsems, ssems):
    i = pl.program_id(0)
    n_tiles = pl.num_programs(0)
    tt = TOKEN_TILE
    n_assign = tt * TOP_K
    idx_refs, row_refs = (idx0, idx1, idx2), (rows0, rows1, rows2)

    def idx_copy(tile, s):
        return pltpu.make_async_copy(dest_hbm.at[pl.ds(tile * n_assign, n_assign)], idx_refs[s],
                                     isems.at[s])

    def row_load(tile, s):
        return pltpu.make_async_copy(h2_hbm.at[pl.ds(tile * (tt * SLABS), tt * SLABS), :], row_refs[s],
                                     lsems.at[s])

    def wait_sent(s):
        for _ in range(TOP_K):
            pltpu.make_async_copy(row_refs[s], hs_ref.at[pl.ds(0, tt * SLABS), :], ssems.at[s]).wait()

    @pl.when(i == 0)
    def _():
        idx_copy(0, 0).start()
        row_load(0, 0).start()
        zero_ref[...] = jnp.zeros_like(zero_ref)
        bits = [1 << b for b in reversed(range(int(math.log2(EXPERT_BLOCK))))]
        for phase in ("start", "wait"):
            def fill(e, carry, phase=phase):
                pos = fill_ref[e]
                pad = pad_ref[e]
                for bit in bits:
                    @pl.when((pad & bit) != 0)
                    def _(pos=pos, bit=bit):
                        cp = _pad_copy(zero_ref, hs_ref, pos, bit, zsem)
                        cp.start() if phase == "start" else cp.wait()
                    pos = pos + (pad & bit)
                return carry

            lax.fori_loop(0, N_EXPERTS, fill, 0)

    for s in range(3):
        @pl.when(i % 3 == s)
        def _(s=s):
            nxt, prev = (s + 1) % 3, (s + 2) % 3

            @pl.when(i >= 2)
            def _():
                wait_sent(nxt)

            @pl.when(i + 1 < n_tiles)
            def _():
                idx_copy(i + 1, nxt).start()
                row_load(i + 1, nxt).start()

            idx_copy(i, s).wait()
            row_load(i, s).wait()

            def start(t, carry):
                for k in range(TOP_K):
                    _row_copy(row_refs[s], t, hs_ref, idx_refs[s][t * TOP_K + k],
                              ssems.at[s]).start(priority=k % 2)
                return carry

            lax.fori_loop(0, tt, start, 0, unroll=ISSUE_UNROLL)

            @pl.when(i == n_tiles - 1)
            def _():
                @pl.when(i >= 1)
                def _():
                    wait_sent(prev)
                wait_sent(s)


def _dispatch_call(fill_start, pad_len, dest_flat, h2_slabs, n_slots):
    n = h2_slabs.shape[0] // SLABS
    tt = TOKEN_TILE
    any_spec = pl.BlockSpec(memory_space=pl.ANY)
    grid_spec = pltpu.PrefetchScalarGridSpec(
        num_scalar_prefetch=2,
        grid=(n // tt,),
        in_specs=[any_spec, any_spec],
        out_specs=any_spec,
        scratch_shapes=[pltpu.SMEM((tt * TOP_K,), jnp.int32)] * 3
                       + [pltpu.VMEM((tt * SLABS, LANES), F32)] * 3
                       + [pltpu.VMEM((EXPERT_BLOCK // 2 * SLABS, LANES), F32),
                          pltpu.SemaphoreType.DMA, pltpu.SemaphoreType.DMA((3,)),
                          pltpu.SemaphoreType.DMA((3,)), pltpu.SemaphoreType.DMA((3,))],
    )
    return pl.pallas_call(
        _dispatch_kernel,
        grid_spec=grid_spec,
        out_shape=jax.ShapeDtypeStruct((n_slots * SLABS, LANES), F32),
        compiler_params=_cparams(1),
        name="dispatch",
    )(fill_start, pad_len, dest_flat, h2_slabs)


GLU_BLOCK = 2 * LANES


def _expert_kernel(be_ref, first_ref, nv_ref, hs_ref, w1_ref, b1_ref, w2_ref, b2_ref, perm_ref,
                   ys_ref, h_ref, act_ref, w1p_ref, w2p_ref):
    del be_ref
    i = pl.program_id(0)
    n_glu = 2 * D_FF // GLU_BLOCK

    @pl.when((first_ref[i] != 0) & (i < nv_ref[0]))
    def _():
        for b in range(n_glu):
            cols = slice(b * GLU_BLOCK, (b + 1) * GLU_BLOCK)
            w1p_ref[:, cols] = _dot(w1_ref[0, :, cols].astype(BF16), perm_ref[...]).astype(BF16)
        w2p_ref[...] = w2_ref[0].astype(BF16)

    @pl.when(i < nv_ref[0])
    def _():
        blk = EXPERT_BLOCK
        for s in range(SLABS):
            h_ref[:, s * LANES:(s + 1) * LANES] = hs_ref[pl.ds(s, blk, stride=SLABS), :].astype(BF16)
        h = h_ref[...]
        for b in range(n_glu):
            cols = slice(b * GLU_BLOCK, (b + 1) * GLU_BLOCK)
            u = _dot(h, w1p_ref[:, cols]) + b1_ref[0, :, cols]
            glu = jnp.minimum(u[:, :LANES], SWIGLU_LIMIT)
            lin = jnp.clip(u[:, LANES:], -SWIGLU_LIMIT, SWIGLU_LIMIT)
            act = glu * jax.nn.sigmoid(SWIGLU_ALPHA * glu) * (lin + 1.0)
            act_ref[:, b * LANES:(b + 1) * LANES] = act.astype(BF16)
        y = _dot(act_ref[...], w2p_ref[...]) + b2_ref[0]
        for s in range(SLABS):
            ys_ref[pl.ds(s, blk, stride=SLABS), :] = y[:, s * LANES:(s + 1) * LANES]


def _expert_call(block_e, first, n_valid, hs, w1, b1p, w2, b2, perm, layer, n_blocks):
    blk = EXPERT_BLOCK
    d = D_MODEL

    def row_blk(i, be, first, nv):
        return (jnp.minimum(i, nv[0] - 1), 0)

    def expert_blk(i, be, first, nv):
        return (be[i], 0, 0)

    def weight_blk(i, be, first, nv):
        return (layer, be[i], 0, 0)

    grid_spec = pltpu.PrefetchScalarGridSpec(
        num_scalar_prefetch=3,
        grid=(n_blocks,),
        in_specs=[pl.BlockSpec((blk * SLABS, LANES), row_blk),
                  pl.BlockSpec((None, 1, d, 2 * D_FF), weight_blk),
                  pl.BlockSpec((1, 1, 2 * D_FF), expert_blk),
                  pl.BlockSpec((None, 1, D_FF, d), weight_blk),
                  pl.BlockSpec((1, 1, d), expert_blk),
                  pl.BlockSpec((GLU_BLOCK, GLU_BLOCK), lambda i, *_: (0, 0))],
        out_specs=pl.BlockSpec((blk * SLABS, LANES), row_blk),
        scratch_shapes=[pltpu.VMEM((blk, d), BF16), pltpu.VMEM((blk, D_FF), BF16),
                        pltpu.VMEM((d, 2 * D_FF), BF16), pltpu.VMEM((D_FF, d), BF16)],
    )
    return pl.pallas_call(
        _expert_kernel,
        grid_spec=grid_spec,
        out_shape=jax.ShapeDtypeStruct(hs.shape, F32),
        compiler_params=_cparams(1, EXPERT_VMEM_LIMIT),
        name="expert_ffn",
    )(block_e, first, n_valid, hs, w1, b1p, w2, b2, perm)


def _combine_kernel(dest_hbm, ys_hbm, gate_ref, x_ref, g2_ref, o_ref,
                    idx0, idx1, buf0, buf1, sems, isems):
    i = pl.program_id(0)
    n_tiles = pl.num_programs(0)
    tt = COMBINE_TILE
    n_assign = tt * TOP_K
    idx_refs, buf_refs = (idx0, idx1), (buf0, buf1)

    def idx_copy(tile, s):
        return pltpu.make_async_copy(dest_hbm.at[pl.ds(tile * n_assign, n_assign)], idx_refs[s],
                                     isems.at[s])

    def issue(s):
        def start(t, carry):
            for k in range(TOP_K):
                _row_copy(ys_hbm, idx_refs[s][t * TOP_K + k], buf_refs[s], k * tt + t,
                          sems.at[s]).start(priority=k % 2)
            return carry
        lax.fori_loop(0, tt, start, 0, unroll=ISSUE_UNROLL)

    @pl.when(i == 0)
    def _():
        idx_copy(0, 0).start()
        idx_copy(0, 0).wait()
        issue(0)

        @pl.when(n_tiles > 1)
        def _():
            idx_copy(1, 1).start()

    for s in (0, 1):
        @pl.when(i % 2 == s)
        def _(s=s):
            @pl.when(i + 1 < n_tiles)
            def _():
                idx_copy(i + 1, 1 - s).wait()
                issue(1 - s)

            pltpu.make_async_copy(ys_hbm.at[pl.ds(0, n_assign * SLABS), :], buf_refs[s], sems.at[s]).wait()

            @pl.when(i + 2 < n_tiles)
            def _():
                idx_copy(i + 2, s).start()

            gate = gate_ref[...]
            g2 = g2_ref[0]
            for sl in range(SLABS):
                cols = slice(sl * LANES, (sl + 1) * LANES)
                y = jnp.zeros((tt, LANES), F32)
                for k in range(TOP_K):
                    y = y + gate[:, k:k + 1] * buf_refs[s][pl.ds(k * tt * SLABS + sl, tt, stride=SLABS), :]
                o_ref[:, cols] = x_ref[:, cols] + g2[:, cols] * y


def _combine_call(dest_flat, ys, gate, x1, mod3, n_lat_rows, seq):
    n, d = x1.shape
    tt = COMBINE_TILE
    tiles_per_seq = seq // tt
    n_batch = n_lat_rows // seq

    def mod_row(i):
        return jnp.minimum(i // tiles_per_seq, n_batch)

    return pl.pallas_call(
        _combine_kernel,
        grid=(n // tt,),
        in_specs=[pl.BlockSpec(memory_space=pl.ANY), pl.BlockSpec(memory_space=pl.ANY),
                  pl.BlockSpec((tt, TOP_K), lambda i: (i, 0)),
                  pl.BlockSpec((tt, d), lambda i: (i, 0)),
                  pl.BlockSpec((1, 1, d), lambda i: (mod_row(i), 0, 5))],
        out_specs=pl.BlockSpec((tt, d), lambda i: (i, 0)),
        out_shape=jax.ShapeDtypeStruct((n, d), F32),
        scratch_shapes=[pltpu.SMEM((tt * TOP_K,), jnp.int32), pltpu.SMEM((tt * TOP_K,), jnp.int32),
                        pltpu.VMEM((TOP_K * tt * SLABS, LANES), F32),
                        pltpu.VMEM((TOP_K * tt * SLABS, LANES), F32),
                        pltpu.SemaphoreType.DMA((2,)), pltpu.SemaphoreType.DMA((2,))],
        compiler_params=_cparams(1),
        name="combine",
    )(dest_flat, ys, gate, x1, mod3)


def _block_diag(blocks):
    g, a, b = blocks.shape
    eye = jnp.eye(g, dtype=blocks.dtype)
    return (eye[:, None, :, None] * blocks[:, :, None, :]).reshape(g * a, g * b)


def _rope_tables(seq, extra_rows):
    half = DIFF_QK_DIM // 2
    inv = ROPE_THETA ** (-jnp.arange(0, half, 2, dtype=F32) / half)
    pos = jnp.arange(seq)
    c = jnp.arange(GROUP_WIDTH)
    dd = c % DIFF_QK_DIM
    axis_pos = jnp.where((dd < half)[None, :], (pos // GRID_W)[:, None], (pos % GRID_W)[:, None])
    ang = axis_pos.astype(F32) * inv[dd % (half // 2)][None, :]
    sign = jnp.where((dd % half) < half // 2, -1.0, 1.0).astype(F32)
    cos_t = jnp.concatenate([jnp.cos(ang), jnp.ones((extra_rows, GROUP_WIDTH), F32)], axis=0)
    sin_t = jnp.concatenate([jnp.sin(ang) * sign[None, :], jnp.zeros((extra_rows, GROUP_WIDTH), F32)], axis=0)
    return cos_t, sin_t


def _dft_tables(n, dtype):
    k = np.arange(n, dtype=np.int64)
    ang = ((k[:, None] * k[None, :]) % n).astype(np.float64) * (2.0 * math.pi / n)
    return jnp.asarray(np.cos(ang), dtype), jnp.asarray(np.sin(ang), dtype)


def _na_bias_tables(rpb, grid_rows):
    w = GRID_W
    col = np.arange(w)
    col_start = np.clip(col - NA_COLS // 2, 0, w - NA_COLS)
    in_win = (col[None, :] >= col_start[:, None]) & (col[None, :] < col_start[:, None] + NA_COLS)
    rel_c = np.clip(col[None, :] - col[:, None], 1 - NA_COLS, NA_COLS - 1) + NA_COLS - 1
    n_tiles = grid_rows // NA_Q_ROWS
    cases = (0, 1, n_tiles - 1)
    rel_r = np.zeros((3, NA_Q_ROWS, NA_KEY_ROWS), np.int32)
    valid = np.zeros((3, NA_Q_ROWS, NA_KEY_ROWS), bool)
    for ci, tile in enumerate(cases):
        r0 = tile * NA_Q_ROWS
        k0 = int(np.clip(r0 - NA_ROWS // 2, 0, grid_rows - NA_KEY_ROWS))
        for j in range(NA_Q_ROWS):
            start = int(np.clip(r0 + j - NA_ROWS // 2, 0, grid_rows - NA_ROWS))
            for i in range(NA_KEY_ROWS):
                kr = k0 + i
                valid[ci, j, i] = start <= kr < start + NA_ROWS
                rel_r[ci, j, i] = np.clip(kr - (r0 + j) + NA_ROWS - 1, 0, 2 * NA_ROWS - 2)
    del rel_c
    edge = w - NA_COLS
    ext = jnp.pad(rpb.astype(F32), ((0, 0), (0, 0), (edge, edge)), mode='edge')
    t = jnp.stack([ext[:, :, w - 1 - q:2 * w - 1 - q] for q in range(w)], axis=2)
    select = (valid[..., None] & (rel_r[..., None] == np.arange(2 * NA_ROWS - 1))).astype(np.float32)
    b = jnp.einsum('cjir,hrqk->chjqik', select, t, precision=HIGHEST)
    keep = valid[:, None, :, None, :, None] & in_win[None, None, None, :, None, :]
    b = jnp.where(keep, b, MASK_VALUE)
    return b.reshape(3, rpb.shape[0], NA_Q_ROWS * w, NA_KEY_ROWS * w)


def _moe(h2_slabs, logits, x1, mod3, w1, b1p, w2, b2, perm, layer, n_lat_rows, seq):
    n = x1.shape[0]
    blk = EXPERT_BLOCK
    assert n * TOP_K <= RANK_STRIDE
    gate, code, counts = _route_call(logits)
    counts = counts[0, :N_EXPERTS]
    padded = (counts + blk - 1) // blk * blk
    padded_end = jnp.cumsum(padded)
    padded_start = padded_end - padded
    codes = code.reshape(1, n * TOP_K)
    experts = lax.shift_right_logical(codes, RANK_SHIFT)
    first_slot = jnp.sum(jnp.where(experts == jnp.arange(N_EXPERTS, dtype=jnp.int32)[:, None],
                                   padded_start.astype(jnp.int32)[:, None], 0), axis=0)
    dest = first_slot + (codes[0] & RANK_MASK)
    n_blocks = n * TOP_K // blk + N_EXPERTS
    block_row0 = jnp.arange(n_blocks, dtype=jnp.int32) * blk
    block_e = jnp.minimum(jnp.sum((padded_end[None, :] <= block_row0[:, None]).astype(jnp.int32), axis=1),
                          N_EXPERTS - 1).astype(jnp.int32)
    first = jnp.concatenate([jnp.ones((1,), jnp.int32),
                             (block_e[1:] != block_e[:-1]).astype(jnp.int32)])
    n_valid = (padded_end[-1:] // blk).astype(jnp.int32)
    fill_start = (padded_start + counts).astype(jnp.int32)
    pad_len = (padded - counts).astype(jnp.int32)
    hs = _dispatch_call(fill_start, pad_len, dest, h2_slabs, n_blocks * blk)
    ys = _expert_call(block_e, first, n_valid, hs, w1, b1p, w2, b2, perm, layer, n_blocks)
    return _combine_call(dest, ys, gate, x1, mod3, n_lat_rows, seq)


def kernel(x, c, ctx, c_ctx, w_ada, b_ada, g_norm1, g_norm2, w_in, w_out, na_q_gain, na_k_gain, na_rpb, diff_q_gain, diff_k_gain, diff_lambda_q1, diff_lambda_k1, diff_lambda_q2, diff_lambda_k2, diff_subln, pool_w, pool_scale, fft_w, router_w, router_b, moe_w1, moe_b1, moe_w2, moe_b2):
    n_batch, seq, d = x.shape
    ctx_len = ctx.shape[1]
    depth = w_ada.shape[0]
    gw = GROUP_WIDTH
    assert d == D_MODEL and seq % ROW_TILE == 0 and seq % ctx_len == 0 and seq % TOKEN_TILE == 0
    assert seq % COMBINE_TILE == 0 and (n_batch * ctx_len) % COMBINE_TILE == 0
    assert (n_batch * ctx_len) % ROW_TILE == 0 and (seq // GRID_W) >= NA_KEY_ROWS
    n_lat = n_batch * seq
    n_ctx = n_batch * ctx_len

    xa, xb, ctx_blk0 = x.reshape(n_lat, d), ctx.reshape(n_ctx, d), 0
    mod_rows = -(-(n_batch + 1) // SUBLANES) * SUBLANES
    cvec = jnp.zeros((mod_rows, d), F32).at[:n_batch].set(c).at[n_batch].set(c_ctx)
    mod = _ada_call(cvec, w_ada, b_ada)

    cos_t, sin_t = _rope_tables(seq, ROW_TILE)
    ones = lambda w: _block_diag(jnp.ones((gw // w, w, w), BF16))
    g64, g32 = ones(HEAD_DIM), ones(DIFF_QK_DIM)
    cl_lat, sl_lat = _dft_tables(seq, BF16)
    cl_ctx, sl_ctx = _dft_tables(ctx_len, BF16)
    cc, sc = _dft_tables(gw // 4, F32)
    n_grp = fft_w.shape[1]
    cc_bd = _block_diag(jnp.broadcast_to(cc, (n_grp,) + cc.shape)).astype(BF16)
    sc_bd = _block_diag(jnp.broadcast_to(sc, (n_grp,) + sc.shape)).astype(BF16)
    src = np.arange(GLU_BLOCK)
    dst = np.where(src % 2 == 0, src // 2, LANES + src // 2)
    perm_np = np.zeros((GLU_BLOCK, GLU_BLOCK), np.float32)
    perm_np[src, dst] = 1.0
    glu_perm = jnp.asarray(perm_np, BF16)

    tile = lambda v: jnp.tile(v.astype(F32), gw // v.shape[0])

    def in_proj_params(l):
        gains = jnp.stack([tile(na_q_gain[l]) * (HEAD_DIM ** -0.5 * LOG2_E), tile(na_k_gain[l]),
                           tile(diff_q_gain[l]) * (DIFF_QK_DIM ** -0.5 * LOG2_E), tile(diff_k_gain[l])]
                          + [jnp.zeros((gw,), F32)] * 4)
        return (mod[l].reshape(mod_rows, 1, 6 * d), g_norm1[l].reshape(1, d), w_in[l].astype(BF16),
                gains, cos_t, sin_t, g64, g32)

    for l in range(depth):
        ctx_out = l < depth - 1
        lam_init = 0.8 - 0.6 * math.exp(-0.3 * l)
        mod3 = mod[l].reshape(mod_rows, 1, 6 * d)
        qkv, pool_in, fft_in = _in_call(xa, xb, ctx_blk0, n_lat + n_ctx, in_proj_params(l), n_lat, seq)
        lam_vecs = jnp.stack([diff_lambda_q1[l], diff_lambda_k1[l], diff_lambda_q2[l], diff_lambda_k2[l]]
                             + [jnp.zeros_like(diff_lambda_q1[l])] * 4).astype(F32)
        subln = tile(diff_subln[l]).reshape(1, gw)
        bias_tab = _na_bias_tables(na_rpb[l] * LOG2_E, seq // GRID_W)
        pool_bd = _block_diag(pool_w[l]).astype(BF16)
        fftw_bd = _block_diag(fft_w[l]).astype(BF16)
        rw_f32 = jnp.zeros((d, LANES), F32).at[:, :N_EXPERTS].set(router_w[l])
        rw_hi = rw_f32.astype(BF16)
        rw_pad = jnp.concatenate([rw_hi, (rw_f32 - rw_hi.astype(F32)).astype(BF16)], axis=1)
        rb_pad = jnp.full((1, LANES), -jnp.inf, F32).at[0, :N_EXPERTS].set(router_b[l])
        n_e = moe_w1.shape[1]
        b1p = moe_b1[l].reshape(n_e, 2 * D_FF // GLU_BLOCK, LANES, 2).transpose(0, 1, 3, 2)
        b1p = b1p.reshape(n_e, 1, 2 * D_FF)
        b2 = moe_b2[l].reshape(n_e, 1, d)

        rows = n_lat + n_ctx if ctx_out else n_lat
        ya = _na_call(qkv, bias_tab, n_batch, seq, rows, ctx_len)
        yd = _df_call(qkv, lam_vecs, subln, lam_init, n_batch, seq, rows, ctx_len)
        yb = _pool_call(pool_in, pool_bd, pool_scale[l].reshape(1, gw), n_batch, seq, 0, rows)
        yf = _fft_call(fft_in, cl_lat, sl_lat, cc_bd, sc_bd, fftw_bd, n_batch, seq, 0, rows)
        if ctx_out:
            ya, yd = _ctx_attn_call(qkv, lam_vecs, subln, lam_init, ya, yd, n_batch, seq, ctx_len)
            yb = _pool_call(pool_in, pool_bd, pool_scale[l].reshape(1, gw), n_batch, ctx_len,
                            n_lat // ctx_len, rows, prev=yb)
            yf = _fft_call(fft_in, cl_ctx, sl_ctx, cc_bd, sc_bd, fftw_bd, n_batch, ctx_len,
                           n_lat // ctx_len, rows, prev=yf)
        x1, h2_slabs, logits = _out_call(ya, yd, yb, yf, w_out[l].astype(BF16), xa, xb, ctx_blk0, mod3,
                                         g_norm2[l].reshape(1, d), rw_pad, rb_pad, rows, n_lat, seq)
        x2 = _moe(h2_slabs, logits, x1, mod3, moe_w1, b1p, moe_w2, b2, glu_perm, l, n_lat, seq)
        xa, xb, ctx_blk0 = x2, x2, n_lat // ROW_TILE
    return x2[:n_lat].reshape(n_batch, seq, d)
```

```python
import functools
import math

import numpy as np
import jax
import jax.numpy as jnp
from jax import lax
from jax.experimental import pallas as pl
from jax.experimental.pallas import tpu as pltpu

F32 = jnp.float32
BF16 = jnp.bfloat16
HIGHEST = lax.Precision.HIGHEST

D_MODEL = 1024
DEPTH = 2
GRID_W = 64
HEAD_DIM = 64
GROUP_WIDTH = 256
N_HEADS = GROUP_WIDTH // HEAD_DIM
DIFF_QK_DIM = HEAD_DIM // 2
NA_ROWS = 8
NA_COLS = 16
POOL_WINDOWS = (2, 4, 8, 16)
POOL_HALO = max(POOL_WINDOWS) // 2
N_EXPERTS = 32
TOP_K = 4
D_FF = D_MODEL
SWIGLU_ALPHA = 1.702
SWIGLU_LIMIT = 7.0
ROPE_THETA = 10000.0
NORM_EPS = 1e-6
MASK_VALUE = -1e30

LANES = 128
SUBLANES = 8
ROW_TILE = 512
TOKEN_TILE = 512
COMBINE_TILE = 256
EXPERT_BLOCK = 512
Q_TILE = 256
DIFF_Q_TILE = 512
LOG2_E = math.log2(math.e)
NA_Q_ROWS = Q_TILE // GRID_W
NA_KEY_ROWS = NA_ROWS + NA_Q_ROWS - 1
FFT_TILE = 512
SLABS = D_MODEL // LANES
ISSUE_UNROLL = 8
RANK_STRIDE = 1 << 17
RANK_MASK = RANK_STRIDE - 1
RANK_SHIFT = 17
VMEM_LIMIT = 48 * 1024 * 1024
EXPERT_VMEM_LIMIT = 56 * 1024 * 1024


def _cparams(n_axes, vmem_limit=VMEM_LIMIT):
    return pltpu.CompilerParams(dimension_semantics=("arbitrary",) * n_axes,
                                vmem_limit_bytes=vmem_limit)


def _dot(a, b):
    return jnp.dot(a, b, preferred_element_type=F32)


def _dot_nt(a, b):
    return lax.dot_general(a, b, (((1,), (1,)), ((), ())), preferred_element_type=F32)


def _full(shape):
    zeros = (0,) * len(shape)
    return pl.BlockSpec(shape, lambda *_: zeros)


def _ada_kernel(c_ref, w_ref, b_ref, o_ref):
    c = c_ref[...]
    act = c * jax.nn.sigmoid(c)
    o_ref[0] = jnp.dot(act, w_ref[0], precision=HIGHEST, preferred_element_type=F32) + b_ref[0]


def _ada_call(cvec, w_ada, b_ada):
    depth, d, n = w_ada.shape
    r = cvec.shape[0]
    tn = 1024
    return pl.pallas_call(
        _ada_kernel,
        grid=(depth, n // tn),
        in_specs=[_full((r, d)),
                  pl.BlockSpec((1, d, tn), lambda l, j: (l, 0, j)),
                  pl.BlockSpec((1, 1, tn), lambda l, j: (l, 0, j))],
        out_specs=pl.BlockSpec((1, r, tn), lambda l, j: (l, 0, j)),
        out_shape=jax.ShapeDtypeStruct((depth, r, n), F32),
        compiler_params=_cparams(2),
        name="ada_mod",
    )(cvec, w_ada, b_ada.reshape(depth, 1, n))


def _seg_mean_sq(p, gmat_ref, width):
    sq = p * p
    hi = sq.astype(BF16)
    lo = (sq - hi.astype(F32)).astype(BF16)
    g = gmat_ref[...]
    return (_dot(hi, g) + _dot(lo, g)) * (1.0 / width)


def _stream_specs(tm, d, n_lat_tiles, ctx_blk0):
    return [pl.BlockSpec((tm, d), lambda i: (jnp.minimum(i, n_lat_tiles - 1), 0)),
            pl.BlockSpec((tm, d), lambda i: (ctx_blk0 + jnp.maximum(i - n_lat_tiles, 0), 0))]


def _in_kernel(xa_ref, xb_ref, *rest, n_lat_tiles):
    x = jnp.where(pl.program_id(0) < n_lat_tiles, xa_ref[...], xb_ref[...])
    _in_body(x, *rest)


def _in_body(x, sh_ref, sc_ref, g_ref, w_ref, gains_ref, cos_ref, sin_ref, g64_ref, g32_ref,
             qkv_ref, pool_ref, fft_ref):
    ms = jnp.mean(x * x, axis=-1, keepdims=True)
    y = x * lax.rsqrt(ms + NORM_EPS) * g_ref[...]
    h = (y * (1.0 + sc_ref[0]) + sh_ref[0]).astype(BF16)
    gw = GROUP_WIDTH

    def proj(g):
        return _dot(h, w_ref[:, g * gw:(g + 1) * gw])

    def put(g, val):
        qkv_ref[:, g * gw:(g + 1) * gw] = val.astype(BF16)

    def normed(p, gmat_ref, width, row):
        return p * lax.rsqrt(_seg_mean_sq(p, gmat_ref, width) + NORM_EPS) * gains_ref[row:row + 1, :]

    lane = lax.broadcasted_iota(jnp.int32, (1, gw), 1)
    first_half = (lane % 16) < 8

    def rope(p):
        rot = jnp.where(first_half, pltpu.roll(p, gw - 8, 1), pltpu.roll(p, 8, 1))
        return p * cos_ref[...] + rot * sin_ref[...]

    put(0, normed(proj(0), g64_ref, HEAD_DIM, 0))
    put(1, normed(proj(1), g64_ref, HEAD_DIM, 1))
    put(2, proj(2))
    put(3, rope(normed(proj(3), g32_ref, DIFF_QK_DIM, 2)))
    put(4, rope(normed(proj(4), g32_ref, DIFF_QK_DIM, 3)))
    put(5, proj(5))
    pool_ref[...] = proj(6)
    fft_ref[...] = proj(7).astype(BF16)


def _in_proj_io(tm, rows, params, n_lat_rows, seq):
    mod3, g1, w_in_bf, gains, cos_t, sin_t, g64, g32 = params
    d = D_MODEL
    gw = GROUP_WIDTH
    n_lat_tiles = n_lat_rows // tm
    tiles_per_seq = seq // tm
    n_batch = n_lat_rows // seq

    def mod_row(i):
        return jnp.minimum(i // tiles_per_seq, n_batch)

    def tab_row(i):
        return jnp.where(i < n_lat_tiles, i % tiles_per_seq, tiles_per_seq)

    full = lambda shape: pl.BlockSpec(shape, lambda i, *_: (0,) * len(shape))
    in_specs = [pl.BlockSpec((1, 1, d), lambda i, *_: (mod_row(i), 0, 0)),
                pl.BlockSpec((1, 1, d), lambda i, *_: (mod_row(i), 0, 1)),
                full((1, d)), full(w_in_bf.shape), full(gains.shape),
                pl.BlockSpec((tm, gw), lambda i, *_: (tab_row(i), 0)),
                pl.BlockSpec((tm, gw), lambda i, *_: (tab_row(i), 0)),
                full((gw, gw)), full((gw, gw))]
    args = [mod3, mod3, g1, w_in_bf, gains, cos_t, sin_t, g64, g32]
    out_specs = [pl.BlockSpec((tm, 6 * gw), lambda i, *_: (i, 0)),
                 pl.BlockSpec((tm, gw), lambda i, *_: (i, 0)),
                 pl.BlockSpec((tm, gw), lambda i, *_: (i, 0))]
    out_shape = [jax.ShapeDtypeStruct((rows, 6 * gw), BF16),
                 jax.ShapeDtypeStruct((rows, gw), F32),
                 jax.ShapeDtypeStruct((rows, gw), BF16)]
    return in_specs, args, out_specs, out_shape


def _in_call(xa, xb, ctx_blk0, rows, params, n_lat_rows, seq):
    tm = ROW_TILE
    n_lat_tiles = n_lat_rows // tm
    in_specs, args, out_specs, out_shape = _in_proj_io(tm, rows, params, n_lat_rows, seq)
    return pl.pallas_call(
        functools.partial(_in_kernel, n_lat_tiles=n_lat_tiles),
        grid=(rows // tm,),
        in_specs=_stream_specs(tm, D_MODEL, n_lat_tiles, ctx_blk0) + in_specs,
        out_specs=out_specs,
        out_shape=out_shape,
        compiler_params=_cparams(1),
        name="in_proj",
    )(xa, xb, *args)


def _lane_mask(width, start, size):
    lane = lax.broadcasted_iota(jnp.int32, (1, width), 1)
    return (lane >= start) & (lane < start + size)


def _softmax_pv(qm, keys, biases, vals, sum_lane=None):
    scores = []
    for k, bias in zip(keys, biases):
        s = _dot_nt(qm, k)
        scores.append(s if bias is None else s + bias)
    m = scores[0].max(axis=-1, keepdims=True)
    for s in scores[1:]:
        m = jnp.maximum(m, s.max(axis=-1, keepdims=True))
    o = None
    l = None
    for s, v in zip(scores, vals):
        e = jnp.exp2(s - m)
        part = _dot(e.astype(BF16), v)
        o = part if o is None else o + part
        if sum_lane is None:
            ls = e.sum(axis=-1, keepdims=True)
            l = ls if l is None else l + ls
    if sum_lane is not None:
        lane = lax.broadcasted_iota(jnp.int32, (1, o.shape[1]), 1)
        l = jnp.sum(jnp.where(lane == sum_lane, o, 0.0), axis=-1, keepdims=True)
    return o, l


def _na_heads(q, keys, bias_fn, vals):
    gw = GROUP_WIDTH
    acc = jnp.zeros((q.shape[0], gw), F32)
    for h in range(N_HEADS):
        mask = _lane_mask(gw, h * HEAD_DIM, HEAD_DIM)
        qm = jnp.where(mask, q, jnp.zeros_like(q))
        o, l = _softmax_pv(qm, keys, bias_fn(h), vals)
        acc = acc + jnp.where(mask, o / l, 0.0)
    return acc


def _lambda(lam_ref, lam_init):
    lv = lam_ref[...]
    d1 = jnp.sum(lv[0:1, :] * lv[1:2, :], axis=-1, keepdims=True)
    d2 = jnp.sum(lv[2:3, :] * lv[3:4, :], axis=-1, keepdims=True)
    return jnp.exp(d1) - jnp.exp(d2) + lam_init


def _diff_heads(q, keys, vals, lam, subln, lam_init, acc_ref):
    gw = GROUP_WIDTH
    lane = lax.broadcasted_iota(jnp.int32, (1, gw), 1)
    acc_ref[...] = jnp.zeros_like(acc_ref)

    def scores_of(h, part):
        plo = h * HEAD_DIM + part * DIFF_QK_DIM
        qm = jnp.where((lane >= plo) & (lane < plo + DIFF_QK_DIM), q, jnp.zeros_like(q))
        scores = [_dot_nt(qm, k) for k in keys]
        m = scores[0].max(axis=-1, keepdims=True)
        for s in scores[1:]:
            m = jnp.maximum(m, s.max(axis=-1, keepdims=True))
        return scores, m

    def attend(h, scores, m):
        lo = h * HEAD_DIM
        hm = (lane >= lo) & (lane < lo + HEAD_DIM)
        o = None
        for s, v in zip(scores, vals):
            part = _dot(jnp.exp2(s - m).astype(BF16), jnp.where(hm, v, jnp.ones_like(v)))
            o = part if o is None else o + part
        l = jnp.sum(jnp.where(lane == (lo + HEAD_DIM) % gw, o, 0.0), axis=-1, keepdims=True)
        return o / l

    combos = [(h, part) for h in range(N_HEADS) for part in range(2)]
    pending = [scores_of(*combos[0])]
    outs = []
    for c, (h, part) in enumerate(combos):
        if c + 1 < len(combos):
            pending.append(scores_of(*combos[c + 1]))
        outs.append(attend(h, *pending.pop(0)))
        if part == 1:
            lo = h * HEAD_DIM
            hm = (lane >= lo) & (lane < lo + HEAD_DIM)
            a = jnp.where(hm, outs[-2] - lam * outs[-1], 0.0)
            ms = jnp.sum(a * a, axis=-1, keepdims=True) * (1.0 / HEAD_DIM)
            acc_ref[...] += a * lax.rsqrt(ms + NORM_EPS)
    return acc_ref[...] * subln * (1.0 - lam_init)


def _na_kernel(q_ref, k_ref, v_ref, kc_ref, vc_ref, t_ref, o_ref, *, grid_rows):
    tile = pl.program_id(1)
    q_row0 = tile * NA_Q_ROWS
    key_row0 = jnp.clip(q_row0 - NA_ROWS // 2, 0, grid_rows - NA_KEY_ROWS)
    ks = pl.multiple_of(key_row0 * GRID_W, GRID_W)
    n_win = NA_KEY_ROWS * GRID_W
    kwin = k_ref[pl.ds(ks, n_win), :]
    vwin = v_ref[pl.ds(ks, n_win), :]

    def bias_of(h):
        rows = []
        for j in range(NA_Q_ROWS):
            start = jnp.clip(q_row0 + j - NA_ROWS // 2, 0, grid_rows - NA_ROWS)
            blocks = []
            for i in range(NA_KEY_ROWS):
                key_row = key_row0 + i
                in_rows = (key_row >= start) & (key_row < start + NA_ROWS)
                rel = jnp.clip(key_row - (q_row0 + j) + NA_ROWS - 1, 0, 2 * NA_ROWS - 2)
                blocks.append(jnp.where(in_rows, t_ref[h, rel], MASK_VALUE))
            rows.append(jnp.concatenate(blocks, axis=-1))
        return jnp.concatenate(rows, axis=0)

    acc = _na_heads(q_ref[...], [kwin, kc_ref[...]], lambda h: [bias_of(h), None],
                    [vwin, vc_ref[...]])
    o_ref[...] = acc.astype(BF16)


def _na_call(qkv, rel_tab, n_batch, seq, total_rows, ctx_len):
    gw = GROUP_WIDTH
    qt = Q_TILE
    n_q = seq // qt
    grid_rows = seq // GRID_W
    ctx_blk0 = n_batch * seq // ctx_len
    return pl.pallas_call(
        functools.partial(_na_kernel, grid_rows=grid_rows),
        grid=(n_batch, n_q),
        in_specs=[pl.BlockSpec((qt, gw), lambda b, j: (b * n_q + j, 0)),
                  pl.BlockSpec((seq, gw), lambda b, j: (b, 1)),
                  pl.BlockSpec((seq, gw), lambda b, j: (b, 2)),
                  pl.BlockSpec((ctx_len, gw), lambda b, j: (ctx_blk0 + b, 1)),
                  pl.BlockSpec((ctx_len, gw), lambda b, j: (ctx_blk0 + b, 2)),
                  _full(rel_tab.shape)],
        out_specs=pl.BlockSpec((qt, gw), lambda b, j: (b * n_q + j, 0)),
        out_shape=jax.ShapeDtypeStruct((total_rows, gw), BF16),
        compiler_params=_cparams(2),
        name="na_attn",
    )(qkv, qkv, qkv, qkv, qkv, rel_tab)


def _df_kernel(q_ref, k_ref, v_ref, kc_ref, vc_ref, lam_ref, subln_ref, o_ref, acc_ref, *, lam_init):
    lam = _lambda(lam_ref, lam_init)
    acc = _diff_heads(q_ref[...], [kc_ref[...], k_ref[...]], [vc_ref[...], v_ref[...]], lam,
                      subln_ref[...], lam_init, acc_ref)
    o_ref[...] = acc.astype(BF16)


def _df_call(qkv, lam_vecs, subln, lam_init, n_batch, seq, total_rows, ctx_len):
    gw = GROUP_WIDTH
    qt = DIFF_Q_TILE
    n_q = seq // qt
    ctx_blk0 = n_batch * seq // ctx_len
    return pl.pallas_call(
        functools.partial(_df_kernel, lam_init=lam_init),
        grid=(n_batch, n_q),
        in_specs=[pl.BlockSpec((qt, gw), lambda b, j: (b * n_q + j, 3)),
                  pl.BlockSpec((seq, gw), lambda b, j: (b, 4)),
                  pl.BlockSpec((seq, gw), lambda b, j: (b, 5)),
                  pl.BlockSpec((ctx_len, gw), lambda b, j: (ctx_blk0 + b, 4)),
                  pl.BlockSpec((ctx_len, gw), lambda b, j: (ctx_blk0 + b, 5)),
                  _full(lam_vecs.shape),
                  _full(subln.shape)],
        out_specs=pl.BlockSpec((qt, gw), lambda b, j: (b * n_q + j, 0)),
        out_shape=jax.ShapeDtypeStruct((total_rows, gw), BF16),
        scratch_shapes=[pltpu.VMEM((qt, gw), F32)],
        compiler_params=_cparams(2),
        name="diff_attn",
    )(qkv, qkv, qkv, qkv, qkv, lam_vecs, subln)


def _ctx_attn_kernel(qkv_ref, lam_ref, subln_ref, ya_in, yd_in, ya_ref, yd_ref, acc_ref, *, lam_init):
    del ya_in, yd_in
    gw = GROUP_WIDTH
    col = lambda g: qkv_ref[:, g * gw:(g + 1) * gw]
    ya = _na_heads(col(0), [col(1)], lambda h: [None], [col(2)])
    ya_ref[...] = ya.astype(BF16)
    lam = _lambda(lam_ref, lam_init)
    yd = _diff_heads(col(3), [col(4)], [col(5)], lam, subln_ref[...], lam_init, acc_ref)
    yd_ref[...] = yd.astype(BF16)


def _ctx_attn_call(qkv, lam_vecs, subln, lam_init, ya, yd, n_batch, seq, ctx_len):
    gw = GROUP_WIDTH
    ctx_blk0 = n_batch * seq // ctx_len
    any_spec = pl.BlockSpec(memory_space=pl.ANY)
    out_spec = pl.BlockSpec((ctx_len, gw), lambda b: (ctx_blk0 + b, 0))
    return pl.pallas_call(
        functools.partial(_ctx_attn_kernel, lam_init=lam_init),
        grid=(n_batch,),
        in_specs=[pl.BlockSpec((ctx_len, 6 * gw), lambda b: (ctx_blk0 + b, 0)),
                  _full(lam_vecs.shape), _full(subln.shape), any_spec, any_spec],
        out_specs=[out_spec, out_spec],
        out_shape=[jax.ShapeDtypeStruct(ya.shape, BF16), jax.ShapeDtypeStruct(yd.shape, BF16)],
        input_output_aliases={3: 0, 4: 1},
        scratch_shapes=[pltpu.VMEM((ctx_len, gw), F32)],
        compiler_params=_cparams(1),
        name="ctx_attn",
    )(qkv, lam_vecs, subln, ya, yd)


def _pool_kernel(p_ref, w_ref, scale_ref, *rest, seq, aliased):
    o_ref, pad_ref = rest[-2], rest[-1]
    del aliased
    gw = GROUP_WIDTH
    halo = POOL_HALO
    pad_ref[0:halo, :] = jnp.zeros((halo, gw), F32)
    pad_ref[halo + seq:, :] = jnp.zeros((halo, gw), F32)
    pad_ref[halo:halo + seq, :] = p_ref[...]
    chunk = min(seq, 256)
    lane_group = lax.broadcasted_iota(jnp.int32, (1, gw), 1) // (gw // len(POOL_WINDOWS))
    for c0 in range(0, seq, chunk):
        def at(off):
            return pad_ref[halo + c0 + off:halo + c0 + off + chunk, :]
        pos = c0 + lax.broadcasted_iota(jnp.int32, (chunk, 1), 0)
        x = at(0)
        run = x
        mean = None
        lo_done, hi_done = 0, 0
        for g, win in enumerate(POOL_WINDOWS):
            half = win // 2
            for off in range(-half, -lo_done):
                run = run + at(off)
            for off in range(hi_done + 1, half):
                run = run + at(off)
            lo_done, hi_done = half, half - 1
            cnt = (jnp.minimum(pos + half, seq) - jnp.maximum(pos - half, 0)).astype(F32)
            m = run / cnt
            mean = m if mean is None else jnp.where(lane_group == g, m, mean)
        y = _dot((mean - x).astype(BF16), w_ref[...]) * scale_ref[...]
        o_ref[c0:c0 + chunk, :] = y.astype(BF16)


def _pool_call(pool_in, w_bd, scale, n_seq, seq, row_blk0, total_rows, prev=None):
    gw = GROUP_WIDTH
    in_specs = [pl.BlockSpec((seq, gw), lambda b: (row_blk0 + b, 0)), _full((gw, gw)), _full((1, gw))]
    args = [pool_in, w_bd, scale]
    aliases = {}
    if prev is not None:
        in_specs.append(pl.BlockSpec(memory_space=pl.ANY))
        args.append(prev)
        aliases = {3: 0}
    return pl.pallas_call(
        functools.partial(_pool_kernel, seq=seq, aliased=prev is not None),
        grid=(n_seq,),
        in_specs=in_specs,
        out_specs=pl.BlockSpec((seq, gw), lambda b: (row_blk0 + b, 0)),
        out_shape=jax.ShapeDtypeStruct((total_rows, gw), BF16),
        scratch_shapes=[pltpu.VMEM((seq + 2 * POOL_HALO, gw), F32)],
        input_output_aliases=aliases,
        compiler_params=_cparams(1),
        name="pool_mix",
    )(*args)


def _fft_kernel(t_ref, cl_ref, sl_ref, cc_ref, sc_ref, w_ref, *rest, norm):
    o_ref, a_ref, b_ref = rest[-3], rest[-2], rest[-1]

    @pl.when(pl.program_id(1) == 0)
    def _():
        t = t_ref[...]
        a_ref[...] = _dot(t, cc_ref[...]).astype(BF16)
        b_ref[...] = _dot(t, sc_ref[...]).astype(BF16)

    f = (_dot(cl_ref[...], a_ref[...]) - _dot(sl_ref[...], b_ref[...])) * norm
    o_ref[...] = _dot(f.astype(BF16), w_ref[...]).astype(BF16)


def _fft_call(fft_in, cl, sl, cc_bd, sc_bd, w_bd, n_seq, seq, row_blk0, total_rows, prev=None):
    gw = GROUP_WIDTH
    tk = min(FFT_TILE, seq)
    n_k = seq // tk
    in_specs = [pl.BlockSpec((seq, gw), lambda b, k: (row_blk0 + b, 0)),
                pl.BlockSpec((tk, seq), lambda b, k: (k, 0)),
                pl.BlockSpec((tk, seq), lambda b, k: (k, 0)),
                _full((gw, gw)), _full((gw, gw)), _full((gw, gw))]
    args = [fft_in, cl, sl, cc_bd, sc_bd, w_bd]
    aliases = {}
    if prev is not None:
        in_specs.append(pl.BlockSpec(memory_space=pl.ANY))
        args.append(prev)
        aliases = {6: 0}
    norm = 1.0 / math.sqrt(seq * (gw // 4))
    return pl.pallas_call(
        functools.partial(_fft_kernel, norm=norm),
        grid=(n_seq, n_k),
        in_specs=in_specs,
        out_specs=pl.BlockSpec((tk, gw), lambda b, k: ((row_blk0 + b) * n_k + k, 0)),
        out_shape=jax.ShapeDtypeStruct((total_rows, gw), BF16),
        scratch_shapes=[pltpu.VMEM((seq, gw), BF16), pltpu.VMEM((seq, gw), BF16)],
        input_output_aliases=aliases,
        compiler_params=_cparams(2),
        name="fourier_mix",
    )(*args)


def _out_kernel(ya_ref, yd_ref, yb_ref, yf_ref, w_ref, xa_ref, xb_ref, g1_ref, sh2_ref, sc2_ref, gn2_ref,
                rw_ref, rb_ref, x1_ref, h2_ref, lg_ref, *, n_lat_tiles):
    gw = GROUP_WIDTH
    x = jnp.where(pl.program_id(0) < n_lat_tiles, xa_ref[...], xb_ref[...])
    acc = _dot(ya_ref[...], w_ref[0:gw, :])
    acc = acc + _dot(yd_ref[...], w_ref[gw:2 * gw, :])
    acc = acc + _dot(yb_ref[...], w_ref[2 * gw:3 * gw, :])
    acc = acc + _dot(yf_ref[...], w_ref[3 * gw:4 * gw, :])
    x1 = x + g1_ref[0] * acc
    x1_ref[...] = x1
    ms = jnp.mean(x1 * x1, axis=-1, keepdims=True)
    h2 = x1 * lax.rsqrt(ms + NORM_EPS) * gn2_ref[...] * (1.0 + sc2_ref[0]) + sh2_ref[0]
    h_hi = h2.astype(BF16)
    h_lo = (h2 - h_hi.astype(F32)).astype(BF16)
    by_hi = _dot(h_hi, rw_ref[...])
    by_lo = _dot(h_lo, rw_ref[:, :LANES])
    lg_ref[...] = by_hi[:, :LANES] + by_hi[:, LANES:] + by_lo + rb_ref[...]
    tm = x1.shape[0]
    for s in range(SLABS):
        h2_ref[pl.ds(s, tm, stride=SLABS), :] = h2[:, s * LANES:(s + 1) * LANES]


def _out_call(ya, yd, yb, yf, w_out_bf, xa, xb, ctx_blk0, mod3, gn2, rw_pad, rb_pad, rows, n_lat_rows, seq):
    d = D_MODEL
    gw = GROUP_WIDTH
    tm = ROW_TILE
    n_lat_tiles = n_lat_rows // tm
    tiles_per_seq = seq // tm
    n_batch = n_lat_rows // seq

    def mod_row(i):
        return jnp.minimum(i // tiles_per_seq, n_batch)

    mix_spec = pl.BlockSpec((tm, gw), lambda i: (i, 0))
    mod_spec = lambda chunk: pl.BlockSpec((1, 1, d), lambda i: (mod_row(i), 0, chunk))
    return pl.pallas_call(
        functools.partial(_out_kernel, n_lat_tiles=n_lat_tiles),
        grid=(rows // tm,),
        in_specs=[mix_spec, mix_spec, mix_spec, mix_spec, _full((4 * gw, d))]
                 + _stream_specs(tm, d, n_lat_tiles, ctx_blk0) + [
                  mod_spec(2), mod_spec(3), mod_spec(4), _full((1, d)),
                  _full(rw_pad.shape), _full(rb_pad.shape)],
        out_specs=[pl.BlockSpec((tm, d), lambda i: (i, 0)),
                   pl.BlockSpec((tm * SLABS, LANES), lambda i: (i, 0)),
                   pl.BlockSpec((tm, LANES), lambda i: (i, 0))],
        out_shape=[jax.ShapeDtypeStruct((rows, d), F32),
                   jax.ShapeDtypeStruct((rows * SLABS, LANES), F32),
                   jax.ShapeDtypeStruct((rows, LANES), F32)],
        compiler_params=_cparams(1),
        name="out_proj",
    )(ya, yd, yb, yf, w_out_bf, xa, xb, mod3, mod3, mod3, gn2, rw_pad, rb_pad)


def _route_kernel(lg_ref, gate_ref, code_ref, cnt_ref, carry_ref):
    i = pl.program_id(0)

    @pl.when(i == 0)
    def _():
        carry_ref[...] = jnp.zeros_like(carry_ref)

    lg = lg_ref[...]
    tm = lg.shape[0]
    lane = lax.broadcasted_iota(jnp.int32, (tm, LANES), 1)
    vals, idxs = [], []
    onehot = jnp.zeros((tm, LANES), F32)
    for _ in range(TOP_K):
        m = lg.max(axis=-1, keepdims=True)
        idx = jnp.min(jnp.where(lg == m, lane, LANES), axis=-1, keepdims=True)
        sel = lane == idx
        onehot = onehot + sel.astype(F32)
        lg = jnp.where(sel, -jnp.inf, lg)
        vals.append(m)
        idxs.append(idx)
    exps = [jnp.exp(v - vals[0]) for v in vals]
    denom = exps[0] + exps[1] + exps[2] + exps[3]
    r_i = lax.broadcasted_iota(jnp.int32, (tm, tm), 0)
    c_i = lax.broadcasted_iota(jnp.int32, (tm, tm), 1)
    tri = (c_i < r_i).astype(BF16)
    before = _dot(tri, onehot.astype(BF16)) + carry_ref[...]
    col4 = lax.broadcasted_iota(jnp.int32, (tm, TOP_K), 1)
    gate_o = jnp.zeros((tm, TOP_K), F32)
    code_o = jnp.zeros((tm, TOP_K), F32)
    for k in range(TOP_K):
        rk = jnp.sum(jnp.where(lane == idxs[k], before, 0.0), axis=-1, keepdims=True)
        gate_o = jnp.where(col4 == k, exps[k] / denom, gate_o)
        code_o = jnp.where(col4 == k, idxs[k].astype(F32) * float(RANK_STRIDE) + rk, code_o)
    gate_ref[...] = gate_o
    code_ref[...] = code_o.astype(jnp.int32)
    carry_ref[...] = carry_ref[...] + jnp.sum(onehot, axis=0, keepdims=True)
    cnt_ref[...] = carry_ref[...].astype(jnp.int32)


def _route_call(logits):
    n = logits.shape[0]
    tm = ROW_TILE
    k_spec = pl.BlockSpec((tm, TOP_K), lambda i: (i, 0))
    return pl.pallas_call(
        _route_kernel,
        grid=(n // tm,),
        in_specs=[pl.BlockSpec((tm, LANES), lambda i: (i, 0))],
        out_specs=[k_spec, k_spec, _full((1, LANES))],
        out_shape=[jax.ShapeDtypeStruct((n, TOP_K), F32),
                   jax.ShapeDtypeStruct((n, TOP_K), jnp.int32),
                   jax.ShapeDtypeStruct((1, LANES), jnp.int32)],
        scratch_shapes=[pltpu.VMEM((1, LANES), F32)],
        compiler_params=_cparams(1),
        name="route",
    )(logits)


def _row_copy(src_ref, src_row, dst_ref, dst_row, sem):
    return pltpu.make_async_copy(
        src_ref.at[pl.ds(pl.multiple_of(src_row * SLABS, SLABS), SLABS), :],
        dst_ref.at[pl.ds(pl.multiple_of(dst_row * SLABS, SLABS), SLABS), :], sem)


def _pad_copy(zero_ref, hs_ref, start_row, n_rows, sem):
    return pltpu.make_async_copy(
        zero_ref.at[pl.ds(0, n_rows * SLABS), :],
        hs_ref.at[pl.ds(pl.multiple_of(start_row * SLABS, SLABS), n_rows * SLABS), :], sem)


def _dispatch_kernel(fill_ref, pad_ref, dest_hbm, h2_hbm, hs_ref,
                     idx0, idx1, idx2, rows0, rows1, rows2, zero_ref, zsem, isems, lsems, ssems):
    i = pl.program_id(0)
    n_tiles = pl.num_programs(0)
    tt = TOKEN_TILE
    n_assign = tt * TOP_K
    idx_refs, row_refs = (idx0, idx1, idx2), (rows0, rows1, rows2)

    def idx_copy(tile, s):
        return pltpu.make_async_copy(dest_hbm.at[pl.ds(tile * n_assign, n_assign)], idx_refs[s],
                                     isems.at[s])

    def row_load(tile, s):
        return pltpu.make_async_copy(h2_hbm.at[pl.ds(tile * (tt * SLABS), tt * SLABS), :], row_refs[s],
                                     lsems.at[s])

    def wait_sent(s):
        for _ in range(TOP_K):
            pltpu.make_async_copy(row_refs[s], hs_ref.at[pl.ds(0, tt * SLABS), :], ssems.at[s]).wait()

    @pl.when(i == 0)
    def _():
        idx_copy(0, 0).start()
        row_load(0, 0).start()
        zero_ref[...] = jnp.zeros_like(zero_ref)
        bits = [1 << b for b in reversed(range(int(math.log2(EXPERT_BLOCK))))]
        for phase in ("start", "wait"):
            def fill(e, carry, phase=phase):
                pos = fill_ref[e]
                pad = pad_ref[e]
                for bit in bits:
                    @pl.when((pad & bit) != 0)
                    def _(pos=pos, bit=bit):
                        cp = _pad_copy(zero_ref, hs_ref, pos, bit, zsem)
                        cp.start() if phase == "start" else cp.wait()
                    pos = pos + (pad & bit)
                return carry

            lax.fori_loop(0, N_EXPERTS, fill, 0)

    for s in range(3):
        @pl.when(i % 3 == s)
        def _(s=s):
            nxt, prev = (s + 1) % 3, (s + 2) % 3

            @pl.when(i >= 2)
            def _():
                wait_sent(nxt)

            @pl.when(i + 1 < n_tiles)
            def _():
                idx_copy(i + 1, nxt).start()
                row_load(i + 1, nxt).start()

            idx_copy(i, s).wait()
            row_load(i, s).wait()

            def start(t, carry):
                for k in range(TOP_K):
                    _row_copy(row_refs[s], t, hs_ref, idx_refs[s][t * TOP_K + k],
                              ssems.at[s]).start(priority=k % 2)
                return carry

            lax.fori_loop(0, tt, start, 0, unroll=ISSUE_UNROLL)

            @pl.when(i == n_tiles - 1)
            def _():
                @pl.when(i >= 1)
                def _():
                    wait_sent(prev)
                wait_sent(s)


def _dispatch_call(fill_start, pad_len, dest_flat, h2_slabs, n_slots):
    n = h2_slabs.shape[0] // SLABS
    tt = TOKEN_TILE
    any_spec = pl.BlockSpec(memory_space=pl.ANY)
    grid_spec = pltpu.PrefetchScalarGridSpec(
        num_scalar_prefetch=2,
        grid=(n // tt,),
        in_specs=[any_spec, any_spec],
        out_specs=any_spec,
        scratch_shapes=[pltpu.SMEM((tt * TOP_K,), jnp.int32)] * 3
                       + [pltpu.VMEM((tt * SLABS, LANES), F32)] * 3
                       + [pltpu.VMEM((EXPERT_BLOCK // 2 * SLABS, LANES), F32),
                          pltpu.SemaphoreType.DMA, pltpu.SemaphoreType.DMA((3,)),
                          pltpu.SemaphoreType.DMA((3,)), pltpu.SemaphoreType.DMA((3,))],
    )
    return pl.pallas_call(
        _dispatch_kernel,
        grid_spec=grid_spec,
        out_shape=jax.ShapeDtypeStruct((n_slots * SLABS, LANES), F32),
        compiler_params=_cparams(1),
        name="dispatch",
    )(fill_start, pad_len, dest_flat, h2_slabs)


GLU_BLOCK = 2 * LANES


def _expert_kernel(be_ref, first_ref, nv_ref, hs_ref, w1_ref, b1_ref, w2_ref, b2_ref, perm_ref,
                   ys_ref, h_ref, act_ref, w1p_ref, w2p_ref):
    del be_ref
    i = pl.program_id(0)
    n_glu = 2 * D_FF // GLU_BLOCK

    @pl.when((first_ref[i] != 0) & (i < nv_ref[0]))
    def _():
        for b in range(n_glu):
            cols = slice(b * GLU_BLOCK, (b + 1) * GLU_BLOCK)
            w1p_ref[:, cols] = _dot(w1_ref[0, :, cols].astype(BF16), perm_ref[...]).astype(BF16)
        w2p_ref[...] = w2_ref[0].astype(BF16)

    @pl.when(i < nv_ref[0])
    def _():
        blk = EXPERT_BLOCK
        for s in range(SLABS):
            h_ref[:, s * LANES:(s + 1) * LANES] = hs_ref[pl.ds(s, blk, stride=SLABS), :].astype(BF16)
        h = h_ref[...]
        for b in range(n_glu):
            cols = slice(b * GLU_BLOCK, (b + 1) * GLU_BLOCK)
            u = _dot(h, w1p_ref[:, cols]) + b1_ref[0, :, cols]
            glu = jnp.minimum(u[:, :LANES], SWIGLU_LIMIT)
            lin = jnp.clip(u[:, LANES:], -SWIGLU_LIMIT, SWIGLU_LIMIT)
            act = glu * jax.nn.sigmoid(SWIGLU_ALPHA * glu) * (lin + 1.0)
            act_ref[:, b * LANES:(b + 1) * LANES] = act.astype(BF16)
        y = _dot(act_ref[...], w2p_ref[...]) + b2_ref[0]
        for s in range(SLABS):
            ys_ref[pl.ds(s, blk, stride=SLABS), :] = y[:, s * LANES:(s + 1) * LANES]


def _expert_call(block_e, first, n_valid, hs, w1, b1p, w2, b2, perm, layer, n_blocks):
    blk = EXPERT_BLOCK
    d = D_MODEL

    def row_blk(i, be, first, nv):
        return (jnp.minimum(i, nv[0] - 1), 0)

    def expert_blk(i, be, first, nv):
        return (be[i], 0, 0)

    def weight_blk(i, be, first, nv):
        return (layer, be[i], 0, 0)

    grid_spec = pltpu.PrefetchScalarGridSpec(
        num_scalar_prefetch=3,
        grid=(n_blocks,),
        in_specs=[pl.BlockSpec((blk * SLABS, LANES), row_blk),
                  pl.BlockSpec((None, 1, d, 2 * D_FF), weight_blk),
                  pl.BlockSpec((1, 1, 2 * D_FF), expert_blk),
                  pl.BlockSpec((None, 1, D_FF, d), weight_blk),
                  pl.BlockSpec((1, 1, d), expert_blk),
                  pl.BlockSpec((GLU_BLOCK, GLU_BLOCK), lambda i, *_: (0, 0))],
        out_specs=pl.BlockSpec((blk * SLABS, LANES), row_blk),
        scratch_shapes=[pltpu.VMEM((blk, d), BF16), pltpu.VMEM((blk, D_FF), BF16),
                        pltpu.VMEM((d, 2 * D_FF), BF16), pltpu.VMEM((D_FF, d), BF16)],
    )
    return pl.pallas_call(
        _expert_kernel,
        grid_spec=grid_spec,
        out_shape=jax.ShapeDtypeStruct(hs.shape, F32),
        compiler_params=_cparams(1, EXPERT_VMEM_LIMIT),
        name="expert_ffn",
    )(block_e, first, n_valid, hs, w1, b1p, w2, b2, perm)


def _combine_kernel(dest_hbm, ys_hbm, gate_ref, x_ref, g2_ref, o_ref,
                    idx0, idx1, buf0, buf1, sems, isems):
    i = pl.program_id(0)
    n_tiles = pl.num_programs(0)
    tt = COMBINE_TILE
    n_assign = tt * TOP_K
    idx_refs, buf_refs = (idx0, idx1), (buf0, buf1)

    def idx_copy(tile, s):
        return pltpu.make_async_copy(dest_hbm.at[pl.ds(tile * n_assign, n_assign)], idx_refs[s],
                                     isems.at[s])

    def issue(s):
        def start(t, carry):
            for k in range(TOP_K):
                _row_copy(ys_hbm, idx_refs[s][t * TOP_K + k], buf_refs[s], k * tt + t,
                          sems.at[s]).start(priority=k % 2)
            return carry
        lax.fori_loop(0, tt, start, 0, unroll=ISSUE_UNROLL)

    @pl.when(i == 0)
    def _():
        idx_copy(0, 0).start()
        idx_copy(0, 0).wait()
        issue(0)

        @pl.when(n_tiles > 1)
        def _():
            idx_copy(1, 1).start()

    for s in (0, 1):
        @pl.when(i % 2 == s)
        def _(s=s):
            @pl.when(i + 1 < n_tiles)
            def _():
                idx_copy(i + 1, 1 - s).wait()
                issue(1 - s)

            pltpu.make_async_copy(ys_hbm.at[pl.ds(0, n_assign * SLABS), :], buf_refs[s], sems.at[s]).wait()

            @pl.when(i + 2 < n_tiles)
            def _():
                idx_copy(i + 2, s).start()

            gate = gate_ref[...]
            g2 = g2_ref[0]
            for sl in range(SLABS):
                cols = slice(sl * LANES, (sl + 1) * LANES)
                y = jnp.zeros((tt, LANES), F32)
                for k in range(TOP_K):
                    y = y + gate[:, k:k + 1] * buf_refs[s][pl.ds(k * tt * SLABS + sl, tt, stride=SLABS), :]
                o_ref[:, cols] = x_ref[:, cols] + g2[:, cols] * y


def _combine_call(dest_flat, ys, gate, x1, mod3, n_lat_rows, seq):
    n, d = x1.shape
    tt = COMBINE_TILE
    tiles_per_seq = seq // tt
    n_batch = n_lat_rows // seq

    def mod_row(i):
        return jnp.minimum(i // tiles_per_seq, n_batch)

    return pl.pallas_call(
        _combine_kernel,
        grid=(n // tt,),
        in_specs=[pl.BlockSpec(memory_space=pl.ANY), pl.BlockSpec(memory_space=pl.ANY),
                  pl.BlockSpec((tt, TOP_K), lambda i: (i, 0)),
                  pl.BlockSpec((tt, d), lambda i: (i, 0)),
                  pl.BlockSpec((1, 1, d), lambda i: (mod_row(i), 0, 5))],
        out_specs=pl.BlockSpec((tt, d), lambda i: (i, 0)),
        out_shape=jax.ShapeDtypeStruct((n, d), F32),
        scratch_shapes=[pltpu.SMEM((tt * TOP_K,), jnp.int32), pltpu.SMEM((tt * TOP_K,), jnp.int32),
                        pltpu.VMEM((TOP_K * tt * SLABS, LANES), F32),
                        pltpu.VMEM((TOP_K * tt * SLABS, LANES), F32),
                        pltpu.SemaphoreType.DMA((2,)), pltpu.SemaphoreType.DMA((2,))],
        compiler_params=_cparams(1),
        name="combine",
    )(dest_flat, ys, gate, x1, mod3)


def _block_diag(blocks):
    g, a, b = blocks.shape
    eye = jnp.eye(g, dtype=blocks.dtype)
    return (eye[:, None, :, None] * blocks[:, :, None, :]).reshape(g * a, g * b)


def _rope_tables(seq, extra_rows):
    half = DIFF_QK_DIM // 2
    inv = ROPE_THETA ** (-jnp.arange(0, half, 2, dtype=F32) / half)
    pos = jnp.arange(seq)
    c = jnp.arange(GROUP_WIDTH)
    dd = c % DIFF_QK_DIM
    axis_pos = jnp.where((dd < half)[None, :], (pos // GRID_W)[:, None], (pos % GRID_W)[:, None])
    ang = axis_pos.astype(F32) * inv[dd % (half // 2)][None, :]
    sign = jnp.where((dd % half) < half // 2, -1.0, 1.0).astype(F32)
    cos_t = jnp.concatenate([jnp.cos(ang), jnp.ones((extra_rows, GROUP_WIDTH), F32)], axis=0)
    sin_t = jnp.concatenate([jnp.sin(ang) * sign[None, :], jnp.zeros((extra_rows, GROUP_WIDTH), F32)], axis=0)
    return cos_t, sin_t


def _dft_tables(n, dtype):
    k = np.arange(n, dtype=np.int64)
    ang = ((k[:, None] * k[None, :]) % n).astype(np.float64) * (2.0 * math.pi / n)
    return jnp.asarray(np.cos(ang), dtype), jnp.asarray(np.sin(ang), dtype)


def _na_rel_tables(rpb):
    w = GRID_W
    col = np.arange(w)
    col_start = np.clip(col - NA_COLS // 2, 0, w - NA_COLS)
    in_win = (col[None, :] >= col_start[:, None]) & (col[None, :] < col_start[:, None] + NA_COLS)
    edge = w - NA_COLS
    ext = jnp.pad(rpb.astype(F32), ((0, 0), (0, 0), (edge, edge)), mode='edge')
    t = jnp.stack([ext[:, :, w - 1 - q:2 * w - 1 - q] for q in range(w)], axis=2)
    return jnp.where(in_win[None, None], t, MASK_VALUE)


def _moe(h2_slabs, logits, x1, mod3, w1, b1p, w2, b2, perm, layer, n_lat_rows, seq):
    n = x1.shape[0]
    blk = EXPERT_BLOCK
    assert n * TOP_K <= RANK_STRIDE
    gate, code, counts = _route_call(logits)
    counts = counts[0, :N_EXPERTS]
    padded = (counts + blk - 1) // blk * blk
    padded_end = jnp.cumsum(padded)
    padded_start = padded_end - padded
    codes = code.reshape(1, n * TOP_K)
    experts = lax.shift_right_logical(codes, RANK_SHIFT)
    first_slot = jnp.sum(jnp.where(experts == jnp.arange(N_EXPERTS, dtype=jnp.int32)[:, None],
                                   padded_start.astype(jnp.int32)[:, None], 0), axis=0)
    dest = first_slot + (codes[0] & RANK_MASK)
    n_blocks = n * TOP_K // blk + N_EXPERTS
    block_row0 = jnp.arange(n_blocks, dtype=jnp.int32) * blk
    block_e = jnp.minimum(jnp.sum((padded_end[None, :] <= block_row0[:, None]).astype(jnp.int32), axis=1),
                          N_EXPERTS - 1).astype(jnp.int32)
    first = jnp.concatenate([jnp.ones((1,), jnp.int32),
                             (block_e[1:] != block_e[:-1]).astype(jnp.int32)])
    n_valid = (padded_end[-1:] // blk).astype(jnp.int32)
    fill_start = (padded_start + counts).astype(jnp.int32)
    pad_len = (padded - counts).astype(jnp.int32)
    hs = _dispatch_call(fill_start, pad_len, dest, h2_slabs, n_blocks * blk)
    ys = _expert_call(block_e, first, n_valid, hs, w1, b1p, w2, b2, perm, layer, n_blocks)
    return _combine_call(dest, ys, gate, x1, mod3, n_lat_rows, seq)


def kernel(x, c, ctx, c_ctx, w_ada, b_ada, g_norm1, g_norm2, w_in, w_out, na_q_gain, na_k_gain, na_rpb, diff_q_gain, diff_k_gain, diff_lambda_q1, diff_lambda_k1, diff_lambda_q2, diff_lambda_k2, diff_subln, pool_w, pool_scale, fft_w, router_w, router_b, moe_w1, moe_b1, moe_w2, moe_b2):
    n_batch, seq, d = x.shape
    ctx_len = ctx.shape[1]
    depth = w_ada.shape[0]
    gw = GROUP_WIDTH
    assert d == D_MODEL and seq % ROW_TILE == 0 and seq % ctx_len == 0 and seq % TOKEN_TILE == 0
    assert seq % COMBINE_TILE == 0 and (n_batch * ctx_len) % COMBINE_TILE == 0
    assert (n_batch * ctx_len) % ROW_TILE == 0 and (seq // GRID_W) >= NA_KEY_ROWS
    n_lat = n_batch * seq
    n_ctx = n_batch * ctx_len

    xa, xb, ctx_blk0 = x.reshape(n_lat, d), ctx.reshape(n_ctx, d), 0
    mod_rows = -(-(n_batch + 1) // SUBLANES) * SUBLANES
    cvec = jnp.zeros((mod_rows, d), F32).at[:n_batch].set(c).at[n_batch].set(c_ctx)
    mod = _ada_call(cvec, w_ada, b_ada)

    cos_t, sin_t = _rope_tables(seq, ROW_TILE)
    ones = lambda w: _block_diag(jnp.ones((gw // w, w, w), BF16))
    g64, g32 = ones(HEAD_DIM), ones(DIFF_QK_DIM)
    cl_lat, sl_lat = _dft_tables(seq, BF16)
    cl_ctx, sl_ctx = _dft_tables(ctx_len, BF16)
    cc, sc = _dft_tables(gw // 4, F32)
    n_grp = fft_w.shape[1]
    cc_bd = _block_diag(jnp.broadcast_to(cc, (n_grp,) + cc.shape)).astype(BF16)
    sc_bd = _block_diag(jnp.broadcast_to(sc, (n_grp,) + sc.shape)).astype(BF16)
    src = np.arange(GLU_BLOCK)
    dst = np.where(src % 2 == 0, src // 2, LANES + src // 2)
    perm_np = np.zeros((GLU_BLOCK, GLU_BLOCK), np.float32)
    perm_np[src, dst] = 1.0
    glu_perm = jnp.asarray(perm_np, BF16)

    tile = lambda v: jnp.tile(v.astype(F32), gw // v.shape[0])

    def in_proj_params(l):
        gains = jnp.stack([tile(na_q_gain[l]) * (HEAD_DIM ** -0.5 * LOG2_E), tile(na_k_gain[l]),
                           tile(diff_q_gain[l]) * (DIFF_QK_DIM ** -0.5 * LOG2_E), tile(diff_k_gain[l])]
                          + [jnp.zeros((gw,), F32)] * 4)
        return (mod[l].reshape(mod_rows, 1, 6 * d), g_norm1[l].reshape(1, d), w_in[l].astype(BF16),
                gains, cos_t, sin_t, g64, g32)

    for l in range(depth):
        ctx_out = l < depth - 1
        lam_init = 0.8 - 0.6 * math.exp(-0.3 * l)
        mod3 = mod[l].reshape(mod_rows, 1, 6 * d)
        qkv, pool_in, fft_in = _in_call(xa, xb, ctx_blk0, n_lat + n_ctx, in_proj_params(l), n_lat, seq)
        lam_vecs = jnp.stack([diff_lambda_q1[l], diff_lambda_k1[l], diff_lambda_q2[l], diff_lambda_k2[l]]
                             + [jnp.zeros_like(diff_lambda_q1[l])] * 4).astype(F32)
        subln = tile(diff_subln[l]).reshape(1, gw)
        rel_tab = _na_rel_tables(na_rpb[l] * LOG2_E)
        pool_bd = _block_diag(pool_w[l]).astype(BF16)
        fftw_bd = _block_diag(fft_w[l]).astype(BF16)
        rw_f32 = jnp.zeros((d, LANES), F32).at[:, :N_EXPERTS].set(router_w[l])
        rw_hi = rw_f32.astype(BF16)
        rw_pad = jnp.concatenate([rw_hi, (rw_f32 - rw_hi.astype(F32)).astype(BF16)], axis=1)
        rb_pad = jnp.full((1, LANES), -jnp.inf, F32).at[0, :N_EXPERTS].set(router_b[l])
        n_e = moe_w1.shape[1]
        b1p = moe_b1[l].reshape(n_e, 2 * D_FF // GLU_BLOCK, LANES, 2).transpose(0, 1, 3, 2)
        b1p = b1p.reshape(n_e, 1, 2 * D_FF)
        b2 = moe_b2[l].reshape(n_e, 1, d)

        rows = n_lat + n_ctx if ctx_out else n_lat
        ya = _na_call(qkv, rel_tab, n_batch, seq, rows, ctx_len)
        yd = _df_call(qkv, lam_vecs, subln, lam_init, n_batch, seq, rows, ctx_len)
        yb = _pool_call(pool_in, pool_bd, pool_scale[l].reshape(1, gw), n_batch, seq, 0, rows)
        yf = _fft_call(fft_in, cl_lat, sl_lat, cc_bd, sc_bd, fftw_bd, n_batch, seq, 0, rows)
        if ctx_out:
            ya, yd = _ctx_attn_call(qkv, lam_vecs, subln, lam_init, ya, yd, n_batch, seq, ctx_len)
            yb = _pool_call(pool_in, pool_bd, pool_scale[l].reshape(1, gw), n_batch, ctx_len,
                            n_lat // ctx_len, rows, prev=yb)
            yf = _fft_call(fft_in, cl_ctx, sl_ctx, cc_bd, sc_bd, fftw_bd, n_batch, ctx_len,
                           n_lat // ctx_len, rows, prev=yf)
        x1, h2_slabs, logits = _out_call(ya, yd, yb, yf, w_out[l].astype(BF16), xa, xb, ctx_blk0, mod3,
                                         g_norm2[l].reshape(1, d), rw_pad, rb_pad, rows, n_lat, seq)
        x2 = _moe(h2_slabs, logits, x1, mod3, moe_w1, b1p, moe_w2, b2, glu_perm, l, n_lat, seq)
        xa, xb, ctx_blk0 = x2, x2, n_lat // ROW_TILE
    return x2[:n_lat].reshape(n_batch, seq, d)
```

```python
import functools
import math

import numpy as np
import jax
import jax.numpy as jnp
from jax import lax
from jax.experimental import pallas as pl
from jax.experimental.pallas import tpu as pltpu

F32 = jnp.float32
BF16 = jnp.bfloat16
HIGHEST = lax.Precision.HIGHEST

D_MODEL = 1024
DEPTH = 2
GRID_W = 64
HEAD_DIM = 64
GROUP_WIDTH = 256
N_HEADS = GROUP_WIDTH // HEAD_DIM
DIFF_QK_DIM = HEAD_DIM // 2
ROPE_SEG = DIFF_QK_DIM // 2
NA_ROWS = 8
NA_COLS = 16
POOL_WINDOWS = (2, 4, 8, 16)
POOL_HALO = max(POOL_WINDOWS) // 2
N_EXPERTS = 32
TOP_K = 4
D_FF = D_MODEL
SWIGLU_ALPHA = 1.702
SWIGLU_LIMIT = 7.0
ROPE_THETA = 10000.0
NORM_EPS = 1e-6
MASK_VALUE = -1e30

LANES = 128
SUBLANES = 8
ROW_TILE = 512
TOKEN_TILE = 512
COMBINE_TILE = 256
EXPERT_BLOCK = 512
Q_TILE = 256
DIFF_Q_TILE = 512
LOG2_E = math.log2(math.e)
NA_Q_ROWS = Q_TILE // GRID_W
NA_KEY_ROWS = NA_ROWS + NA_Q_ROWS - 1
FFT_TILE = 512
SLABS = D_MODEL // LANES
ISSUE_UNROLL = 16
RANK_STRIDE = 1 << 17
RANK_MASK = RANK_STRIDE - 1
RANK_SHIFT = 17
VMEM_LIMIT = 48 * 1024 * 1024
EXPERT_VMEM_LIMIT = 56 * 1024 * 1024


def _cparams(n_axes, vmem_limit=VMEM_LIMIT):
    return pltpu.CompilerParams(dimension_semantics=("arbitrary",) * n_axes,
                                vmem_limit_bytes=vmem_limit)


def _dot(a, b):
    return jnp.dot(a, b, preferred_element_type=F32)


def _dot_nt(a, b):
    return lax.dot_general(a, b, (((1,), (1,)), ((), ())), preferred_element_type=F32)


def _full(shape):
    zeros = (0,) * len(shape)
    return pl.BlockSpec(shape, lambda *_: zeros)


def _ada_kernel(c_ref, w_ref, b_ref, o_ref):
    c = c_ref[...]
    act = c * jax.nn.sigmoid(c)
    o_ref[0] = jnp.dot(act, w_ref[0], precision=HIGHEST, preferred_element_type=F32) + b_ref[0]


def _ada_call(cvec, w_ada, b_ada):
    depth, d, n = w_ada.shape
    r = cvec.shape[0]
    tn = 1024
    return pl.pallas_call(
        _ada_kernel,
        grid=(depth, n // tn),
        in_specs=[_full((r, d)),
                  pl.BlockSpec((1, d, tn), lambda l, j: (l, 0, j)),
                  pl.BlockSpec((1, 1, tn), lambda l, j: (l, 0, j))],
        out_specs=pl.BlockSpec((1, r, tn), lambda l, j: (l, 0, j)),
        out_shape=jax.ShapeDtypeStruct((depth, r, n), F32),
        compiler_params=_cparams(2),
        name="ada_mod",
    )(cvec, w_ada, b_ada.reshape(depth, 1, n))


def _seg_mean_sq(p, gmat_ref, width):
    sq = p * p
    hi = sq.astype(BF16)
    lo = (sq - hi.astype(F32)).astype(BF16)
    g = gmat_ref[...]
    return (_dot(hi, g) + _dot(lo, g)) * (1.0 / width)


def _stream_specs(tm, d, n_lat_tiles, ctx_blk0):
    return [pl.BlockSpec((tm, d), lambda i: (jnp.minimum(i, n_lat_tiles - 1), 0)),
            pl.BlockSpec((tm, d), lambda i: (ctx_blk0 + jnp.maximum(i - n_lat_tiles, 0), 0))]


def _in_kernel(xa_ref, xb_ref, *rest, n_lat_tiles):
    x = jnp.where(pl.program_id(0) < n_lat_tiles, xa_ref[...], xb_ref[...])
    _in_body(x, *rest)


def _in_body(x, sh_ref, sc_ref, g_ref, w_ref, gains_ref, cos_ref, sin_ref, g64_ref, g32_ref,
             qkv_ref, pool_ref, fft_ref):
    ms = jnp.mean(x * x, axis=-1, keepdims=True)
    y = x * lax.rsqrt(ms + NORM_EPS) * g_ref[...]
    h = (y * (1.0 + sc_ref[0]) + sh_ref[0]).astype(BF16)
    gw = GROUP_WIDTH

    def proj(g):
        return _dot(h, w_ref[:, g * gw:(g + 1) * gw])

    def put(g, val):
        qkv_ref[:, g * gw:(g + 1) * gw] = val.astype(BF16)

    def normed(p, gmat_ref, width, row):
        return p * lax.rsqrt(_seg_mean_sq(p, gmat_ref, width) + NORM_EPS) * gains_ref[row:row + 1, :]

    lane = lax.broadcasted_iota(jnp.int32, (1, gw), 1)
    first_half = (lane % ROPE_SEG) < ROPE_SEG // 2

    def rope(p):
        rot = jnp.where(first_half, pltpu.roll(p, gw - ROPE_SEG // 2, 1), pltpu.roll(p, ROPE_SEG // 2, 1))
        return p * cos_ref[...] + rot * sin_ref[...]

    put(0, normed(proj(0), g64_ref, HEAD_DIM, 0))
    put(1, normed(proj(1), g64_ref, HEAD_DIM, 1))
    put(2, proj(2))
    put(3, rope(normed(proj(3), g32_ref, DIFF_QK_DIM, 2)))
    put(4, rope(normed(proj(4), g32_ref, DIFF_QK_DIM, 3)))
    put(5, proj(5))
    pool_ref[...] = proj(6)
    fft_ref[...] = proj(7).astype(BF16)


def _in_proj_io(tm, rows, params, n_lat_rows, seq):
    mod3, g1, w_in_bf, gains, cos_t, sin_t, g64, g32 = params
    d = D_MODEL
    gw = GROUP_WIDTH
    n_lat_tiles = n_lat_rows // tm
    tiles_per_seq = seq // tm
    n_batch = n_lat_rows // seq

    def mod_row(i):
        return jnp.minimum(i // tiles_per_seq, n_batch)

    def tab_row(i):
        return jnp.where(i < n_lat_tiles, i % tiles_per_seq, tiles_per_seq)

    full = lambda shape: pl.BlockSpec(shape, lambda i, *_: (0,) * len(shape))
    in_specs = [pl.BlockSpec((1, 1, d), lambda i, *_: (mod_row(i), 0, 0)),
                pl.BlockSpec((1, 1, d), lambda i, *_: (mod_row(i), 0, 1)),
                full((1, d)), full(w_in_bf.shape), full(gains.shape),
                pl.BlockSpec((tm, gw), lambda i, *_: (tab_row(i), 0)),
                pl.BlockSpec((tm, gw), lambda i, *_: (tab_row(i), 0)),
                full((gw, gw)), full((gw, gw))]
    args = [mod3, mod3, g1, w_in_bf, gains, cos_t, sin_t, g64, g32]
    out_specs = [pl.BlockSpec((tm, 6 * gw), lambda i, *_: (i, 0)),
                 pl.BlockSpec((tm, gw), lambda i, *_: (i, 0)),
                 pl.BlockSpec((tm, gw), lambda i, *_: (i, 0))]
    out_shape = [jax.ShapeDtypeStruct((rows, 6 * gw), BF16),
                 jax.ShapeDtypeStruct((rows, gw), F32),
                 jax.ShapeDtypeStruct((rows, gw), BF16)]
    return in_specs, args, out_specs, out_shape


def _in_call(xa, xb, ctx_blk0, rows, params, n_lat_rows, seq):
    tm = ROW_TILE
    n_lat_tiles = n_lat_rows // tm
    in_specs, args, out_specs, out_shape = _in_proj_io(tm, rows, params, n_lat_rows, seq)
    return pl.pallas_call(
        functools.partial(_in_kernel, n_lat_tiles=n_lat_tiles),
        grid=(rows // tm,),
        in_specs=_stream_specs(tm, D_MODEL, n_lat_tiles, ctx_blk0) + in_specs,
        out_specs=out_specs,
        out_shape=out_shape,
        compiler_params=_cparams(1),
        name="in_proj",
    )(xa, xb, *args)


def _lane_mask(width, start, size):
    lane = lax.broadcasted_iota(jnp.int32, (1, width), 1)
    return (lane >= start) & (lane < start + size)


def _softmax_pv(qm, keys, biases, vals, sum_lane=None):
    scores = []
    for k, bias in zip(keys, biases):
        s = _dot_nt(qm, k)
        scores.append(s if bias is None else s + bias)
    m = scores[0].max(axis=-1, keepdims=True)
    for s in scores[1:]:
        m = jnp.maximum(m, s.max(axis=-1, keepdims=True))
    o = None
    l = None
    for s, v in zip(scores, vals):
        e = jnp.exp2(s - m)
        part = _dot(e.astype(BF16), v)
        o = part if o is None else o + part
        if sum_lane is None:
            ls = e.sum(axis=-1, keepdims=True)
            l = ls if l is None else l + ls
    if sum_lane is not None:
        lane = lax.broadcasted_iota(jnp.int32, (1, o.shape[1]), 1)
        l = jnp.sum(jnp.where(lane == sum_lane, o, 0.0), axis=-1, keepdims=True)
    return o, l


def _na_heads(q, keys, bias_fn, vals):
    gw = GROUP_WIDTH
    acc = jnp.zeros((q.shape[0], gw), F32)
    for h in range(N_HEADS):
        mask = _lane_mask(gw, h * HEAD_DIM, HEAD_DIM)
        qm = jnp.where(mask, q, jnp.zeros_like(q))
        o, l = _softmax_pv(qm, keys, bias_fn(h), vals)
        acc = acc + jnp.where(mask, o / l, 0.0)
    return acc


def _lambda(lam_ref, lam_init):
    lv = lam_ref[...]
    d1 = jnp.sum(lv[0:1, :] * lv[1:2, :], axis=-1, keepdims=True)
    d2 = jnp.sum(lv[2:3, :] * lv[3:4, :], axis=-1, keepdims=True)
    return jnp.exp(d1) - jnp.exp(d2) + lam_init


def _diff_heads(q, keys, vals, lam, subln, lam_init, acc_ref):
    gw = GROUP_WIDTH
    lane = lax.broadcasted_iota(jnp.int32, (1, gw), 1)
    acc_ref[...] = jnp.zeros_like(acc_ref)

    def scores_of(h, part):
        plo = h * HEAD_DIM + part * DIFF_QK_DIM
        qm = jnp.where((lane >= plo) & (lane < plo + DIFF_QK_DIM), q, jnp.zeros_like(q))
        scores = [_dot_nt(qm, k) for k in keys]
        m = scores[0].max(axis=-1, keepdims=True)
        for s in scores[1:]:
            m = jnp.maximum(m, s.max(axis=-1, keepdims=True))
        return scores, m

    def attend(h, scores, m):
        lo = h * HEAD_DIM
        hm = (lane >= lo) & (lane < lo + HEAD_DIM)
        o = None
        for s, v in zip(scores, vals):
            part = _dot(jnp.exp2(s - m).astype(BF16), jnp.where(hm, v, jnp.ones_like(v)))
            o = part if o is None else o + part
        l = jnp.sum(jnp.where(lane == (lo + HEAD_DIM) % gw, o, 0.0), axis=-1, keepdims=True)
        return o / l

    combos = [(h, part) for h in range(N_HEADS) for part in range(2)]
    pending = [scores_of(*combos[0])]
    outs = []
    for c, (h, part) in enumerate(combos):
        if c + 1 < len(combos):
            pending.append(scores_of(*combos[c + 1]))
        outs.append(attend(h, *pending.pop(0)))
        if part == 1:
            lo = h * HEAD_DIM
            hm = (lane >= lo) & (lane < lo + HEAD_DIM)
            a = jnp.where(hm, outs[-2] - lam * outs[-1], 0.0)
            ms = jnp.sum(a * a, axis=-1, keepdims=True) * (1.0 / HEAD_DIM)
            acc_ref[...] += a * lax.rsqrt(ms + NORM_EPS)
    return acc_ref[...] * subln * (1.0 - lam_init)


def _na_kernel(q_ref, k_ref, v_ref, kc_ref, vc_ref, t_ref, o_ref, *, grid_rows):
    tile = pl.program_id(1)
    q_row0 = tile * NA_Q_ROWS
    key_row0 = jnp.clip(q_row0 - NA_ROWS // 2, 0, grid_rows - NA_KEY_ROWS)
    ks = pl.multiple_of(key_row0 * GRID_W, GRID_W)
    n_win = NA_KEY_ROWS * GRID_W
    kwin = k_ref[pl.ds(ks, n_win), :]
    vwin = v_ref[pl.ds(ks, n_win), :]

    def bias_of(h):
        left = lax.broadcasted_iota(jnp.int32, (1, 2 * GRID_W), 1) < GRID_W
        rows = []
        for j in range(NA_Q_ROWS):
            start = jnp.clip(q_row0 + j - NA_ROWS // 2, 0, grid_rows - NA_ROWS)
            in_rows = lambda key_row: (key_row >= start) & (key_row < start + NA_ROWS)
            blocks = []
            for i in range(0, NA_KEY_ROWS, 2):
                key_row = key_row0 + i
                rel = key_row - (q_row0 + j) + NA_ROWS - 1
                pair = t_ref[h, jnp.clip(rel + 1, 0, 2 * NA_ROWS - 1)]
                if i + 1 < NA_KEY_ROWS:
                    keep = jnp.where(left, in_rows(key_row).astype(jnp.int32),
                                     in_rows(key_row + 1).astype(jnp.int32))
                    blocks.append(jnp.where(keep != 0, pair, MASK_VALUE))
                else:
                    blocks.append(jnp.where(in_rows(key_row), pair[:, :GRID_W], MASK_VALUE))
            rows.append(jnp.concatenate(blocks, axis=-1))
        return jnp.concatenate(rows, axis=0)

    acc = _na_heads(q_ref[...], [kwin, kc_ref[...]], lambda h: [bias_of(h), None],
                    [vwin, vc_ref[...]])
    o_ref[...] = acc.astype(BF16)


def _na_call(qkv, rel_tab, n_batch, seq, total_rows, ctx_len):
    gw = GROUP_WIDTH
    qt = Q_TILE
    n_q = seq // qt
    grid_rows = seq // GRID_W
    ctx_blk0 = n_batch * seq // ctx_len
    return pl.pallas_call(
        functools.partial(_na_kernel, grid_rows=grid_rows),
        grid=(n_batch, n_q),
        in_specs=[pl.BlockSpec((qt, gw), lambda b, j: (b * n_q + j, 0)),
                  pl.BlockSpec((seq, gw), lambda b, j: (b, 1)),
                  pl.BlockSpec((seq, gw), lambda b, j: (b, 2)),
                  pl.BlockSpec((ctx_len, gw), lambda b, j: (ctx_blk0 + b, 1)),
                  pl.BlockSpec((ctx_len, gw), lambda b, j: (ctx_blk0 + b, 2)),
                  _full(rel_tab.shape)],
        out_specs=pl.BlockSpec((qt, gw), lambda b, j: (b * n_q + j, 0)),
        out_shape=jax.ShapeDtypeStruct((total_rows, gw), BF16),
        compiler_params=_cparams(2),
        name="na_attn",
    )(qkv, qkv, qkv, qkv, qkv, rel_tab)


def _df_kernel(q_ref, k_ref, v_ref, kc_ref, vc_ref, lam_ref, subln_ref, o_ref, acc_ref, *, lam_init):
    lam = _lambda(lam_ref, lam_init)
    acc = _diff_heads(q_ref[...], [kc_ref[...], k_ref[...]], [vc_ref[...], v_ref[...]], lam,
                      subln_ref[...], lam_init, acc_ref)
    o_ref[...] = acc.astype(BF16)


def _df_call(qkv, lam_vecs, subln, lam_init, n_batch, seq, total_rows, ctx_len):
    gw = GROUP_WIDTH
    qt = DIFF_Q_TILE
    n_q = seq // qt
    ctx_blk0 = n_batch * seq // ctx_len
    return pl.pallas_call(
        functools.partial(_df_kernel, lam_init=lam_init),
        grid=(n_batch, n_q),
        in_specs=[pl.BlockSpec((qt, gw), lambda b, j: (b * n_q + j, 3)),
                  pl.BlockSpec((seq, gw), lambda b, j: (b, 4)),
                  pl.BlockSpec((seq, gw), lambda b, j: (b, 5)),
                  pl.BlockSpec((ctx_len, gw), lambda b, j: (ctx_blk0 + b, 4)),
                  pl.BlockSpec((ctx_len, gw), lambda b, j: (ctx_blk0 + b, 5)),
                  _full(lam_vecs.shape),
                  _full(subln.shape)],
        out_specs=pl.BlockSpec((qt, gw), lambda b, j: (b * n_q + j, 0)),
        out_shape=jax.ShapeDtypeStruct((total_rows, gw), BF16),
        scratch_shapes=[pltpu.VMEM((qt, gw), F32)],
        compiler_params=_cparams(2),
        name="diff_attn",
    )(qkv, qkv, qkv, qkv, qkv, lam_vecs, subln)


def _ctx_attn_kernel(qkv_ref, lam_ref, subln_ref, ya_in, yd_in, ya_ref, yd_ref, acc_ref, *, lam_init):
    del ya_in, yd_in
    gw = GROUP_WIDTH
    col = lambda g: qkv_ref[:, g * gw:(g + 1) * gw]
    ya = _na_heads(col(0), [col(1)], lambda h: [None], [col(2)])
    ya_ref[...] = ya.astype(BF16)
    lam = _lambda(lam_ref, lam_init)
    yd = _diff_heads(col(3), [col(4)], [col(5)], lam, subln_ref[...], lam_init, acc_ref)
    yd_ref[...] = yd.astype(BF16)


def _ctx_attn_call(qkv, lam_vecs, subln, lam_init, ya, yd, n_batch, seq, ctx_len):
    gw = GROUP_WIDTH
    ctx_blk0 = n_batch * seq // ctx_len
    any_spec = pl.BlockSpec(memory_space=pl.ANY)
    out_spec = pl.BlockSpec((ctx_len, gw), lambda b: (ctx_blk0 + b, 0))
    return pl.pallas_call(
        functools.partial(_ctx_attn_kernel, lam_init=lam_init),
        grid=(n_batch,),
        in_specs=[pl.BlockSpec((ctx_len, 6 * gw), lambda b: (ctx_blk0 + b, 0)),
                  _full(lam_vecs.shape), _full(subln.shape), any_spec, any_spec],
        out_specs=[out_spec, out_spec],
        out_shape=[jax.ShapeDtypeStruct(ya.shape, BF16), jax.ShapeDtypeStruct(yd.shape, BF16)],
        input_output_aliases={3: 0, 4: 1},
        scratch_shapes=[pltpu.VMEM((ctx_len, gw), F32)],
        compiler_params=_cparams(1),
        name="ctx_attn",
    )(qkv, lam_vecs, subln, ya, yd)


def _pool_kernel(p_ref, w_ref, scale_ref, *rest, seq):
    o_ref, pad_ref = rest[-2], rest[-1]
    gw = GROUP_WIDTH
    halo = POOL_HALO
    pad_ref[0:halo, :] = jnp.zeros((halo, gw), F32)
    pad_ref[halo + seq:, :] = jnp.zeros((halo, gw), F32)
    pad_ref[halo:halo + seq, :] = p_ref[...]
    chunk = min(seq, 256)
    lane_group = lax.broadcasted_iota(jnp.int32, (1, gw), 1) // (gw // len(POOL_WINDOWS))
    for c0 in range(0, seq, chunk):
        def at(off):
            return pad_ref[halo + c0 + off:halo + c0 + off + chunk, :]
        pos = c0 + lax.broadcasted_iota(jnp.int32, (chunk, 1), 0)
        x = at(0)
        run = x
        mean = None
        lo_done, hi_done = 0, 0
        for g, win in enumerate(POOL_WINDOWS):
            half = win // 2
            for off in range(-half, -lo_done):
                run = run + at(off)
            for off in range(hi_done + 1, half):
                run = run + at(off)
            lo_done, hi_done = half, half - 1
            cnt = (jnp.minimum(pos + half, seq) - jnp.maximum(pos - half, 0)).astype(F32)
            m = run / cnt
            mean = m if mean is None else jnp.where(lane_group == g, m, mean)
        y = _dot((mean - x).astype(BF16), w_ref[...]) * scale_ref[...]
        o_ref[c0:c0 + chunk, :] = y.astype(BF16)


def _pool_call(pool_in, w_bd, scale, n_seq, seq, row_blk0, total_rows, prev=None):
    gw = GROUP_WIDTH
    in_specs = [pl.BlockSpec((seq, gw), lambda b: (row_blk0 + b, 0)), _full((gw, gw)), _full((1, gw))]
    args = [pool_in, w_bd, scale]
    aliases = {}
    if prev is not None:
        in_specs.append(pl.BlockSpec(memory_space=pl.ANY))
        args.append(prev)
        aliases = {3: 0}
    return pl.pallas_call(
        functools.partial(_pool_kernel, seq=seq),
        grid=(n_seq,),
        in_specs=in_specs,
        out_specs=pl.BlockSpec((seq, gw), lambda b: (row_blk0 + b, 0)),
        out_shape=jax.ShapeDtypeStruct((total_rows, gw), BF16),
        scratch_shapes=[pltpu.VMEM((seq + 2 * POOL_HALO, gw), F32)],
        input_output_aliases=aliases,
        compiler_params=_cparams(1),
        name="pool_mix",
    )(*args)


def _fft_kernel(t_ref, cl_ref, sl_ref, cc_ref, sc_ref, w_ref, *rest, norm):
    o_ref, a_ref, b_ref = rest[-3], rest[-2], rest[-1]

    @pl.when(pl.program_id(1) == 0)
    def _():
        t = t_ref[...]
        a_ref[...] = _dot(t, cc_ref[...]).astype(BF16)
        b_ref[...] = _dot(t, sc_ref[...]).astype(BF16)

    f = (_dot(cl_ref[...], a_ref[...]) - _dot(sl_ref[...], b_ref[...])) * norm
    o_ref[...] = _dot(f.astype(BF16), w_ref[...]).astype(BF16)


def _fft_call(fft_in, cl, sl, cc_bd, sc_bd, w_bd, n_seq, seq, row_blk0, total_rows, prev=None):
    gw = GROUP_WIDTH
    tk = min(FFT_TILE, seq)
    n_k = seq // tk
    in_specs = [pl.BlockSpec((seq, gw), lambda b, k: (row_blk0 + b, 0)),
                pl.BlockSpec((tk, seq), lambda b, k: (k, 0)),
                pl.BlockSpec((tk, seq), lambda b, k: (k, 0)),
                _full((gw, gw)), _full((gw, gw)), _full((gw, gw))]
    args = [fft_in, cl, sl, cc_bd, sc_bd, w_bd]
    aliases = {}
    if prev is not None:
        in_specs.append(pl.BlockSpec(memory_space=pl.ANY))
        args.append(prev)
        aliases = {6: 0}
    norm = 1.0 / math.sqrt(seq * (gw // 4))
    return pl.pallas_call(
        functools.partial(_fft_kernel, norm=norm),
        grid=(n_seq, n_k),
        in_specs=in_specs,
        out_specs=pl.BlockSpec((tk, gw), lambda b, k: ((row_blk0 + b) * n_k + k, 0)),
        out_shape=jax.ShapeDtypeStruct((total_rows, gw), BF16),
        scratch_shapes=[pltpu.VMEM((seq, gw), BF16), pltpu.VMEM((seq, gw), BF16)],
        input_output_aliases=aliases,
        compiler_params=_cparams(2),
        name="fourier_mix",
    )(*args)


def _out_kernel(ya_ref, yd_ref, yb_ref, yf_ref, w_ref, xa_ref, xb_ref, g1_ref, sh2_ref, sc2_ref, gn2_ref,
                rw_ref, rb_ref, x1_ref, h2_ref, lg_ref, *, n_lat_tiles):
    gw = GROUP_WIDTH
    x = jnp.where(pl.program_id(0) < n_lat_tiles, xa_ref[...], xb_ref[...])
    acc = _dot(ya_ref[...], w_ref[0:gw, :])
    acc = acc + _dot(yd_ref[...], w_ref[gw:2 * gw, :])
    acc = acc + _dot(yb_ref[...], w_ref[2 * gw:3 * gw, :])
    acc = acc + _dot(yf_ref[...], w_ref[3 * gw:4 * gw, :])
    x1 = x + g1_ref[0] * acc
    x1_ref[...] = x1
    ms = jnp.mean(x1 * x1, axis=-1, keepdims=True)
    h2 = x1 * lax.rsqrt(ms + NORM_EPS) * gn2_ref[...] * (1.0 + sc2_ref[0]) + sh2_ref[0]
    h_hi = h2.astype(BF16)
    h_lo = (h2 - h_hi.astype(F32)).astype(BF16)
    by_hi = _dot(h_hi, rw_ref[...])
    by_lo = _dot(h_lo, rw_ref[:, :LANES])
    lg_ref[...] = by_hi[:, :LANES] + by_hi[:, LANES:] + by_lo + rb_ref[...]
    tm = x1.shape[0]
    for s in range(SLABS):
        h2_ref[pl.ds(s, tm, stride=SLABS), :] = h2[:, s * LANES:(s + 1) * LANES]


def _out_call(ya, yd, yb, yf, w_out_bf, xa, xb, ctx_blk0, mod3, gn2, rw_pad, rb_pad, rows, n_lat_rows, seq):
    d = D_MODEL
    gw = GROUP_WIDTH
    tm = ROW_TILE
    n_lat_tiles = n_lat_rows // tm
    tiles_per_seq = seq // tm
    n_batch = n_lat_rows // seq

    def mod_row(i):
        return jnp.minimum(i // tiles_per_seq, n_batch)

    mix_spec = pl.BlockSpec((tm, gw), lambda i: (i, 0))
    mod_spec = lambda chunk: pl.BlockSpec((1, 1, d), lambda i: (mod_row(i), 0, chunk))
    return pl.pallas_call(
        functools.partial(_out_kernel, n_lat_tiles=n_lat_tiles),
        grid=(rows // tm,),
        in_specs=[mix_spec, mix_spec, mix_spec, mix_spec, _full((4 * gw, d))]
                 + _stream_specs(tm, d, n_lat_tiles, ctx_blk0) + [
                  mod_spec(2), mod_spec(3), mod_spec(4), _full((1, d)),
                  _full(rw_pad.shape), _full(rb_pad.shape)],
        out_specs=[pl.BlockSpec((tm, d), lambda i: (i, 0)),
                   pl.BlockSpec((tm * SLABS, LANES), lambda i: (i, 0)),
                   pl.BlockSpec((tm, LANES), lambda i: (i, 0))],
        out_shape=[jax.ShapeDtypeStruct((rows, d), F32),
                   jax.ShapeDtypeStruct((rows * SLABS, LANES), F32),
                   jax.ShapeDtypeStruct((rows, LANES), F32)],
        compiler_params=_cparams(1),
        name="out_proj",
    )(ya, yd, yb, yf, w_out_bf, xa, xb, mod3, mod3, mod3, gn2, rw_pad, rb_pad)


def _route_kernel(lg_ref, gate_ref, code_ref, cnt_ref, carry_ref):
    i = pl.program_id(0)

    @pl.when(i == 0)
    def _():
        carry_ref[...] = jnp.zeros_like(carry_ref)

    lg = lg_ref[...]
    tm = lg.shape[0]
    lane = lax.broadcasted_iota(jnp.int32, (tm, LANES), 1)
    vals, idxs = [], []
    onehot = jnp.zeros((tm, LANES), F32)
    for _ in range(TOP_K):
        m = lg.max(axis=-1, keepdims=True)
        idx = jnp.min(jnp.where(lg == m, lane, LANES), axis=-1, keepdims=True)
        sel = lane == idx
        onehot = onehot + sel.astype(F32)
        lg = jnp.where(sel, -jnp.inf, lg)
        vals.append(m)
        idxs.append(idx)
    exps = [jnp.exp(v - vals[0]) for v in vals]
    denom = exps[0] + exps[1] + exps[2] + exps[3]
    r_i = lax.broadcasted_iota(jnp.int32, (tm, tm), 0)
    c_i = lax.broadcasted_iota(jnp.int32, (tm, tm), 1)
    tri = (c_i < r_i).astype(BF16)
    before = _dot(tri, onehot.astype(BF16)) + carry_ref[...]
    col4 = lax.broadcasted_iota(jnp.int32, (tm, TOP_K), 1)
    gate_o = jnp.zeros((tm, TOP_K), F32)
    code_o = jnp.zeros((tm, TOP_K), F32)
    for k in range(TOP_K):
        rk = jnp.sum(jnp.where(lane == idxs[k], before, 0.0), axis=-1, keepdims=True)
        gate_o = jnp.where(col4 == k, exps[k] / denom, gate_o)
        code_o = jnp.where(col4 == k, idxs[k].astype(F32) * float(RANK_STRIDE) + rk, code_o)
    gate_ref[...] = gate_o
    code_ref[...] = code_o.astype(jnp.int32)
    carry_ref[...] = carry_ref[...] + jnp.sum(onehot, axis=0, keepdims=True)
    cnt_ref[...] = carry_ref[...].astype(jnp.int32)


def _route_call(logits):
    n = logits.shape[0]
    tm = ROW_TILE
    k_spec = pl.BlockSpec((tm, TOP_K), lambda i: (i, 0))
    return pl.pallas_call(
        _route_kernel,
        grid=(n // tm,),
        in_specs=[pl.BlockSpec((tm, LANES), lambda i: (i, 0))],
        out_specs=[k_spec, k_spec, _full((1, LANES))],
        out_shape=[jax.ShapeDtypeStruct((n, TOP_K), F32),
                   jax.ShapeDtypeStruct((n, TOP_K), jnp.int32),
                   jax.ShapeDtypeStruct((1, LANES), jnp.int32)],
        scratch_shapes=[pltpu.VMEM((1, LANES), F32)],
        compiler_params=_cparams(1),
        name="route",
    )(logits)


def _row_copy(src_ref, src_row, dst_ref, dst_row, sem):
    return pltpu.make_async_copy(
        src_ref.at[pl.ds(pl.multiple_of(src_row * SLABS, SLABS), SLABS), :],
        dst_ref.at[pl.ds(pl.multiple_of(dst_row * SLABS, SLABS), SLABS), :], sem)


def _pad_copy(zero_ref, hs_ref, start_row, n_rows, sem):
    return pltpu.make_async_copy(
        zero_ref.at[pl.ds(0, n_rows * SLABS), :],
        hs_ref.at[pl.ds(pl.multiple_of(start_row * SLABS, SLABS), n_rows * SLABS), :], sem)


def _dispatch_kernel(fill_ref, pad_ref, dest_hbm, h2_hbm, hs_ref,
                     idx0, idx1, idx2, rows0, rows1, rows2, zero_ref, zsem, isems, lsems, ssems):
    i = pl.program_id(0)
    n_tiles = pl.num_programs(0)
    tt = TOKEN_TILE
    n_assign = tt * TOP_K
    idx_refs, row_refs = (idx0, idx1, idx2), (rows0, rows1, rows2)

    def idx_copy(tile, s):
        return pltpu.make_async_copy(dest_hbm.at[pl.ds(tile * n_assign, n_assign)], idx_refs[s],
                                     isems.at[s])

    def row_load(tile, s):
        return pltpu.make_async_copy(h2_hbm.at[pl.ds(tile * (tt * SLABS), tt * SLABS), :], row_refs[s],
                                     lsems.at[s])

    def wait_sent(s):
        for _ in range(TOP_K):
            pltpu.make_async_copy(row_refs[s], hs_ref.at[pl.ds(0, tt * SLABS), :], ssems.at[s]).wait()

    @pl.when(i == 0)
    def _():
        idx_copy(0, 0).start()
        row_load(0, 0).start()
        zero_ref[...] = jnp.zeros_like(zero_ref)
        bits = [1 << b for b in reversed(range(int(math.log2(EXPERT_BLOCK))))]
        for phase in ("start", "wait"):
            def fill(e, carry, phase=phase):
                pos = fill_ref[e]
                pad = pad_ref[e]
                for bit in bits:
                    @pl.when((pad & bit) != 0)
                    def _(pos=pos, bit=bit):
                        cp = _pad_copy(zero_ref, hs_ref, pos, bit, zsem)
                        cp.start() if phase == "start" else cp.wait()
                    pos = pos + (pad & bit)
                return carry

            lax.fori_loop(0, N_EXPERTS, fill, 0)

    for s in range(3):
        @pl.when(i % 3 == s)
        def _(s=s):
            nxt, prev = (s + 1) % 3, (s + 2) % 3

            @pl.when(i >= 2)
            def _():
                wait_sent(nxt)

            @pl.when(i + 1 < n_tiles)
            def _():
                idx_copy(i + 1, nxt).start()
                row_load(i + 1, nxt).start()

            idx_copy(i, s).wait()
            row_load(i, s).wait()

            def start(t, carry):
                for k in range(TOP_K):
                    _row_copy(row_refs[s], t, hs_ref, idx_refs[s][t * TOP_K + k],
                              ssems.at[s]).start(priority=k % 2)
                return carry

            lax.fori_loop(0, tt, start, 0, unroll=ISSUE_UNROLL)

            @pl.when(i == n_tiles - 1)
            def _():
                @pl.when(i >= 1)
                def _():
                    wait_sent(prev)
                wait_sent(s)


def _dispatch_call(fill_start, pad_len, dest_flat, h2_slabs, n_slots):
    n = h2_slabs.shape[0] // SLABS
    tt = TOKEN_TILE
    any_spec = pl.BlockSpec(memory_space=pl.ANY)
    grid_spec = pltpu.PrefetchScalarGridSpec(
        num_scalar_prefetch=2,
        grid=(n // tt,),
        in_specs=[any_spec, any_spec],
        out_specs=any_spec,
        scratch_shapes=[pltpu.SMEM((tt * TOP_K,), jnp.int32)] * 3
                       + [pltpu.VMEM((tt * SLABS, LANES), F32)] * 3
                       + [pltpu.VMEM((EXPERT_BLOCK // 2 * SLABS, LANES), F32),
                          pltpu.SemaphoreType.DMA, pltpu.SemaphoreType.DMA((3,)),
                          pltpu.SemaphoreType.DMA((3,)), pltpu.SemaphoreType.DMA((3,))],
    )
    return pl.pallas_call(
        _dispatch_kernel,
        grid_spec=grid_spec,
        out_shape=jax.ShapeDtypeStruct((n_slots * SLABS, LANES), F32),
        compiler_params=_cparams(1),
        name="dispatch",
    )(fill_start, pad_len, dest_flat, h2_slabs)


GLU_BLOCK = 2 * LANES


def _expert_kernel(be_ref, first_ref, nv_ref, hs_ref, w1_ref, b1_ref, w2_ref, b2_ref, perm_ref,
                   ys_ref, h_ref, act_ref, w1p_ref, w2p_ref):
    del be_ref
    i = pl.program_id(0)
    n_glu = 2 * D_FF // GLU_BLOCK

    @pl.when((first_ref[i] != 0) & (i < nv_ref[0]))
    def _():
        for b in range(n_glu):
            cols = slice(b * GLU_BLOCK, (b + 1) * GLU_BLOCK)
            w1p_ref[:, cols] = _dot(w1_ref[0, :, cols].astype(BF16), perm_ref[...]).astype(BF16)
        w2p_ref[...] = w2_ref[0].astype(BF16)

    @pl.when(i < nv_ref[0])
    def _():
        blk = EXPERT_BLOCK
        for s in range(SLABS):
            h_ref[:, s * LANES:(s + 1) * LANES] = hs_ref[pl.ds(s, blk, stride=SLABS), :].astype(BF16)
        h = h_ref[...]
        for b in range(n_glu):
            cols = slice(b * GLU_BLOCK, (b + 1) * GLU_BLOCK)
            u = _dot(h, w1p_ref[:, cols]) + b1_ref[0, :, cols]
            glu = jnp.minimum(u[:, :LANES], SWIGLU_LIMIT)
            lin = jnp.clip(u[:, LANES:], -SWIGLU_LIMIT, SWIGLU_LIMIT)
            act = glu * jax.nn.sigmoid(SWIGLU_ALPHA * glu) * (lin + 1.0)
            act_ref[:, b * LANES:(b + 1) * LANES] = act.astype(BF16)
        y = _dot(act_ref[...], w2p_ref[...]) + b2_ref[0]
        for s in range(SLABS):
            ys_ref[pl.ds(s, blk, stride=SLABS), :] = y[:, s * LANES:(s + 1) * LANES]


def _expert_call(block_e, first, n_valid, hs, w1, b1p, w2, b2, perm, layer, n_blocks):
    blk = EXPERT_BLOCK
    d = D_MODEL

    def row_blk(i, be, first, nv):
        return (jnp.minimum(i, nv[0] - 1), 0)

    def expert_blk(i, be, first, nv):
        return (be[i], 0, 0)

    def weight_blk(i, be, first, nv):
        return (layer, be[i], 0, 0)

    grid_spec = pltpu.PrefetchScalarGridSpec(
        num_scalar_prefetch=3,
        grid=(n_blocks,),
        in_specs=[pl.BlockSpec((blk * SLABS, LANES), row_blk),
                  pl.BlockSpec((None, 1, d, 2 * D_FF), weight_blk),
                  pl.BlockSpec((1, 1, 2 * D_FF), expert_blk),
                  pl.BlockSpec((None, 1, D_FF, d), weight_blk),
                  pl.BlockSpec((1, 1, d), expert_blk),
                  pl.BlockSpec((GLU_BLOCK, GLU_BLOCK), lambda i, *_: (0, 0))],
        out_specs=pl.BlockSpec((blk * SLABS, LANES), row_blk),
        scratch_shapes=[pltpu.VMEM((blk, d), BF16), pltpu.VMEM((blk, D_FF), BF16),
                        pltpu.VMEM((d, 2 * D_FF), BF16), pltpu.VMEM((D_FF, d), BF16)],
    )
    return pl.pallas_call(
        _expert_kernel,
        grid_spec=grid_spec,
        out_shape=jax.ShapeDtypeStruct(hs.shape, F32),
        compiler_params=_cparams(1, EXPERT_VMEM_LIMIT),
        name="expert_ffn",
    )(block_e, first, n_valid, hs, w1, b1p, w2, b2, perm)


def _combine_kernel(dest_hbm, ys_hbm, gate_ref, x_ref, g2_ref, o_ref,
                    idx0, idx1, buf0, buf1, sems, isems):
    i = pl.program_id(0)
    n_tiles = pl.num_programs(0)
    tt = COMBINE_TILE
    n_assign = tt * TOP_K
    idx_refs, buf_refs = (idx0, idx1), (buf0, buf1)

    def idx_copy(tile, s):
        return pltpu.make_async_copy(dest_hbm.at[pl.ds(tile * n_assign, n_assign)], idx_refs[s],
                                     isems.at[s])

    def issue(s):
        def start(t, carry):
            for k in range(TOP_K):
                _row_copy(ys_hbm, idx_refs[s][t * TOP_K + k], buf_refs[s], k * tt + t,
                          sems.at[s]).start(priority=k % 2)
            return carry
        lax.fori_loop(0, tt, start, 0, unroll=ISSUE_UNROLL)

    @pl.when(i == 0)
    def _():
        idx_copy(0, 0).start()
        idx_copy(0, 0).wait()
        issue(0)

        @pl.when(n_tiles > 1)
        def _():
            idx_copy(1, 1).start()

    for s in (0, 1):
        @pl.when(i % 2 == s)
        def _(s=s):
            @pl.when(i + 1 < n_tiles)
            def _():
                idx_copy(i + 1, 1 - s).wait()
                issue(1 - s)

            pltpu.make_async_copy(ys_hbm.at[pl.ds(0, n_assign * SLABS), :], buf_refs[s], sems.at[s]).wait()

            @pl.when(i + 2 < n_tiles)
            def _():
                idx_copy(i + 2, s).start()

            gate = gate_ref[...]
            g2 = g2_ref[0]
            for sl in range(SLABS):
                cols = slice(sl * LANES, (sl + 1) * LANES)
                y = jnp.zeros((tt, LANES), F32)
                for k in range(TOP_K):
                    y = y + gate[:, k:k + 1] * buf_refs[s][pl.ds(k * tt * SLABS + sl, tt, stride=SLABS), :]
                o_ref[:, cols] = x_ref[:, cols] + g2[:, cols] * y


def _combine_call(dest_flat, ys, gate, x1, mod3, n_lat_rows, seq):
    n, d = x1.shape
    tt = COMBINE_TILE
    tiles_per_seq = seq // tt
    n_batch = n_lat_rows // seq

    def mod_row(i):
        return jnp.minimum(i // tiles_per_seq, n_batch)

    return pl.pallas_call(
        _combine_kernel,
        grid=(n // tt,),
        in_specs=[pl.BlockSpec(memory_space=pl.ANY), pl.BlockSpec(memory_space=pl.ANY),
                  pl.BlockSpec((tt, TOP_K), lambda i: (i, 0)),
                  pl.BlockSpec((tt, d), lambda i: (i, 0)),
                  pl.BlockSpec((1, 1, d), lambda i: (mod_row(i), 0, 5))],
        out_specs=pl.BlockSpec((tt, d), lambda i: (i, 0)),
        out_shape=jax.ShapeDtypeStruct((n, d), F32),
        scratch_shapes=[pltpu.SMEM((tt * TOP_K,), jnp.int32), pltpu.SMEM((tt * TOP_K,), jnp.int32),
                        pltpu.VMEM((TOP_K * tt * SLABS, LANES), F32),
                        pltpu.VMEM((TOP_K * tt * SLABS, LANES), F32),
                        pltpu.SemaphoreType.DMA((2,)), pltpu.SemaphoreType.DMA((2,))],
        compiler_params=_cparams(1),
        name="combine",
    )(dest_flat, ys, gate, x1, mod3)


def _block_diag(blocks):
    g, a, b = blocks.shape
    eye = jnp.eye(g, dtype=blocks.dtype)
    return (eye[:, None, :, None] * blocks[:, :, None, :]).reshape(g * a, g * b)


def _rope_tables(seq, extra_rows):
    half = DIFF_QK_DIM // 2
    inv = ROPE_THETA ** (-jnp.arange(0, half, 2, dtype=F32) / half)
    pos = jnp.arange(seq)
    c = jnp.arange(GROUP_WIDTH)
    dd = c % DIFF_QK_DIM
    axis_pos = jnp.where((dd < half)[None, :], (pos // GRID_W)[:, None], (pos % GRID_W)[:, None])
    ang = axis_pos.astype(F32) * inv[dd % (half // 2)][None, :]
    sign = jnp.where((dd % half) < half // 2, -1.0, 1.0).astype(F32)
    cos_t = jnp.concatenate([jnp.cos(ang), jnp.ones((extra_rows, GROUP_WIDTH), F32)], axis=0)
    sin_t = jnp.concatenate([jnp.sin(ang) * sign[None, :], jnp.zeros((extra_rows, GROUP_WIDTH), F32)], axis=0)
    return cos_t, sin_t


def _dft_tables(n, dtype):
    k = np.arange(n, dtype=np.int64)
    ang = ((k[:, None] * k[None, :]) % n).astype(np.float64) * (2.0 * math.pi / n)
    return jnp.asarray(np.cos(ang), dtype), jnp.asarray(np.sin(ang), dtype)


def _na_rel_tables(rpb):
    w = GRID_W
    col = np.arange(w)
    col_start = np.clip(col - NA_COLS // 2, 0, w - NA_COLS)
    in_win = (col[None, :] >= col_start[:, None]) & (col[None, :] < col_start[:, None] + NA_COLS)
    edge = w - NA_COLS
    ext = jnp.pad(rpb.astype(F32), ((0, 0), (0, 0), (edge, edge)), mode='edge')
    t = jnp.stack([ext[:, :, w - 1 - q:2 * w - 1 - q] for q in range(w)], axis=2)
    t = jnp.where(in_win[None, None], t, MASK_VALUE)
    masked = jnp.full_like(t[:, :1], MASK_VALUE)
    padded = jnp.concatenate([masked, t, masked], axis=1)
    return jnp.concatenate([padded[:, :-1], padded[:, 1:]], axis=-1)


def _moe(h2_slabs, logits, x1, mod3, w1, b1p, w2, b2, perm, layer, n_lat_rows, seq):
    n = x1.shape[0]
    blk = EXPERT_BLOCK
    assert n * TOP_K <= RANK_STRIDE
    gate, code, counts = _route_call(logits)
    counts = counts[0, :N_EXPERTS]
    padded = (counts + blk - 1) // blk * blk
    padded_end = jnp.cumsum(padded)
    padded_start = padded_end - padded
    codes = code.reshape(1, n * TOP_K)
    experts = lax.shift_right_logical(codes, RANK_SHIFT)
    first_slot = jnp.sum(jnp.where(experts == jnp.arange(N_EXPERTS, dtype=jnp.int32)[:, None],
                                   padded_start.astype(jnp.int32)[:, None], 0), axis=0)
    dest = first_slot + (codes[0] & RANK_MASK)
    n_blocks = n * TOP_K // blk + N_EXPERTS
    block_row0 = jnp.arange(n_blocks, dtype=jnp.int32) * blk
    block_e = jnp.minimum(jnp.sum((padded_end[None, :] <= block_row0[:, None]).astype(jnp.int32), axis=1),
                          N_EXPERTS - 1).astype(jnp.int32)
    first = jnp.concatenate([jnp.ones((1,), jnp.int32),
                             (block_e[1:] != block_e[:-1]).astype(jnp.int32)])
    n_valid = (padded_end[-1:] // blk).astype(jnp.int32)
    fill_start = (padded_start + counts).astype(jnp.int32)
    pad_len = (padded - counts).astype(jnp.int32)
    hs = _dispatch_call(fill_start, pad_len, dest, h2_slabs, n_blocks * blk)
    ys = _expert_call(block_e, first, n_valid, hs, w1, b1p, w2, b2, perm, layer, n_blocks)
    return _combine_call(dest, ys, gate, x1, mod3, n_lat_rows, seq)


def kernel(x, c, ctx, c_ctx, w_ada, b_ada, g_norm1, g_norm2, w_in, w_out, na_q_gain, na_k_gain, na_rpb, diff_q_gain, diff_k_gain, diff_lambda_q1, diff_lambda_k1, diff_lambda_q2, diff_lambda_k2, diff_subln, pool_w, pool_scale, fft_w, router_w, router_b, moe_w1, moe_b1, moe_w2, moe_b2):
    n_batch, seq, d = x.shape
    ctx_len = ctx.shape[1]
    depth = w_ada.shape[0]
    gw = GROUP_WIDTH
    assert d == D_MODEL and seq % ROW_TILE == 0 and seq % ctx_len == 0 and seq % TOKEN_TILE == 0
    assert seq % COMBINE_TILE == 0 and (n_batch * ctx_len) % COMBINE_TILE == 0
    assert (n_batch * ctx_len) % ROW_TILE == 0 and (seq // GRID_W) >= NA_KEY_ROWS
    n_lat = n_batch * seq
    n_ctx = n_batch * ctx_len

    xa, xb, ctx_blk0 = x.reshape(n_lat, d), ctx.reshape(n_ctx, d), 0
    mod_rows = -(-(n_batch + 1) // SUBLANES) * SUBLANES
    cvec = jnp.zeros((mod_rows, d), F32).at[:n_batch].set(c).at[n_batch].set(c_ctx)
    mod = _ada_call(cvec, w_ada, b_ada)

    cos_t, sin_t = _rope_tables(seq, ROW_TILE)
    ones = lambda w: _block_diag(jnp.ones((gw // w, w, w), BF16))
    g64, g32 = ones(HEAD_DIM), ones(DIFF_QK_DIM)
    cl_lat, sl_lat = _dft_tables(seq, BF16)
    cl_ctx, sl_ctx = _dft_tables(ctx_len, BF16)
    cc, sc = _dft_tables(gw // 4, F32)
    n_grp = fft_w.shape[1]
    cc_bd = _block_diag(jnp.broadcast_to(cc, (n_grp,) + cc.shape)).astype(BF16)
    sc_bd = _block_diag(jnp.broadcast_to(sc, (n_grp,) + sc.shape)).astype(BF16)
    src = np.arange(GLU_BLOCK)
    dst = np.where(src % 2 == 0, src // 2, LANES + src // 2)
    perm_np = np.zeros((GLU_BLOCK, GLU_BLOCK), np.float32)
    perm_np[src, dst] = 1.0
    glu_perm = jnp.asarray(perm_np, BF16)

    tile = lambda v: jnp.tile(v.astype(F32), gw // v.shape[0])

    def in_proj_params(l):
        gains = jnp.stack([tile(na_q_gain[l]) * (HEAD_DIM ** -0.5 * LOG2_E), tile(na_k_gain[l]),
                           tile(diff_q_gain[l]) * (DIFF_QK_DIM ** -0.5 * LOG2_E), tile(diff_k_gain[l])]
                          + [jnp.zeros((gw,), F32)] * 4)
        return (mod[l].reshape(mod_rows, 1, 6 * d), g_norm1[l].reshape(1, d), w_in[l].astype(BF16),
                gains, cos_t, sin_t, g64, g32)

    for l in range(depth):
        ctx_out = l < depth - 1
        lam_init = 0.8 - 0.6 * math.exp(-0.3 * l)
        mod3 = mod[l].reshape(mod_rows, 1, 6 * d)
        qkv, pool_in, fft_in = _in_call(xa, xb, ctx_blk0, n_lat + n_ctx, in_proj_params(l), n_lat, seq)
        lam_vecs = jnp.stack([diff_lambda_q1[l], diff_lambda_k1[l], diff_lambda_q2[l], diff_lambda_k2[l]]
                             + [jnp.zeros_like(diff_lambda_q1[l])] * 4).astype(F32)
        subln = tile(diff_subln[l]).reshape(1, gw)
        rel_tab = _na_rel_tables(na_rpb[l] * LOG2_E)
        pool_bd = _block_diag(pool_w[l]).astype(BF16)
        fftw_bd = _block_diag(fft_w[l]).astype(BF16)
        rw_f32 = jnp.zeros((d, LANES), F32).at[:, :N_EXPERTS].set(router_w[l])
        rw_hi = rw_f32.astype(BF16)
        rw_pad = jnp.concatenate([rw_hi, (rw_f32 - rw_hi.astype(F32)).astype(BF16)], axis=1)
        rb_pad = jnp.full((1, LANES), -jnp.inf, F32).at[0, :N_EXPERTS].set(router_b[l])
        n_e = moe_w1.shape[1]
        b1p = moe_b1[l].reshape(n_e, 2 * D_FF // GLU_BLOCK, LANES, 2).transpose(0, 1, 3, 2)
        b1p = b1p.reshape(n_e, 1, 2 * D_FF)
        b2 = moe_b2[l].reshape(n_e, 1, d)

        rows = n_lat + n_ctx if ctx_out else n_lat
        ya = _na_call(qkv, rel_tab, n_batch, seq, rows, ctx_len)
        yd = _df_call(qkv, lam_vecs, subln, lam_init, n_batch, seq, rows, ctx_len)
        yb = _pool_call(pool_in, pool_bd, pool_scale[l].reshape(1, gw), n_batch, seq, 0, rows)
        yf = _fft_call(fft_in, cl_lat, sl_lat, cc_bd, sc_bd, fftw_bd, n_batch, seq, 0, rows)
        if ctx_out:
            ya, yd = _ctx_attn_call(qkv, lam_vecs, subln, lam_init, ya, yd, n_batch, seq, ctx_len)
            yb = _pool_call(pool_in, pool_bd, pool_scale[l].reshape(1, gw), n_batch, ctx_len,
                            n_lat // ctx_len, rows, prev=yb)
            yf = _fft_call(fft_in, cl_ctx, sl_ctx, cc_bd, sc_bd, fftw_bd, n_batch, ctx_len,
                           n_lat // ctx_len, rows, prev=yf)
        x1, h2_slabs, logits = _out_call(ya, yd, yb, yf, w_out[l].astype(BF16), xa, xb, ctx_blk0, mod3,
                                         g_norm2[l].reshape(1, d), rw_pad, rb_pad, rows, n_lat, seq)
        x2 = _moe(h2_slabs, logits, x1, mod3, moe_w1, b1p, moe_w2, b2, glu_perm, l, n_lat, seq)
        xa, xb, ctx_blk0 = x2, x2, n_lat // ROW_TILE
    return x2[:n_lat].reshape(n_batch, seq, d)
```

```python
import functools
import math

import numpy as np
import jax
import jax.numpy as jnp
from jax import lax
from jax.experimental import pallas as pl
from jax.experimental.pallas import tpu as pltpu

F32 = jnp.float32
BF16 = jnp.bfloat16
HIGHEST = lax.Precision.HIGHEST

D_MODEL = 1024
DEPTH = 2
GRID_W = 64
HEAD_DIM = 64
GROUP_WIDTH = 256
N_HEADS = GROUP_WIDTH // HEAD_DIM
DIFF_QK_DIM = HEAD_DIM // 2
ROPE_SEG = DIFF_QK_DIM // 2
NA_ROWS = 8
NA_COLS = 16
POOL_WINDOWS = (2, 4, 8, 16)
POOL_HALO = max(POOL_WINDOWS) // 2
N_EXPERTS = 32
TOP_K = 4
D_FF = D_MODEL
SWIGLU_ALPHA = 1.702
SWIGLU_LIMIT = 7.0
ROPE_THETA = 10000.0
NORM_EPS = 1e-6
MASK_VALUE = -1e30

LANES = 128
SUBLANES = 8
ROW_TILE = 512
TOKEN_TILE = 512
COMBINE_TILE = 256
EXPERT_BLOCK = 512
Q_TILE = 256
DIFF_Q_TILE = 512
LOG2_E = math.log2(math.e)
NA_Q_ROWS = Q_TILE // GRID_W
NA_KEY_ROWS = NA_ROWS + NA_Q_ROWS - 1
FFT_TILE = 512
SLABS = D_MODEL // LANES
ISSUE_UNROLL = 8
RANK_STRIDE = 1 << 17
RANK_MASK = RANK_STRIDE - 1
RANK_SHIFT = 17
VMEM_LIMIT = 48 * 1024 * 1024
EXPERT_VMEM_LIMIT = 56 * 1024 * 1024


def _cparams(n_axes, vmem_limit=VMEM_LIMIT):
    return pltpu.CompilerParams(dimension_semantics=("arbitrary",) * n_axes,
                                vmem_limit_bytes=vmem_limit)


def _dot(a, b):
    return jnp.dot(a, b, preferred_element_type=F32)


def _dot_nt(a, b):
    return lax.dot_general(a, b, (((1,), (1,)), ((), ())), preferred_element_type=F32)


def _full(shape):
    zeros = (0,) * len(shape)
    return pl.BlockSpec(shape, lambda *_: zeros)


def _ada_kernel(c_ref, w_ref, b_ref, o_ref):
    c = c_ref[...]
    act = c * jax.nn.sigmoid(c)
    o_ref[0] = jnp.dot(act, w_ref[0], precision=HIGHEST, preferred_element_type=F32) + b_ref[0]


def _ada_call(cvec, w_ada, b_ada):
    depth, d, n = w_ada.shape
    r = cvec.shape[0]
    tn = 1024
    return pl.pallas_call(
        _ada_kernel,
        grid=(depth, n // tn),
        in_specs=[_full((r, d)),
                  pl.BlockSpec((1, d, tn), lambda l, j: (l, 0, j)),
                  pl.BlockSpec((1, 1, tn), lambda l, j: (l, 0, j))],
        out_specs=pl.BlockSpec((1, r, tn), lambda l, j: (l, 0, j)),
        out_shape=jax.ShapeDtypeStruct((depth, r, n), F32),
        compiler_params=_cparams(2),
        name="ada_mod",
    )(cvec, w_ada, b_ada.reshape(depth, 1, n))


def _seg_mean_sq(p, gmat_ref, width):
    sq = p * p
    hi = sq.astype(BF16)
    lo = (sq - hi.astype(F32)).astype(BF16)
    g = gmat_ref[...]
    return (_dot(hi, g) + _dot(lo, g)) * (1.0 / width)


def _stream_specs(tm, d, n_lat_tiles, ctx_blk0):
    return [pl.BlockSpec((tm, d), lambda i: (jnp.minimum(i, n_lat_tiles - 1), 0)),
            pl.BlockSpec((tm, d), lambda i: (ctx_blk0 + jnp.maximum(i - n_lat_tiles, 0), 0))]


def _in_kernel(xa_ref, xb_ref, *rest, n_lat_tiles):
    x = jnp.where(pl.program_id(0) < n_lat_tiles, xa_ref[...], xb_ref[...])
    _in_body(x, *rest)


def _in_body(x, sh_ref, sc_ref, g_ref, w_ref, gains_ref, cos_ref, sin_ref, g64_ref, g32_ref,
             qkv_ref, pool_ref, fft_ref):
    ms = jnp.mean(x * x, axis=-1, keepdims=True)
    y = x * lax.rsqrt(ms + NORM_EPS) * g_ref[...]
    h = (y * (1.0 + sc_ref[0]) + sh_ref[0]).astype(BF16)
    gw = GROUP_WIDTH

    def proj(g):
        return _dot(h, w_ref[:, g * gw:(g + 1) * gw])

    def put(g, val):
        qkv_ref[:, g * gw:(g + 1) * gw] = val.astype(BF16)

    def normed(p, gmat_ref, width, row):
        return p * lax.rsqrt(_seg_mean_sq(p, gmat_ref, width) + NORM_EPS) * gains_ref[row:row + 1, :]

    lane = lax.broadcasted_iota(jnp.int32, (1, gw), 1)
    first_half = (lane % ROPE_SEG) < ROPE_SEG // 2

    def rope(p):
        rot = jnp.where(first_half, pltpu.roll(p, gw - ROPE_SEG // 2, 1), pltpu.roll(p, ROPE_SEG // 2, 1))
        return p * cos_ref[...] + rot * sin_ref[...]

    put(0, normed(proj(0), g64_ref, HEAD_DIM, 0))
    put(1, normed(proj(1), g64_ref, HEAD_DIM, 1))
    put(2, proj(2))
    put(3, rope(normed(proj(3), g32_ref, DIFF_QK_DIM, 2)))
    put(4, rope(normed(proj(4), g32_ref, DIFF_QK_DIM, 3)))
    put(5, proj(5))
    pool_ref[...] = proj(6)
    fft_ref[...] = proj(7).astype(BF16)


def _in_proj_io(tm, rows, params, n_lat_rows, seq):
    mod3, g1, w_in_bf, gains, cos_t, sin_t, g64, g32 = params
    d = D_MODEL
    gw = GROUP_WIDTH
    n_lat_tiles = n_lat_rows // tm
    tiles_per_seq = seq // tm
    n_batch = n_lat_rows // seq

    def mod_row(i):
        return jnp.minimum(i // tiles_per_seq, n_batch)

    def tab_row(i):
        return jnp.where(i < n_lat_tiles, i % tiles_per_seq, tiles_per_seq)

    full = lambda shape: pl.BlockSpec(shape, lambda i, *_: (0,) * len(shape))
    in_specs = [pl.BlockSpec((1, 1, d), lambda i, *_: (mod_row(i), 0, 0)),
                pl.BlockSpec((1, 1, d), lambda i, *_: (mod_row(i), 0, 1)),
                full((1, d)), full(w_in_bf.shape), full(gains.shape),
                pl.BlockSpec((tm, gw), lambda i, *_: (tab_row(i), 0)),
                pl.BlockSpec((tm, gw), lambda i, *_: (tab_row(i), 0)),
                full((gw, gw)), full((gw, gw))]
    args = [mod3, mod3, g1, w_in_bf, gains, cos_t, sin_t, g64, g32]
    out_specs = [pl.BlockSpec((tm, 6 * gw), lambda i, *_: (i, 0)),
                 pl.BlockSpec((tm, gw), lambda i, *_: (i, 0)),
                 pl.BlockSpec((tm, gw), lambda i, *_: (i, 0))]
    out_shape = [jax.ShapeDtypeStruct((rows, 6 * gw), BF16),
                 jax.ShapeDtypeStruct((rows, gw), F32),
                 jax.ShapeDtypeStruct((rows, gw), BF16)]
    return in_specs, args, out_specs, out_shape


def _in_call(xa, xb, ctx_blk0, rows, params, n_lat_rows, seq):
    tm = ROW_TILE
    n_lat_tiles = n_lat_rows // tm
    in_specs, args, out_specs, out_shape = _in_proj_io(tm, rows, params, n_lat_rows, seq)
    return pl.pallas_call(
        functools.partial(_in_kernel, n_lat_tiles=n_lat_tiles),
        grid=(rows // tm,),
        in_specs=_stream_specs(tm, D_MODEL, n_lat_tiles, ctx_blk0) + in_specs,
        out_specs=out_specs,
        out_shape=out_shape,
        compiler_params=_cparams(1),
        name="in_proj",
    )(xa, xb, *args)


def _lane_mask(width, start, size):
    lane = lax.broadcasted_iota(jnp.int32, (1, width), 1)
    return (lane >= start) & (lane < start + size)


def _softmax_pv(qm, keys, biases, vals, sum_lane=None):
    scores = []
    for k, bias in zip(keys, biases):
        s = _dot_nt(qm, k)
        scores.append(s if bias is None else s + bias)
    m = scores[0].max(axis=-1, keepdims=True)
    for s in scores[1:]:
        m = jnp.maximum(m, s.max(axis=-1, keepdims=True))
    o = None
    l = None
    for s, v in zip(scores, vals):
        e = jnp.exp2(s - m)
        part = _dot(e.astype(BF16), v)
        o = part if o is None else o + part
        if sum_lane is None:
            ls = e.sum(axis=-1, keepdims=True)
            l = ls if l is None else l + ls
    if sum_lane is not None:
        lane = lax.broadcasted_iota(jnp.int32, (1, o.shape[1]), 1)
        l = jnp.sum(jnp.where(lane == sum_lane, o, 0.0), axis=-1, keepdims=True)
    return o, l


def _na_heads(q, keys, bias_fn, vals):
    gw = GROUP_WIDTH
    acc = jnp.zeros((q.shape[0], gw), F32)
    for h in range(N_HEADS):
        mask = _lane_mask(gw, h * HEAD_DIM, HEAD_DIM)
        qm = jnp.where(mask, q, jnp.zeros_like(q))
        vals_h = [jnp.where(mask, v, jnp.ones_like(v)) for v in vals]
        o, l = _softmax_pv(qm, keys, bias_fn(h), vals_h, ((h + 1) * HEAD_DIM) % gw)
        acc = acc + jnp.where(mask, o / l, 0.0)
    return acc


def _lambda(lam_ref, lam_init):
    lv = lam_ref[...]
    d1 = jnp.sum(lv[0:1, :] * lv[1:2, :], axis=-1, keepdims=True)
    d2 = jnp.sum(lv[2:3, :] * lv[3:4, :], axis=-1, keepdims=True)
    return jnp.exp(d1) - jnp.exp(d2) + lam_init


def _diff_heads(q, keys, vals, lam, subln, lam_init, acc_ref):
    gw = GROUP_WIDTH
    lane = lax.broadcasted_iota(jnp.int32, (1, gw), 1)
    acc_ref[...] = jnp.zeros_like(acc_ref)

    def scores_of(h, part):
        plo = h * HEAD_DIM + part * DIFF_QK_DIM
        qm = jnp.where((lane >= plo) & (lane < plo + DIFF_QK_DIM), q, jnp.zeros_like(q))
        scores = [_dot_nt(qm, k) for k in keys]
        m = scores[0].max(axis=-1, keepdims=True)
        for s in scores[1:]:
            m = jnp.maximum(m, s.max(axis=-1, keepdims=True))
        return scores, m

    def attend(h, scores, m):
        lo = h * HEAD_DIM
        hm = (lane >= lo) & (lane < lo + HEAD_DIM)
        o = None
        for s, v in zip(scores, vals):
            part = _dot(jnp.exp2(s - m).astype(BF16), jnp.where(hm, v, jnp.ones_like(v)))
            o = part if o is None else o + part
        l = jnp.sum(jnp.where(lane == (lo + HEAD_DIM) % gw, o, 0.0), axis=-1, keepdims=True)
        return o / l

    combos = [(h, part) for h in range(N_HEADS) for part in range(2)]
    pending = [scores_of(*combos[0])]
    outs = []
    for c, (h, part) in enumerate(combos):
        if c + 1 < len(combos):
            pending.append(scores_of(*combos[c + 1]))
        outs.append(attend(h, *pending.pop(0)))
        if part == 1:
            lo = h * HEAD_DIM
            hm = (lane >= lo) & (lane < lo + HEAD_DIM)
            a = jnp.where(hm, outs[-2] - lam * outs[-1], 0.0)
            ms = jnp.sum(a * a, axis=-1, keepdims=True) * (1.0 / HEAD_DIM)
            acc_ref[...] += a * lax.rsqrt(ms + NORM_EPS)
    return acc_ref[...] * subln * (1.0 - lam_init)


def _na_kernel(q_ref, k_ref, v_ref, kc_ref, vc_ref, t_ref, o_ref, *, grid_rows):
    tile = pl.program_id(1)
    q_row0 = tile * NA_Q_ROWS
    key_row0 = jnp.clip(q_row0 - NA_ROWS // 2, 0, grid_rows - NA_KEY_ROWS)
    ks = pl.multiple_of(key_row0 * GRID_W, GRID_W)
    n_win = NA_KEY_ROWS * GRID_W
    kwin = k_ref[pl.ds(ks, n_win), :]
    vwin = v_ref[pl.ds(ks, n_win), :]

    def bias_of(h):
        left = lax.broadcasted_iota(jnp.int32, (1, 2 * GRID_W), 1) < GRID_W
        rows = []
        for j in range(NA_Q_ROWS):
            start = jnp.clip(q_row0 + j - NA_ROWS // 2, 0, grid_rows - NA_ROWS)
            in_rows = lambda key_row: (key_row >= start) & (key_row < start + NA_ROWS)
            blocks = []
            for i in range(0, NA_KEY_ROWS, 2):
                key_row = key_row0 + i
                rel = key_row - (q_row0 + j) + NA_ROWS - 1
                pair = t_ref[h, jnp.clip(rel + 1, 0, 2 * NA_ROWS - 1)]
                if i + 1 < NA_KEY_ROWS:
                    keep = jnp.where(left, in_rows(key_row).astype(jnp.int32),
                                     in_rows(key_row + 1).astype(jnp.int32))
                    blocks.append(jnp.where(keep != 0, pair, MASK_VALUE))
                else:
                    blocks.append(jnp.where(in_rows(key_row), pair[:, :GRID_W], MASK_VALUE))
            rows.append(jnp.concatenate(blocks, axis=-1))
        return jnp.concatenate(rows, axis=0)

    acc = _na_heads(q_ref[...], [kwin, kc_ref[...]], lambda h: [bias_of(h), None],
                    [vwin, vc_ref[...]])
    o_ref[...] = acc.astype(BF16)


def _na_call(qkv, rel_tab, n_batch, seq, total_rows, ctx_len):
    gw = GROUP_WIDTH
    qt = Q_TILE
    n_q = seq // qt
    grid_rows = seq // GRID_W
    ctx_blk0 = n_batch * seq // ctx_len
    return pl.pallas_call(
        functools.partial(_na_kernel, grid_rows=grid_rows),
        grid=(n_batch, n_q),
        in_specs=[pl.BlockSpec((qt, gw), lambda b, j: (b * n_q + j, 0)),
                  pl.BlockSpec((seq, gw), lambda b, j: (b, 1)),
                  pl.BlockSpec((seq, gw), lambda b, j: (b, 2)),
                  pl.BlockSpec((ctx_len, gw), lambda b, j: (ctx_blk0 + b, 1)),
                  pl.BlockSpec((ctx_len, gw), lambda b, j: (ctx_blk0 + b, 2)),
                  _full(rel_tab.shape)],
        out_specs=pl.BlockSpec((qt, gw), lambda b, j: (b * n_q + j, 0)),
        out_shape=jax.ShapeDtypeStruct((total_rows, gw), BF16),
        compiler_params=_cparams(2),
        name="na_attn",
    )(qkv, qkv, qkv, qkv, qkv, rel_tab)


def _df_kernel(q_ref, k_ref, v_ref, kc_ref, vc_ref, lam_ref, subln_ref, o_ref, acc_ref, *, lam_init):
    lam = _lambda(lam_ref, lam_init)
    acc = _diff_heads(q_ref[...], [kc_ref[...], k_ref[...]], [vc_ref[...], v_ref[...]], lam,
                      subln_ref[...], lam_init, acc_ref)
    o_ref[...] = acc.astype(BF16)


def _df_call(qkv, lam_vecs, subln, lam_init, n_batch, seq, total_rows, ctx_len):
    gw = GROUP_WIDTH
    qt = DIFF_Q_TILE
    n_q = seq // qt
    ctx_blk0 = n_batch * seq // ctx_len
    return pl.pallas_call(
        functools.partial(_df_kernel, lam_init=lam_init),
        grid=(n_batch, n_q),
        in_specs=[pl.BlockSpec((qt, gw), lambda b, j: (b * n_q + j, 3)),
                  pl.BlockSpec((seq, gw), lambda b, j: (b, 4)),
                  pl.BlockSpec((seq, gw), lambda b, j: (b, 5)),
                  pl.BlockSpec((ctx_len, gw), lambda b, j: (ctx_blk0 + b, 4)),
                  pl.BlockSpec((ctx_len, gw), lambda b, j: (ctx_blk0 + b, 5)),
                  _full(lam_vecs.shape),
                  _full(subln.shape)],
        out_specs=pl.BlockSpec((qt, gw), lambda b, j: (b * n_q + j, 0)),
        out_shape=jax.ShapeDtypeStruct((total_rows, gw), BF16),
        scratch_shapes=[pltpu.VMEM((qt, gw), F32)],
        compiler_params=_cparams(2),
        name="diff_attn",
    )(qkv, qkv, qkv, qkv, qkv, lam_vecs, subln)


def _ctx_attn_kernel(qkv_ref, lam_ref, subln_ref, ya_in, yd_in, ya_ref, yd_ref, acc_ref, *, lam_init):
    del ya_in, yd_in
    gw = GROUP_WIDTH
    col = lambda g: qkv_ref[:, g * gw:(g + 1) * gw]
    ya = _na_heads(col(0), [col(1)], lambda h: [None], [col(2)])
    ya_ref[...] = ya.astype(BF16)
    lam = _lambda(lam_ref, lam_init)
    yd = _diff_heads(col(3), [col(4)], [col(5)], lam, subln_ref[...], lam_init, acc_ref)
    yd_ref[...] = yd.astype(BF16)


def _ctx_attn_call(qkv, lam_vecs, subln, lam_init, ya, yd, n_batch, seq, ctx_len):
    gw = GROUP_WIDTH
    ctx_blk0 = n_batch * seq // ctx_len
    any_spec = pl.BlockSpec(memory_space=pl.ANY)
    out_spec = pl.BlockSpec((ctx_len, gw), lambda b: (ctx_blk0 + b, 0))
    return pl.pallas_call(
        functools.partial(_ctx_attn_kernel, lam_init=lam_init),
        grid=(n_batch,),
        in_specs=[pl.BlockSpec((ctx_len, 6 * gw), lambda b: (ctx_blk0 + b, 0)),
                  _full(lam_vecs.shape), _full(subln.shape), any_spec, any_spec],
        out_specs=[out_spec, out_spec],
        out_shape=[jax.ShapeDtypeStruct(ya.shape, BF16), jax.ShapeDtypeStruct(yd.shape, BF16)],
        input_output_aliases={3: 0, 4: 1},
        scratch_shapes=[pltpu.VMEM((ctx_len, gw), F32)],
        compiler_params=_cparams(1),
        name="ctx_attn",
    )(qkv, lam_vecs, subln, ya, yd)


def _pool_kernel(p_ref, w_ref, scale_ref, *rest, seq):
    o_ref, pad_ref = rest[-2], rest[-1]
    gw = GROUP_WIDTH
    halo = POOL_HALO
    pad_ref[0:halo, :] = jnp.zeros((halo, gw), F32)
    pad_ref[halo + seq:, :] = jnp.zeros((halo, gw), F32)
    pad_ref[halo:halo + seq, :] = p_ref[...]
    chunk = min(seq, 256)
    lane_group = lax.broadcasted_iota(jnp.int32, (1, gw), 1) // (gw // len(POOL_WINDOWS))
    for c0 in range(0, seq, chunk):
        def at(off):
            return pad_ref[halo + c0 + off:halo + c0 + off + chunk, :]
        pos = c0 + lax.broadcasted_iota(jnp.int32, (chunk, 1), 0)
        x = at(0)
        run = x
        mean = None
        lo_done, hi_done = 0, 0
        for g, win in enumerate(POOL_WINDOWS):
            half = win // 2
            for off in range(-half, -lo_done):
                run = run + at(off)
            for off in range(hi_done + 1, half):
                run = run + at(off)
            lo_done, hi_done = half, half - 1
            cnt = (jnp.minimum(pos + half, seq) - jnp.maximum(pos - half, 0)).astype(F32)
            m = run / cnt
            mean = m if mean is None else jnp.where(lane_group == g, m, mean)
        y = _dot((mean - x).astype(BF16), w_ref[...]) * scale_ref[...]
        o_ref[c0:c0 + chunk, :] = y.astype(BF16)


def _pool_call(pool_in, w_bd, scale, n_seq, seq, row_blk0, total_rows, prev=None):
    gw = GROUP_WIDTH
    in_specs = [pl.BlockSpec((seq, gw), lambda b: (row_blk0 + b, 0)), _full((gw, gw)), _full((1, gw))]
    args = [pool_in, w_bd, scale]
    aliases = {}
    if prev is not None:
        in_specs.append(pl.BlockSpec(memory_space=pl.ANY))
        args.append(prev)
        aliases = {3: 0}
    return pl.pallas_call(
        functools.partial(_pool_kernel, seq=seq),
        grid=(n_seq,),
        in_specs=in_specs,
        out_specs=pl.BlockSpec((seq, gw), lambda b: (row_blk0 + b, 0)),
        out_shape=jax.ShapeDtypeStruct((total_rows, gw), BF16),
        scratch_shapes=[pltpu.VMEM((seq + 2 * POOL_HALO, gw), F32)],
        input_output_aliases=aliases,
        compiler_params=_cparams(1),
        name="pool_mix",
    )(*args)


def _fft_kernel(t_ref, cl_ref, sl_ref, cc_ref, sc_ref, w_ref, *rest, norm):
    o_ref, a_ref, b_ref = rest[-3], rest[-2], rest[-1]

    @pl.when(pl.program_id(1) == 0)
    def _():
        t = t_ref[...]
        a_ref[...] = _dot(t, cc_ref[...]).astype(BF16)
        b_ref[...] = _dot(t, sc_ref[...]).astype(BF16)

    f = (_dot(cl_ref[...], a_ref[...]) - _dot(sl_ref[...], b_ref[...])) * norm
    o_ref[...] = _dot(f.astype(BF16), w_ref[...]).astype(BF16)


def _fft_call(fft_in, cl, sl, cc_bd, sc_bd, w_bd, n_seq, seq, row_blk0, total_rows, prev=None):
    gw = GROUP_WIDTH
    tk = min(FFT_TILE, seq)
    n_k = seq // tk
    in_specs = [pl.BlockSpec((seq, gw), lambda b, k: (row_blk0 + b, 0)),
                pl.BlockSpec((tk, seq), lambda b, k: (k, 0)),
                pl.BlockSpec((tk, seq), lambda b, k: (k, 0)),
                _full((gw, gw)), _full((gw, gw)), _full((gw, gw))]
    args = [fft_in, cl, sl, cc_bd, sc_bd, w_bd]
    aliases = {}
    if prev is not None:
        in_specs.append(pl.BlockSpec(memory_space=pl.ANY))
        args.append(prev)
        aliases = {6: 0}
    norm = 1.0 / math.sqrt(seq * (gw // 4))
    return pl.pallas_call(
        functools.partial(_fft_kernel, norm=norm),
        grid=(n_seq, n_k),
        in_specs=in_specs,
        out_specs=pl.BlockSpec((tk, gw), lambda b, k: ((row_blk0 + b) * n_k + k, 0)),
        out_shape=jax.ShapeDtypeStruct((total_rows, gw), BF16),
        scratch_shapes=[pltpu.VMEM((seq, gw), BF16), pltpu.VMEM((seq, gw), BF16)],
        input_output_aliases=aliases,
        compiler_params=_cparams(2),
        name="fourier_mix",
    )(*args)


def _out_kernel(ya_ref, yd_ref, yb_ref, yf_ref, w_ref, xa_ref, xb_ref, g1_ref, sh2_ref, sc2_ref, gn2_ref,
                rw_ref, rb_ref, x1_ref, h2_ref, lg_ref, *, n_lat_tiles):
    gw = GROUP_WIDTH
    x = jnp.where(pl.program_id(0) < n_lat_tiles, xa_ref[...], xb_ref[...])
    acc = _dot(ya_ref[...], w_ref[0:gw, :])
    acc = acc + _dot(yd_ref[...], w_ref[gw:2 * gw, :])
    acc = acc + _dot(yb_ref[...], w_ref[2 * gw:3 * gw, :])
    acc = acc + _dot(yf_ref[...], w_ref[3 * gw:4 * gw, :])
    x1 = x + g1_ref[0] * acc
    x1_ref[...] = x1
    ms = jnp.mean(x1 * x1, axis=-1, keepdims=True)
    h2 = x1 * lax.rsqrt(ms + NORM_EPS) * gn2_ref[...] * (1.0 + sc2_ref[0]) + sh2_ref[0]
    h_hi = h2.astype(BF16)
    h_lo = (h2 - h_hi.astype(F32)).astype(BF16)
    by_hi = _dot(h_hi, rw_ref[...])
    by_lo = _dot(h_lo, rw_ref[:, :LANES])
    lg_ref[...] = by_hi[:, :LANES] + by_hi[:, LANES:] + by_lo + rb_ref[...]
    tm = x1.shape[0]
    for s in range(SLABS):
        h2_ref[pl.ds(s, tm, stride=SLABS), :] = h2[:, s * LANES:(s + 1) * LANES]


def _out_call(ya, yd, yb, yf, w_out_bf, xa, xb, ctx_blk0, mod3, gn2, rw_pad, rb_pad, rows, n_lat_rows, seq):
    d = D_MODEL
    gw = GROUP_WIDTH
    tm = ROW_TILE
    n_lat_tiles = n_lat_rows // tm
    tiles_per_seq = seq // tm
    n_batch = n_lat_rows // seq

    def mod_row(i):
        return jnp.minimum(i // tiles_per_seq, n_batch)

    mix_spec = pl.BlockSpec((tm, gw), lambda i: (i, 0))
    mod_spec = lambda chunk: pl.BlockSpec((1, 1, d), lambda i: (mod_row(i), 0, chunk))
    return pl.pallas_call(
        functools.partial(_out_kernel, n_lat_tiles=n_lat_tiles),
        grid=(rows // tm,),
        in_specs=[mix_spec, mix_spec, mix_spec, mix_spec, _full((4 * gw, d))]
                 + _stream_specs(tm, d, n_lat_tiles, ctx_blk0) + [
                  mod_spec(2), mod_spec(3), mod_spec(4), _full((1, d)),
                  _full(rw_pad.shape), _full(rb_pad.shape)],
        out_specs=[pl.BlockSpec((tm, d), lambda i: (i, 0)),
                   pl.BlockSpec((tm * SLABS, LANES), lambda i: (i, 0)),
                   pl.BlockSpec((tm, LANES), lambda i: (i, 0))],
        out_shape=[jax.ShapeDtypeStruct((rows, d), F32),
                   jax.ShapeDtypeStruct((rows * SLABS, LANES), F32),
                   jax.ShapeDtypeStruct((rows, LANES), F32)],
        compiler_params=_cparams(1),
        name="out_proj",
    )(ya, yd, yb, yf, w_out_bf, xa, xb, mod3, mod3, mod3, gn2, rw_pad, rb_pad)


def _route_kernel(lg_ref, gate_ref, code_ref, cnt_ref, carry_ref):
    i = pl.program_id(0)

    @pl.when(i == 0)
    def _():
        carry_ref[...] = jnp.zeros_like(carry_ref)

    lg = lg_ref[...]
    tm = lg.shape[0]
    lane = lax.broadcasted_iota(jnp.int32, (tm, LANES), 1)
    vals, idxs = [], []
    onehot = jnp.zeros((tm, LANES), F32)
    for _ in range(TOP_K):
        m = lg.max(axis=-1, keepdims=True)
        idx = jnp.min(jnp.where(lg == m, lane, LANES), axis=-1, keepdims=True)
        sel = lane == idx
        onehot = onehot + sel.astype(F32)
        lg = jnp.where(sel, -jnp.inf, lg)
        vals.append(m)
        idxs.append(idx)
    exps = [jnp.exp(v - vals[0]) for v in vals]
    denom = exps[0] + exps[1] + exps[2] + exps[3]
    r_i = lax.broadcasted_iota(jnp.int32, (tm, tm), 0)
    c_i = lax.broadcasted_iota(jnp.int32, (tm, tm), 1)
    tri = (c_i < r_i).astype(BF16)
    before = _dot(tri, onehot.astype(BF16)) + carry_ref[...]
    col4 = lax.broadcasted_iota(jnp.int32, (tm, TOP_K), 1)
    gate_o = jnp.zeros((tm, TOP_K), F32)
    code_o = jnp.zeros((tm, TOP_K), F32)
    for k in range(TOP_K):
        rk = jnp.sum(jnp.where(lane == idxs[k], before, 0.0), axis=-1, keepdims=True)
        gate_o = jnp.where(col4 == k, exps[k] / denom, gate_o)
        code_o = jnp.where(col4 == k, idxs[k].astype(F32) * float(RANK_STRIDE) + rk, code_o)
    gate_ref[...] = gate_o
    code_ref[...] = code_o.astype(jnp.int32)
    carry_ref[...] = carry_ref[...] + jnp.sum(onehot, axis=0, keepdims=True)
    cnt_ref[...] = carry_ref[...].astype(jnp.int32)


def _route_call(logits):
    n = logits.shape[0]
    tm = ROW_TILE
    k_spec = pl.BlockSpec((tm, TOP_K), lambda i: (i, 0))
    return pl.pallas_call(
        _route_kernel,
        grid=(n // tm,),
        in_specs=[pl.BlockSpec((tm, LANES), lambda i: (i, 0))],
        out_specs=[k_spec, k_spec, _full((1, LANES))],
        out_shape=[jax.ShapeDtypeStruct((n, TOP_K), F32),
                   jax.ShapeDtypeStruct((n, TOP_K), jnp.int32),
                   jax.ShapeDtypeStruct((1, LANES), jnp.int32)],
        scratch_shapes=[pltpu.VMEM((1, LANES), F32)],
        compiler_params=_cparams(1),
        name="route",
    )(logits)


def _row_copy(src_ref, src_row, dst_ref, dst_row, sem):
    return pltpu.make_async_copy(
        src_ref.at[pl.ds(pl.multiple_of(src_row * SLABS, SLABS), SLABS), :],
        dst_ref.at[pl.ds(pl.multiple_of(dst_row * SLABS, SLABS), SLABS), :], sem)


def _pad_copy(zero_ref, hs_ref, start_row, n_rows, sem):
    return pltpu.make_async_copy(
        zero_ref.at[pl.ds(0, n_rows * SLABS), :],
        hs_ref.at[pl.ds(pl.multiple_of(start_row * SLABS, SLABS), n_rows * SLABS), :], sem)


def _dispatch_kernel(fill_ref, pad_ref, dest_hbm, h2_hbm, hs_ref,
                     idx0, idx1, idx2, rows0, rows1, rows2, zero_ref, zsem, isems, lsems, ssems):
    i = pl.program_id(0)
    n_tiles = pl.num_programs(0)
    tt = TOKEN_TILE
    n_assign = tt * TOP_K
    idx_refs, row_refs = (idx0, idx1, idx2), (rows0, rows1, rows2)

    def idx_copy(tile, s):
        return pltpu.make_async_copy(dest_hbm.at[pl.ds(tile * n_assign, n_assign)], idx_refs[s],
                                     isems.at[s])

    def row_load(tile, s):
        return pltpu.make_async_copy(h2_hbm.at[pl.ds(tile * (tt * SLABS), tt * SLABS), :], row_refs[s],
                                     lsems.at[s])

    def wait_sent(s):
        for _ in range(TOP_K):
            pltpu.make_async_copy(row_refs[s], hs_ref.at[pl.ds(0, tt * SLABS), :], ssems.at[s]).wait()

    @pl.when(i == 0)
    def _():
        idx_copy(0, 0).start()
        row_load(0, 0).start()
        zero_ref[...] = jnp.zeros_like(zero_ref)
        bits = [1 << b for b in reversed(range(int(math.log2(EXPERT_BLOCK))))]
        for phase in ("start", "wait"):
            def fill(e, carry, phase=phase):
                pos = fill_ref[e]
                pad = pad_ref[e]
                for bit in bits:
                    @pl.when((pad & bit) != 0)
                    def _(pos=pos, bit=bit):
                        cp = _pad_copy(zero_ref, hs_ref, pos, bit, zsem)
                        cp.start() if phase == "start" else cp.wait()
                    pos = pos + (pad & bit)
                return carry

            lax.fori_loop(0, N_EXPERTS, fill, 0)

    for s in range(3):
        @pl.when(i % 3 == s)
        def _(s=s):
            nxt, prev = (s + 1) % 3, (s + 2) % 3

            @pl.when(i >= 2)
            def _():
                wait_sent(nxt)

            @pl.when(i + 1 < n_tiles)
            def _():
                idx_copy(i + 1, nxt).start()
                row_load(i + 1, nxt).start()

            idx_copy(i, s).wait()
            row_load(i, s).wait()

            def start(t, carry):
                for k in range(TOP_K):
                    _row_copy(row_refs[s], t, hs_ref, idx_refs[s][t * TOP_K + k],
                              ssems.at[s]).start(priority=k % 2)
                return carry

            lax.fori_loop(0, tt, start, 0, unroll=ISSUE_UNROLL)

            @pl.when(i == n_tiles - 1)
            def _():
                @pl.when(i >= 1)
                def _():
                    wait_sent(prev)
                wait_sent(s)


def _dispatch_call(fill_start, pad_len, dest_flat, h2_slabs, n_slots):
    n = h2_slabs.shape[0] // SLABS
    tt = TOKEN_TILE
    any_spec = pl.BlockSpec(memory_space=pl.ANY)
    grid_spec = pltpu.PrefetchScalarGridSpec(
        num_scalar_prefetch=2,
        grid=(n // tt,),
        in_specs=[any_spec, any_spec],
        out_specs=any_spec,
        scratch_shapes=[pltpu.SMEM((tt * TOP_K,), jnp.int32)] * 3
                       + [pltpu.VMEM((tt * SLABS, LANES), F32)] * 3
                       + [pltpu.VMEM((EXPERT_BLOCK // 2 * SLABS, LANES), F32),
                          pltpu.SemaphoreType.DMA, pltpu.SemaphoreType.DMA((3,)),
                          pltpu.SemaphoreType.DMA((3,)), pltpu.SemaphoreType.DMA((3,))],
    )
    return pl.pallas_call(
        _dispatch_kernel,
        grid_spec=grid_spec,
        out_shape=jax.ShapeDtypeStruct((n_slots * SLABS, LANES), F32),
        compiler_params=_cparams(1),
        name="dispatch",
    )(fill_start, pad_len, dest_flat, h2_slabs)


GLU_BLOCK = 2 * LANES


def _expert_kernel(be_ref, first_ref, nv_ref, hs_ref, w1_ref, b1_ref, w2_ref, b2_ref, perm_ref,
                   ys_ref, h_ref, act_ref, w1p_ref, w2p_ref):
    del be_ref
    i = pl.program_id(0)
    n_glu = 2 * D_FF // GLU_BLOCK

    @pl.when((first_ref[i] != 0) & (i < nv_ref[0]))
    def _():
        for b in range(n_glu):
            cols = slice(b * GLU_BLOCK, (b + 1) * GLU_BLOCK)
            w1p_ref[:, cols] = _dot(w1_ref[0, :, cols].astype(BF16), perm_ref[...]).astype(BF16)
        w2p_ref[...] = w2_ref[0].astype(BF16)

    @pl.when(i < nv_ref[0])
    def _():
        blk = EXPERT_BLOCK
        for s in range(SLABS):
            h_ref[:, s * LANES:(s + 1) * LANES] = hs_ref[pl.ds(s, blk, stride=SLABS), :].astype(BF16)
        h = h_ref[...]
        for b in range(n_glu):
            cols = slice(b * GLU_BLOCK, (b + 1) * GLU_BLOCK)
            u = _dot(h, w1p_ref[:, cols]) + b1_ref[0, :, cols]
            glu = jnp.minimum(u[:, :LANES], SWIGLU_LIMIT)
            lin = jnp.clip(u[:, LANES:], -SWIGLU_LIMIT, SWIGLU_LIMIT)
            act = glu * jax.nn.sigmoid(SWIGLU_ALPHA * glu) * (lin + 1.0)
            act_ref[:, b * LANES:(b + 1) * LANES] = act.astype(BF16)
        y = _dot(act_ref[...], w2p_ref[...]) + b2_ref[0]
        for s in range(SLABS):
            ys_ref[pl.ds(s, blk, stride=SLABS), :] = y[:, s * LANES:(s + 1) * LANES]


def _expert_call(block_e, first, n_valid, hs, w1, b1p, w2, b2, perm, layer, n_blocks):
    blk = EXPERT_BLOCK
    d = D_MODEL

    def row_blk(i, be, first, nv):
        return (jnp.minimum(i, nv[0] - 1), 0)

    def expert_blk(i, be, first, nv):
        return (be[i], 0, 0)

    def weight_blk(i, be, first, nv):
        return (layer, be[i], 0, 0)

    grid_spec = pltpu.PrefetchScalarGridSpec(
        num_scalar_prefetch=3,
        grid=(n_blocks,),
        in_specs=[pl.BlockSpec((blk * SLABS, LANES), row_blk),
                  pl.BlockSpec((None, 1, d, 2 * D_FF), weight_blk),
                  pl.BlockSpec((1, 1, 2 * D_FF), expert_blk),
                  pl.BlockSpec((None, 1, D_FF, d), weight_blk),
                  pl.BlockSpec((1, 1, d), expert_blk),
                  pl.BlockSpec((GLU_BLOCK, GLU_BLOCK), lambda i, *_: (0, 0))],
        out_specs=pl.BlockSpec((blk * SLABS, LANES), row_blk),
        scratch_shapes=[pltpu.VMEM((blk, d), BF16), pltpu.VMEM((blk, D_FF), BF16),
                        pltpu.VMEM((d, 2 * D_FF), BF16), pltpu.VMEM((D_FF, d), BF16)],
    )
    return pl.pallas_call(
        _expert_kernel,
        grid_spec=grid_spec,
        out_shape=jax.ShapeDtypeStruct(hs.shape, F32),
        compiler_params=_cparams(1, EXPERT_VMEM_LIMIT),
        name="expert_ffn",
    )(block_e, first, n_valid, hs, w1, b1p, w2, b2, perm)


def _combine_kernel(dest_hbm, ys_hbm, gate_ref, x_ref, g2_ref, o_ref,
                    idx0, idx1, buf0, buf1, sems, isems):
    i = pl.program_id(0)
    n_tiles = pl.num_programs(0)
    tt = COMBINE_TILE
    n_assign = tt * TOP_K
    idx_refs, buf_refs = (idx0, idx1), (buf0, buf1)

    def idx_copy(tile, s):
        return pltpu.make_async_copy(dest_hbm.at[pl.ds(tile * n_assign, n_assign)], idx_refs[s],
                                     isems.at[s])

    def issue(s):
        def start(t, carry):
            for k in range(TOP_K):
                _row_copy(ys_hbm, idx_refs[s][t * TOP_K + k], buf_refs[s], k * tt + t,
                          sems.at[s]).start(priority=k % 2)
            return carry
        lax.fori_loop(0, tt, start, 0, unroll=ISSUE_UNROLL)

    @pl.when(i == 0)
    def _():
        idx_copy(0, 0).start()
        idx_copy(0, 0).wait()
        issue(0)

        @pl.when(n_tiles > 1)
        def _():
            idx_copy(1, 1).start()

    for s in (0, 1):
        @pl.when(i % 2 == s)
        def _(s=s):
            @pl.when(i + 1 < n_tiles)
            def _():
                idx_copy(i + 1, 1 - s).wait()
                issue(1 - s)

            pltpu.make_async_copy(ys_hbm.at[pl.ds(0, n_assign * SLABS), :], buf_refs[s], sems.at[s]).wait()

            @pl.when(i + 2 < n_tiles)
            def _():
                idx_copy(i + 2, s).start()

            gate = gate_ref[...]
            g2 = g2_ref[0]
            for sl in range(SLABS):
                cols = slice(sl * LANES, (sl + 1) * LANES)
                y = jnp.zeros((tt, LANES), F32)
                for k in range(TOP_K):
                    y = y + gate[:, k:k + 1] * buf_refs[s][pl.ds(k * tt * SLABS + sl, tt, stride=SLABS), :]
                o_ref[:, cols] = x_ref[:, cols] + g2[:, cols] * y


def _combine_call(dest_flat, ys, gate, x1, mod3, n_lat_rows, seq):
    n, d = x1.shape
    tt = COMBINE_TILE
    tiles_per_seq = seq // tt
    n_batch = n_lat_rows // seq

    def mod_row(i):
        return jnp.minimum(i // tiles_per_seq, n_batch)

    return pl.pallas_call(
        _combine_kernel,
        grid=(n // tt,),
        in_specs=[pl.BlockSpec(memory_space=pl.ANY), pl.BlockSpec(memory_space=pl.ANY),
                  pl.BlockSpec((tt, TOP_K), lambda i: (i, 0)),
                  pl.BlockSpec((tt, d), lambda i: (i, 0)),
                  pl.BlockSpec((1, 1, d), lambda i: (mod_row(i), 0, 5))],
        out_specs=pl.BlockSpec((tt, d), lambda i: (i, 0)),
        out_shape=jax.ShapeDtypeStruct((n, d), F32),
        scratch_shapes=[pltpu.SMEM((tt * TOP_K,), jnp.int32), pltpu.SMEM((tt * TOP_K,), jnp.int32),
                        pltpu.VMEM((TOP_K * tt * SLABS, LANES), F32),
                        pltpu.VMEM((TOP_K * tt * SLABS, LANES), F32),
                        pltpu.SemaphoreType.DMA((2,)), pltpu.SemaphoreType.DMA((2,))],
        compiler_params=_cparams(1),
        name="combine",
    )(dest_flat, ys, gate, x1, mod3)


def _block_diag(blocks):
    g, a, b = blocks.shape
    eye = jnp.eye(g, dtype=blocks.dtype)
    return (eye[:, None, :, None] * blocks[:, :, None, :]).reshape(g * a, g * b)


def _rope_tables(seq, extra_rows):
    half = DIFF_QK_DIM // 2
    inv = ROPE_THETA ** (-jnp.arange(0, half, 2, dtype=F32) / half)
    pos = jnp.arange(seq)
    c = jnp.arange(GROUP_WIDTH)
    dd = c % DIFF_QK_DIM
    axis_pos = jnp.where((dd < half)[None, :], (pos // GRID_W)[:, None], (pos % GRID_W)[:, None])
    ang = axis_pos.astype(F32) * inv[dd % (half // 2)][None, :]
    sign = jnp.where((dd % half) < half // 2, -1.0, 1.0).astype(F32)
    cos_t = jnp.concatenate([jnp.cos(ang), jnp.ones((extra_rows, GROUP_WIDTH), F32)], axis=0)
    sin_t = jnp.concatenate([jnp.sin(ang) * sign[None, :], jnp.zeros((extra_rows, GROUP_WIDTH), F32)], axis=0)
    return cos_t, sin_t


def _dft_tables(n, dtype):
    k = np.arange(n, dtype=np.int64)
    ang = ((k[:, None] * k[None, :]) % n).astype(np.float64) * (2.0 * math.pi / n)
    return jnp.asarray(np.cos(ang), dtype), jnp.asarray(np.sin(ang), dtype)


def _na_rel_tables(rpb):
    w = GRID_W
    col = np.arange(w)
    col_start = np.clip(col - NA_COLS // 2, 0, w - NA_COLS)
    in_win = (col[None, :] >= col_start[:, None]) & (col[None, :] < col_start[:, None] + NA_COLS)
    edge = w - NA_COLS
    ext = jnp.pad(rpb.astype(F32), ((0, 0), (0, 0), (edge, edge)), mode='edge')
    t = jnp.stack([ext[:, :, w - 1 - q:2 * w - 1 - q] for q in range(w)], axis=2)
    t = jnp.where(in_win[None, None], t, MASK_VALUE)
    masked = jnp.full_like(t[:, :1], MASK_VALUE)
    padded = jnp.concatenate([masked, t, masked], axis=1)
    return jnp.concatenate([padded[:, :-1], padded[:, 1:]], axis=-1)


def _moe(h2_slabs, logits, x1, mod3, w1, b1p, w2, b2, perm, layer, n_lat_rows, seq):
    n = x1.shape[0]
    blk = EXPERT_BLOCK
    assert n * TOP_K <= RANK_STRIDE
    gate, code, counts = _route_call(logits)
    counts = counts[0, :N_EXPERTS]
    padded = (counts + blk - 1) // blk * blk
    padded_end = jnp.cumsum(padded)
    padded_start = padded_end - padded
    codes = code.reshape(1, n * TOP_K)
    experts = lax.shift_right_logical(codes, RANK_SHIFT)
    first_slot = jnp.sum(jnp.where(experts == jnp.arange(N_EXPERTS, dtype=jnp.int32)[:, None],
                                   padded_start.astype(jnp.int32)[:, None], 0), axis=0)
    dest = first_slot + (codes[0] & RANK_MASK)
    n_blocks = n * TOP_K // blk + N_EXPERTS
    block_row0 = jnp.arange(n_blocks, dtype=jnp.int32) * blk
    block_e = jnp.minimum(jnp.sum((padded_end[None, :] <= block_row0[:, None]).astype(jnp.int32), axis=1),
                          N_EXPERTS - 1).astype(jnp.int32)
    first = jnp.concatenate([jnp.ones((1,), jnp.int32),
                             (block_e[1:] != block_e[:-1]).astype(jnp.int32)])
    n_valid = (padded_end[-1:] // blk).astype(jnp.int32)
    fill_start = (padded_start + counts).astype(jnp.int32)
    pad_len = (padded - counts).astype(jnp.int32)
    hs = _dispatch_call(fill_start, pad_len, dest, h2_slabs, n_blocks * blk)
    ys = _expert_call(block_e, first, n_valid, hs, w1, b1p, w2, b2, perm, layer, n_blocks)
    return _combine_call(dest, ys, gate, x1, mod3, n_lat_rows, seq)


def kernel(x, c, ctx, c_ctx, w_ada, b_ada, g_norm1, g_norm2, w_in, w_out, na_q_gain, na_k_gain, na_rpb, diff_q_gain, diff_k_gain, diff_lambda_q1, diff_lambda_k1, diff_lambda_q2, diff_lambda_k2, diff_subln, pool_w, pool_scale, fft_w, router_w, router_b, moe_w1, moe_b1, moe_w2, moe_b2):
    n_batch, seq, d = x.shape
    ctx_len = ctx.shape[1]
    depth = w_ada.shape[0]
    gw = GROUP_WIDTH
    assert d == D_MODEL and seq % ROW_TILE == 0 and seq % ctx_len == 0 and seq % TOKEN_TILE == 0
    assert seq % COMBINE_TILE == 0 and (n_batch * ctx_len) % COMBINE_TILE == 0
    assert (n_batch * ctx_len) % ROW_TILE == 0 and (seq // GRID_W) >= NA_KEY_ROWS
    n_lat = n_batch * seq
    n_ctx = n_batch * ctx_len

    xa, xb, ctx_blk0 = x.reshape(n_lat, d), ctx.reshape(n_ctx, d), 0
    mod_rows = -(-(n_batch + 1) // SUBLANES) * SUBLANES
    cvec = jnp.zeros((mod_rows, d), F32).at[:n_batch].set(c).at[n_batch].set(c_ctx)
    mod = _ada_call(cvec, w_ada, b_ada)

    cos_t, sin_t = _rope_tables(seq, ROW_TILE)
    ones = lambda w: _block_diag(jnp.ones((gw // w, w, w), BF16))
    g64, g32 = ones(HEAD_DIM), ones(DIFF_QK_DIM)
    cl_lat, sl_lat = _dft_tables(seq, BF16)
    cl_ctx, sl_ctx = _dft_tables(ctx_len, BF16)
    cc, sc = _dft_tables(gw // 4, F32)
    n_grp = fft_w.shape[1]
    cc_bd = _block_diag(jnp.broadcast_to(cc, (n_grp,) + cc.shape)).astype(BF16)
    sc_bd = _block_diag(jnp.broadcast_to(sc, (n_grp,) + sc.shape)).astype(BF16)
    src = np.arange(GLU_BLOCK)
    dst = np.where(src % 2 == 0, src // 2, LANES + src // 2)
    perm_np = np.zeros((GLU_BLOCK, GLU_BLOCK), np.float32)
    perm_np[src, dst] = 1.0
    glu_perm = jnp.asarray(perm_np, BF16)

    tile = lambda v: jnp.tile(v.astype(F32), gw // v.shape[0])

    def in_proj_params(l):
        gains = jnp.stack([tile(na_q_gain[l]) * (HEAD_DIM ** -0.5 * LOG2_E), tile(na_k_gain[l]),
                           tile(diff_q_gain[l]) * (DIFF_QK_DIM ** -0.5 * LOG2_E), tile(diff_k_gain[l])]
                          + [jnp.zeros((gw,), F32)] * 4)
        return (mod[l].reshape(mod_rows, 1, 6 * d), g_norm1[l].reshape(1, d), w_in[l].astype(BF16),
                gains, cos_t, sin_t, g64, g32)

    for l in range(depth):
        ctx_out = l < depth - 1
        lam_init = 0.8 - 0.6 * math.exp(-0.3 * l)
        mod3 = mod[l].reshape(mod_rows, 1, 6 * d)
        qkv, pool_in, fft_in = _in_call(xa, xb, ctx_blk0, n_lat + n_ctx, in_proj_params(l), n_lat, seq)
        lam_vecs = jnp.stack([diff_lambda_q1[l], diff_lambda_k1[l], diff_lambda_q2[l], diff_lambda_k2[l]]
                             + [jnp.zeros_like(diff_lambda_q1[l])] * 4).astype(F32)
        subln = tile(diff_subln[l]).reshape(1, gw)
        rel_tab = _na_rel_tables(na_rpb[l] * LOG2_E)
        pool_bd = _block_diag(pool_w[l]).astype(BF16)
        fftw_bd = _block_diag(fft_w[l]).astype(BF16)
        rw_f32 = jnp.zeros((d, LANES), F32).at[:, :N_EXPERTS].set(router_w[l])
        rw_hi = rw_f32.astype(BF16)
        rw_pad = jnp.concatenate([rw_hi, (rw_f32 - rw_hi.astype(F32)).astype(BF16)], axis=1)
        rb_pad = jnp.full((1, LANES), -jnp.inf, F32).at[0, :N_EXPERTS].set(router_b[l])
        n_e = moe_w1.shape[1]
        b1p = moe_b1[l].reshape(n_e, 2 * D_FF // GLU_BLOCK, LANES, 2).transpose(0, 1, 3, 2)
        b1p = b1p.reshape(n_e, 1, 2 * D_FF)
        b2 = moe_b2[l].reshape(n_e, 1, d)

        rows = n_lat + n_ctx if ctx_out else n_lat
        ya = _na_call(qkv, rel_tab, n_batch, seq, rows, ctx_len)
        yd = _df_call(qkv, lam_vecs, subln, lam_init, n_batch, seq, rows, ctx_len)
        yb = _pool_call(pool_in, pool_bd, pool_scale[l].reshape(1, gw), n_batch, seq, 0, rows)
        yf = _fft_call(fft_in, cl_lat, sl_lat, cc_bd, sc_bd, fftw_bd, n_batch, seq, 0, rows)
        if ctx_out:
            ya, yd = _ctx_attn_call(qkv, lam_vecs, subln, lam_init, ya, yd, n_batch, seq, ctx_len)
            yb = _pool_call(pool_in, pool_bd, pool_scale[l].reshape(1, gw), n_batch, ctx_len,
                            n_lat // ctx_len, rows, prev=yb)
            yf = _fft_call(fft_in, cl_ctx, sl_ctx, cc_bd, sc_bd, fftw_bd, n_batch, ctx_len,
                           n_lat // ctx_len, rows, prev=yf)
        x1, h2_slabs, logits = _out_call(ya, yd, yb, yf, w_out[l].astype(BF16), xa, xb, ctx_blk0, mod3,
                                         g_norm2[l].reshape(1, d), rw_pad, rb_pad, rows, n_lat, seq)
        x2 = _moe(h2_slabs, logits, x1, mod3, moe_w1, b1p, moe_w2, b2, glu_perm, l, n_lat, seq)
        xa, xb, ctx_blk0 = x2, x2, n_lat // ROW_TILE
    return x2[:n_lat].reshape(n_batch, seq, d)
```
